```python
import jax, jax.numpy as jnp
from jax import lax
import numpy as np

D_MODEL = 1024
BATCH = 2
SEQ = 8192
DEPTH = 2
DEC_BATCH = 32
DEC_SEQ = 8
PAST_LEN = 16384
PAGE_SIZE = 128

A_WIDTH = 512
A_GROUPS = 4
A_GROUP_DIM = A_WIDTH // A_GROUPS
CHUNK = 128
B_WIDTH = 512
CONV_W = 31
C_HEADS = 8
C_HEAD_DIM = 64
C_GROUP_WIDTH = C_HEADS * C_HEAD_DIM
C_CONFIGS = ((128, 1), (512, 4), (2048, 16))
C_GROUPS = len(C_CONFIGS)
C_KEYS = C_CONFIGS[0][0] // C_CONFIGS[0][1] + 1
Q_BLOCK = 128
ATTN_SCALE = C_HEAD_DIM ** -0.5
N_BRANCH = 3
BRANCH_WIDTH = 512
D_FF = -(-8 * D_MODEL // (3 * 256)) * 256
EPS = 1e-6
NEG_INF = -1e30
OFF_AU = 0
OFF_AV = OFF_AU + A_WIDTH
OFF_B = OFF_AV + A_WIDTH
OFF_CQ = OFF_B + 2 * B_WIDTH
OFF_CK = OFF_CQ + C_GROUPS * C_GROUP_WIDTH
OFF_CV = OFF_CK + C_GROUPS * C_GROUP_WIDTH
OFF_G = OFF_CV + C_GROUPS * C_GROUP_WIDTH
D_IN = OFF_G + N_BRANCH * D_MODEL

kernel_name = 'hybrid_gmlp_conformer_dilated_attn_step'


def rms_norm(x, g):
    xf = x.astype(jnp.float32)
    y = xf * lax.rsqrt(jnp.mean(xf * xf, axis=-1, keepdims=True) + EPS)
    return (y * g.astype(jnp.float32)).astype(x.dtype)


def layer_norm(x, g, b):
    xf = x.astype(jnp.float32)
    mu = jnp.mean(xf, axis=-1, keepdims=True)
    var = jnp.mean(jnp.square(xf - mu), axis=-1, keepdims=True)
    y = (xf - mu) * lax.rsqrt(var + EPS)
    return (y * g.astype(jnp.float32) + b.astype(jnp.float32)).astype(x.dtype)


def alibi_slopes():
    n = C_GROUPS * C_HEADS
    s = jnp.exp2(-8.0 * (jnp.arange(n, dtype=jnp.float32) + 1.0) / n)
    return s.reshape(C_HEADS, C_GROUPS).T


def chunk_token_mlp(u, vn, w_s, b_s, chunk_len):
    bsz, t, _ = u.shape
    n_chunks = t // chunk_len
    causal = jnp.tril(jnp.ones((CHUNK, CHUNK), dtype=w_s.dtype))
    w = (w_s * causal)[:, :chunk_len, :chunk_len]
    vr = vn.reshape(bsz, n_chunks, chunk_len, A_GROUPS, A_GROUP_DIM)
    mix = jnp.einsum('gts,bcsgd->bctgd', w, vr) + b_s[:, :chunk_len].T[None, None, :, :, None]
    return u * mix.reshape(bsz, t, A_WIDTH)


def conv_module(z, buf, conv_w, conv_b, ln_g, ln_b):
    a, gate = jnp.split(z, 2, axis=-1)
    h = a * jax.nn.sigmoid(gate)
    padded = jnp.concatenate([buf.astype(h.dtype), h], axis=1)
    y = lax.conv_general_dilated(padded, conv_w[:, None, :].astype(h.dtype), window_strides=(1,),
                                 padding='VALID', dimension_numbers=('NWC', 'WIO', 'NWC'),
                                 feature_group_count=B_WIDTH) + conv_b
    y = jax.nn.silu(layer_norm(y, ln_g, ln_b))
    return y, padded[:, -(CONV_W - 1):]


def dilated_window_attention(q, k, v, q_idx, dilation, slopes):
    steps = jnp.arange(C_KEYS)
    idx = q_idx[:, None] - dilation * steps[None, :]
    valid = idx >= 0
    idx = jnp.maximum(idx, 0)
    kg = jnp.take(k, idx, axis=1)
    vg = jnp.take(v, idx, axis=1)
    s = jnp.einsum('bqhe,bqkhe->bqhk', q, kg, preferred_element_type=jnp.float32) * ATTN_SCALE
    s = s - slopes[:, None] * (dilation * steps).astype(jnp.float32)[None, :]
    s = jnp.where(valid[None, :, None, :], s, NEG_INF)
    m = jnp.max(s, axis=-1, keepdims=True)
    p = jnp.exp(s - m)
    den = jnp.sum(p, axis=-1, keepdims=True)
    o = jnp.einsum('bqhk,bqkhe->bqhe', (p / den).astype(v.dtype), vg)
    return o, (m + jnp.log(den))[..., 0]


def merge_dilations(outs, lses):
    w = jax.nn.softmax(jnp.stack(lses, axis=0), axis=0)
    o = jnp.sum(w[..., None] * jnp.stack(outs, axis=0).astype(jnp.float32), axis=0)
    bsz, t = o.shape[0], o.shape[1]
    return o.reshape(bsz, t, C_GROUP_WIDTH).astype(outs[0].dtype)


def dilated_attention_prompt(qs, ks, vs, slopes):
    bsz, t = qs[0].shape[0], qs[0].shape[1]

    def block(i):
        t0 = i * Q_BLOCK
        q_idx = t0 + jnp.arange(Q_BLOCK)
        outs, lses = [], []
        for g, (_, dil) in enumerate(C_CONFIGS):
            qb = lax.dynamic_slice_in_dim(qs[g], t0, Q_BLOCK, axis=1)
            o, l = dilated_window_attention(qb, ks[g], vs[g], q_idx, dil, slopes[g])
            outs.append(o)
            lses.append(l)
        return merge_dilations(outs, lses)

    y = lax.map(block, jnp.arange(t // Q_BLOCK))
    return jnp.swapaxes(y, 0, 1).reshape(bsz, t, C_GROUP_WIDTH)


def dilated_attention_sample(qs, ks, vs, caches, slopes):
    t = qs[0].shape[1]
    outs, lses = [], []
    for g, (_, dil) in enumerate(C_CONFIGS):
        kc, vc = caches[g]
        lb = kc.shape[1]
        k_src = jnp.concatenate([kc.astype(ks[g].dtype), ks[g]], axis=1)
        v_src = jnp.concatenate([vc.astype(vs[g].dtype), vs[g]], axis=1)
        o, l = dilated_window_attention(qs[g], k_src, v_src, lb + jnp.arange(t), dil, slopes[g])
        outs.append(o)
        lses.append(l)
    return merge_dilations(outs, lses)


def mixer_block(xn, w_in, a_norm_g, a_norm_b, a_w_s, a_b_s, b_conv_w, b_conv_b, b_norm_g,
                b_norm_b, w_branch, w_out, is_prompt, conv_buf, caches, slopes):
    bsz, t, _ = xn.shape
    h = xn @ w_in
    u = h[..., OFF_AU:OFF_AV]
    vn = layer_norm(h[..., OFF_AV:OFF_B], a_norm_g, a_norm_b)
    o_a = chunk_token_mlp(u, vn, a_w_s, a_b_s, CHUNK if is_prompt else t)
    if is_prompt:
        conv_buf = jnp.zeros((bsz, CONV_W - 1, B_WIDTH), h.dtype)
    o_b, new_buf = conv_module(h[..., OFF_B:OFF_CQ], conv_buf, b_conv_w, b_conv_b, b_norm_g, b_norm_b)

    def heads(off):
        return [h[..., off + g * C_GROUP_WIDTH: off + (g + 1) * C_GROUP_WIDTH]
                .reshape(bsz, t, C_HEADS, C_HEAD_DIM) for g in range(C_GROUPS)]
    qs, ks, vs = heads(OFF_CQ), heads(OFF_CK), heads(OFF_CV)
    if is_prompt:
        o_c = dilated_attention_prompt(qs, ks, vs, slopes)
        new_kv = [(ks[g][:, t - min(w, t):], vs[g][:, t - min(w, t):])
                  for g, (w, _) in enumerate(C_CONFIGS)]
    else:
        o_c = dilated_attention_sample(qs, ks, vs, caches, slopes)
        new_kv = [(ks[g], vs[g]) for g in range(C_GROUPS)]
    gates = jax.nn.sigmoid(h[..., OFF_G:].reshape(bsz, t, N_BRANCH, D_MODEL))
    branches = jnp.stack([o_a, o_b, o_c], axis=2)
    proj = jnp.einsum('btne,ned->btnd', branches, w_branch.reshape(N_BRANCH, BRANCH_WIDTH, D_MODEL))
    merged = jnp.sum(gates * proj, axis=2)
    return merged @ w_out, new_buf, vn, new_kv


def swiglu(xn, w1, w2):
    gate, up = jnp.split(xn @ w1, 2, axis=-1)
    return (jax.nn.silu(gate) * up) @ w2


def run_trunk(x, is_prompt, conv_state, c_cache, params):
    (norm_pre_mix, norm_post_mix, norm_pre_ffn, norm_post_ffn, w_in, a_norm_g, a_norm_b, a_w_s,
     a_b_s, b_conv_w, b_conv_b, b_norm_g, b_norm_b, w_branch, w_out, ffn_w_in, ffn_w_out) = params
    slopes = alibi_slopes()
    bufs, vns, kvs = [], [], []
    for l in range(DEPTH):
        conv_buf = None if is_prompt else conv_state[l]
        caches = None if is_prompt else [(c_cache[2 * g][l], c_cache[2 * g + 1][l]) for g in range(C_GROUPS)]
        y, buf, vn, kv = mixer_block(rms_norm(x, norm_pre_mix[l]), w_in[l], a_norm_g[l], a_norm_b[l],
                                     a_w_s[l], a_b_s[l], b_conv_w[l], b_conv_b[l], b_norm_g[l],
                                     b_norm_b[l], w_branch[l], w_out[l], is_prompt, conv_buf, caches, slopes)
        x = x + rms_norm(y, norm_post_mix[l])
        x = x + rms_norm(swiglu(rms_norm(x, norm_pre_ffn[l]), ffn_w_in[l], ffn_w_out[l]), norm_post_ffn[l])
        bufs.append(buf)
        vns.append(vn)
        kvs.append(kv)
    new_kv = [jnp.stack([kvs[l][g][j] for l in range(DEPTH)], axis=0)
              for g in range(C_GROUPS) for j in range(2)]
    return x, jnp.stack(bufs, axis=0), jnp.stack(vns, axis=0), new_kv


def setup_inputs(seed: int = 0) -> dict:
    key = jax.random.key(seed)
    ks = jax.random.split(key, 32)
    f32 = jnp.float32

    def nrm(k, shape, scale):
        return scale * jax.random.normal(k, shape, f32)

    def gain(k, shape):
        return 1.0 + 0.02 * jax.random.normal(k, shape, f32)

    def cache_len(w):
        return min(w, PAST_LEN)

    cshape = [(DEPTH, DEC_BATCH, cache_len(w), C_HEADS, C_HEAD_DIM) for (w, _) in C_CONFIGS]
    return {
        'x_prompt': nrm(ks[0], (BATCH, SEQ, D_MODEL), 1.0),
        'x_sample': nrm(ks[1], (DEC_BATCH, DEC_SEQ, D_MODEL), 1.0),
        'state_b_conv': nrm(ks[2], (DEPTH, DEC_BATCH, CONV_W - 1, B_WIDTH), 0.5),
        'cache_c0_k': nrm(ks[3], cshape[0], 1.0),
        'cache_c0_v': nrm(ks[4], cshape[0], 1.0),
        'cache_c1_k': nrm(ks[5], cshape[1], 1.0),
        'cache_c1_v': nrm(ks[6], cshape[1], 1.0),
        'cache_c2_k': nrm(ks[7], cshape[2], 1.0),
        'cache_c2_v': nrm(ks[8], cshape[2], 1.0),
        'norm_pre_mix': gain(ks[9], (DEPTH, D_MODEL)),
        'norm_post_mix': gain(ks[10], (DEPTH, D_MODEL)),
        'norm_pre_ffn': gain(ks[11], (DEPTH, D_MODEL)),
        'norm_post_ffn': gain(ks[12], (DEPTH, D_MODEL)),
        'w_in': nrm(ks[13], (DEPTH, D_MODEL, D_IN), D_MODEL ** -0.5),
        'a_norm_g': gain(ks[14], (DEPTH, A_WIDTH)),
        'a_norm_b': nrm(ks[15], (DEPTH, A_WIDTH), 0.02),
        'a_w_s': nrm(ks[16], (DEPTH, A_GROUPS, CHUNK, CHUNK), CHUNK ** -0.5),
        'a_b_s': gain(ks[17], (DEPTH, A_GROUPS, CHUNK)),
        'b_conv_w': nrm(ks[18], (DEPTH, CONV_W, B_WIDTH), CONV_W ** -0.5),
        'b_conv_b': nrm(ks[19], (DEPTH, B_WIDTH), 0.02),
        'b_norm_g': gain(ks[20], (DEPTH, B_WIDTH)),
        'b_norm_b': nrm(ks[21], (DEPTH, B_WIDTH), 0.02),
        'w_branch': nrm(ks[22], (DEPTH, N_BRANCH * BRANCH_WIDTH, D_MODEL), BRANCH_WIDTH ** -0.5),
        'w_out': nrm(ks[23], (DEPTH, D_MODEL, D_MODEL), D_MODEL ** -0.5),
        'ffn_w_in': nrm(ks[24], (DEPTH, D_MODEL, 2 * D_FF), D_MODEL ** -0.5),
        'ffn_w_out': nrm(ks[25], (DEPTH, D_FF, D_MODEL), D_FF ** -0.5),
    }


def reference(x_prompt, x_sample, state_b_conv, cache_c0_k, cache_c0_v, cache_c1_k, cache_c1_v,
              cache_c2_k, cache_c2_v, norm_pre_mix, norm_post_mix, norm_pre_ffn, norm_post_ffn,
              w_in, a_norm_g, a_norm_b, a_w_s, a_b_s, b_conv_w, b_conv_b, b_norm_g, b_norm_b,
              w_branch, w_out, ffn_w_in, ffn_w_out):
    params = (norm_pre_mix, norm_post_mix, norm_pre_ffn, norm_post_ffn, w_in, a_norm_g, a_norm_b,
              a_w_s, a_b_s, b_conv_w, b_conv_b, b_norm_g, b_norm_b, w_branch, w_out, ffn_w_in, ffn_w_out)
    c_cache = (cache_c0_k, cache_c0_v, cache_c1_k, cache_c1_v, cache_c2_k, cache_c2_v)
    y_prompt, new_b_conv_prompt, _, kv_p = run_trunk(x_prompt, True, None, None, params)
    y_sample, new_b_conv_sample, new_a_v_sample, kv_s = run_trunk(x_sample, False, state_b_conv, c_cache, params)
    (new_c0_k_prompt, new_c0_v_prompt, new_c1_k_prompt, new_c1_v_prompt,
     new_c2_k_prompt, new_c2_v_prompt) = kv_p
    (new_c0_k_sample, new_c0_v_sample, new_c1_k_sample, new_c1_v_sample,
     new_c2_k_sample, new_c2_v_sample) = kv_s
    return (y_prompt, y_sample, new_b_conv_prompt, new_b_conv_sample, new_a_v_sample,
            new_c0_k_prompt, new_c0_v_prompt, new_c1_k_prompt, new_c1_v_prompt,
            new_c2_k_prompt, new_c2_v_prompt,
            new_c0_k_sample, new_c0_v_sample, new_c1_k_sample, new_c1_v_sample,
            new_c2_k_sample, new_c2_v_sample)
```

```python
import functools

import jax
import jax.numpy as jnp
from jax import lax
from jax.experimental import pallas as pl
from jax.experimental.pallas import tpu as pltpu

F32 = jnp.float32
BF16 = jnp.bfloat16

D_MODEL = 1024
DEC_SEQ = 8
A_WIDTH = 512
A_GROUPS = 4
A_GROUP_DIM = 128
CHUNK = 128
B_WIDTH = 512
CONV_W = 31
CONV_HALO = 32
C_HEADS = 8
C_HEAD_DIM = 64
C_GROUP_WIDTH = 512
C_CONFIGS = ((128, 1), (512, 4), (2048, 16))
C_GROUPS = 3
C_WINDOW = 128
N_BRANCH = 3
BRANCH_WIDTH = 512
D_FF = 2816
EPS = 1e-6
NEG_INF = -1e30
ATTN_SCALE = C_HEAD_DIM ** -0.5
REF_OFF_G = 6656
OFF_G = 0
OFF_AU = 3072
OFF_AV = 3584
OFF_B = 4096
OFF_CQ = 5120
OFF_CK = 6656
OFF_CV = 8192
D_IN = 9728
COL_BLOCKS = D_IN // 512

VMEM_LIMIT = 56 * 1024 * 1024


def _slope(g, h):
    n = C_GROUPS * C_HEADS
    return 2.0 ** (-8.0 * (h * C_GROUPS + g + 1.0) / n)


def _params(*sem):
    return pltpu.CompilerParams(dimension_semantics=sem, vmem_limit_bytes=VMEM_LIMIT)


def _rms(x, gain):
    return x * lax.rsqrt(jnp.mean(x * x, axis=-1, keepdims=True) + EPS) * gain


def _layer_norm(x, gain, bias):
    mu = jnp.mean(x, axis=-1, keepdims=True)
    xc = x - mu
    var = jnp.mean(xc * xc, axis=-1, keepdims=True)
    return xc * lax.rsqrt(var + EPS) * gain + bias


def _in_proj_body(x_ref, g_ref, w_ref, o_ref, xn_ref):
    @pl.when(pl.program_id(1) == 0)
    def _():
        xn_ref[...] = _rms(x_ref[...], g_ref[...]).astype(BF16)

    o_ref[...] = jnp.dot(xn_ref[...], w_ref[...], preferred_element_type=F32)


def _in_proj(x, gain, w_bf16):
    m = x.shape[0]
    tm = min(m, 2048)
    tn = 512
    return pl.pallas_call(
        _in_proj_body,
        grid=(m // tm, D_IN // tn),
        in_specs=[pl.BlockSpec((tm, D_MODEL), lambda i, j: (i, 0)),
                  pl.BlockSpec((1, D_MODEL), lambda i, j: (0, 0)),
                  pl.BlockSpec((D_MODEL, tn), lambda i, j: (0, j))],
        out_specs=pl.BlockSpec((tm, tn), lambda i, j: (i, j)),
        out_shape=jax.ShapeDtypeStruct((m, D_IN), F32),
        scratch_shapes=[pltpu.VMEM((tm, D_MODEL), BF16)],
        compiler_params=_params("arbitrary", "arbitrary"),
        name="in_proj",
    )(x, gain.reshape(1, D_MODEL), w_bf16)


def _token_mix_body(u_ref, v_ref, ng_ref, nb_ref, w_ref, bias_ref, o_ref, vn_ref, *, r, chunk_len, n_sub):
    row = lax.broadcasted_iota(jnp.int32, (r, r), 0)
    col = lax.broadcasted_iota(jnp.int32, (r, r), 1)
    shift = chunk_len.bit_length() - 1
    keep = ((row >> shift) == (col >> shift)) & (col <= row)
    wm = [jnp.where(keep, w_ref[g], 0.0).astype(BF16) for g in range(A_GROUPS)]
    for c in range(n_sub):
        rows = slice(c * r, (c + 1) * r)
        vn = _layer_norm(v_ref[rows, :], ng_ref[...], nb_ref[...])
        vn_ref[rows, :] = vn
        vnb = vn.astype(BF16)
        for g in range(A_GROUPS):
            lanes = slice(g * A_GROUP_DIM, (g + 1) * A_GROUP_DIM)
            mix = jnp.dot(wm[g], vnb[:, lanes], preferred_element_type=F32) + bias_ref[:, lanes]
            o_ref[rows, lanes] = u_ref[rows, lanes] * mix


def _token_mix(h, norm_g, norm_b, w_s, b_s, chunk_len):
    m = h.shape[0]
    if chunk_len == CHUNK:
        r, n_sub = CHUNK, min(4, m // CHUNK)
        w = w_s
        bias_rows = b_s.T
    else:
        r, n_sub = m, 1
        reps = m // chunk_len
        w = jnp.tile(w_s[:, :chunk_len, :chunk_len], (1, reps, reps))
        bias_rows = jnp.tile(b_s[:, :chunk_len].T, (reps, 1))
    bias = jnp.repeat(bias_rows, A_GROUP_DIM, axis=1)
    tr = r * n_sub
    body = functools.partial(_token_mix_body, r=r, chunk_len=chunk_len, n_sub=n_sub)
    return pl.pallas_call(
        body,
        grid=(m // tr,),
        in_specs=[pl.BlockSpec((tr, A_WIDTH), lambda i: (i, OFF_AU // A_WIDTH)),
                  pl.BlockSpec((tr, A_WIDTH), lambda i: (i, OFF_AV // A_WIDTH)),
                  pl.BlockSpec((1, A_WIDTH), lambda i: (0, 0)),
                  pl.BlockSpec((1, A_WIDTH), lambda i: (0, 0)),
                  pl.BlockSpec((A_GROUPS, r, r), lambda i: (0, 0, 0)),
                  pl.BlockSpec((r, A_WIDTH), lambda i: (0, 0))],
        out_specs=[pl.BlockSpec((tr, A_WIDTH), lambda i: (i, 0)),
                   pl.BlockSpec((tr, A_WIDTH), lambda i: (i, 0))],
        out_shape=[jax.ShapeDtypeStruct((m, A_WIDTH), F32)] * 2,
        compiler_params=_params("arbitrary"),
        name="token_mix",
    )(h, h, norm_g.reshape(1, A_WIDTH), norm_b.reshape(1, A_WIDTH), w, bias)


def _glu(z):
    return z[:, :B_WIDTH] * jax.nn.sigmoid(z[:, B_WIDTH:])


def _conv_prompt_body(z_ref, zh_ref, cw_ref, cb_ref, lg_ref, lb_ref, o_ref, buf_ref, pad_ref, *, tr, tiles):
    t = pl.program_id(0) % tiles
    hg = _glu(z_ref[...])
    pad_ref[0:CONV_HALO, :] = jnp.where(t == 0, 0.0, _glu(zh_ref[...]))
    pad_ref[CONV_HALO:CONV_HALO + tr, :] = hg
    first = CONV_HALO - (CONV_W - 1)
    sub = 32
    for s in range(tr // sub):
        acc = jnp.zeros((sub, B_WIDTH), F32)
        for k in range(CONV_W):
            start = first + s * sub + k
            acc = acc + cw_ref[k:k + 1, :] * pad_ref[start:start + sub, :]
        y = _layer_norm(acc + cb_ref[...], lg_ref[...], lb_ref[...])
        o_ref[s * sub:(s + 1) * sub, :] = y * jax.nn.sigmoid(y)

    @pl.when(t == tiles - 1)
    def _():
        buf_ref[...] = pad_ref[CONV_HALO + tr - (CONV_W - 1):CONV_HALO + tr, :]


def _conv_prompt(h, bsz, conv_w, conv_b, ln_g, ln_b):
    m = h.shape[0]
    t = m // bsz
    tr = min(256, t)
    tiles = t // tr
    halo_per_tile = tr // CONV_HALO
    body = functools.partial(_conv_prompt_body, tr=tr, tiles=tiles)
    return pl.pallas_call(
        body,
        grid=(m // tr,),
        in_specs=[pl.BlockSpec((tr, 2 * B_WIDTH), lambda i: (i, OFF_B // (2 * B_WIDTH))),
                  pl.BlockSpec((CONV_HALO, 2 * B_WIDTH),
                               lambda i: (jnp.maximum(i * halo_per_tile - 1, 0), OFF_B // (2 * B_WIDTH))),
                  pl.BlockSpec((CONV_W, B_WIDTH), lambda i: (0, 0)),
                  pl.BlockSpec((1, B_WIDTH), lambda i: (0, 0)),
                  pl.BlockSpec((1, B_WIDTH), lambda i: (0, 0)),
                  pl.BlockSpec((1, B_WIDTH), lambda i: (0, 0))],
        out_specs=[pl.BlockSpec((tr, B_WIDTH), lambda i: (i, 0)),
                   pl.BlockSpec((None, CONV_W - 1, B_WIDTH), lambda i: (i // tiles, 0, 0))],
        out_shape=[jax.ShapeDtypeStruct((m, B_WIDTH), F32),
                   jax.ShapeDtypeStruct((bsz, CONV_W - 1, B_WIDTH), F32)],
        scratch_shapes=[pltpu.VMEM((CONV_HALO + tr, B_WIDTH), F32)],
        compiler_params=_params("arbitrary"),
        name="conv_prompt",
    )(h, h, conv_w, conv_b.reshape(1, B_WIDTH), ln_g.reshape(1, B_WIDTH), ln_b.reshape(1, B_WIDTH))


def _conv_sample_body(z_ref, st_ref, cw_ref, cb_ref, lg_ref, lb_ref, o_ref, buf_ref, pad_ref, *, bsz):
    hg = _glu(z_ref[...])
    pad_ref[:, 0:CONV_W - 1, :] = st_ref[...]
    pad_ref[:, CONV_W - 1:CONV_W - 1 + DEC_SEQ, :] = hg.reshape(bsz, DEC_SEQ, B_WIDTH)
    acc = jnp.zeros((bsz, DEC_SEQ, B_WIDTH), F32)
    for k in range(CONV_W):
        acc = acc + cw_ref[k:k + 1, :] * pad_ref[:, k:k + DEC_SEQ, :]
    y = _layer_norm(acc + cb_ref[...], lg_ref[...], lb_ref[...])
    o_ref[...] = (y * jax.nn.sigmoid(y)).reshape(bsz * DEC_SEQ, B_WIDTH)
    buf_ref[...] = pad_ref[:, DEC_SEQ:DEC_SEQ + CONV_W - 1, :]


def _conv_sample(h, state, conv_w, conv_b, ln_g, ln_b):
    m = h.shape[0]
    bsz = state.shape[0]
    body = functools.partial(_conv_sample_body, bsz=bsz)
    return pl.pallas_call(
        body,
        grid=(1,),
        in_specs=[pl.BlockSpec((m, 2 * B_WIDTH), lambda i: (0, OFF_B // (2 * B_WIDTH))),
                  pl.BlockSpec((bsz, CONV_W - 1, B_WIDTH), lambda i: (0, 0, 0)),
                  pl.BlockSpec((CONV_W, B_WIDTH), lambda i: (0, 0)),
                  pl.BlockSpec((1, B_WIDTH), lambda i: (0, 0)),
                  pl.BlockSpec((1, B_WIDTH), lambda i: (0, 0)),
                  pl.BlockSpec((1, B_WIDTH), lambda i: (0, 0))],
        out_specs=[pl.BlockSpec((m, B_WIDTH), lambda i: (0, 0)),
                   pl.BlockSpec((bsz, CONV_W - 1, B_WIDTH), lambda i: (0, 0, 0))],
        out_shape=[jax.ShapeDtypeStruct((m, B_WIDTH), F32),
                   jax.ShapeDtypeStruct((bsz, CONV_W - 1, B_WIDTH), F32)],
        scratch_shapes=[pltpu.VMEM((bsz, CONV_W - 1 + DEC_SEQ, B_WIDTH), F32)],
        compiler_params=_params("arbitrary"),
        name="conv_sample",
    )(h, state, conv_w, conv_b.reshape(1, B_WIDTH), ln_g.reshape(1, B_WIDTH), ln_b.reshape(1, B_WIDTH))


def _dot_nt(a, b):
    return lax.dot_general(a, b, (((1,), (1,)), ((), ())), preferred_element_type=F32)


def _attn_prompt_body(q_ref, kp_ref, kc_ref, vp_ref, vc_ref, o_ref, l_ref, *, g, d, tq):
    iq = pl.program_id(2)
    row = lax.broadcasted_iota(jnp.int32, (tq, tq), 0)
    col = lax.broadcasted_iota(jnp.int32, (tq, tq), 1)
    dist_c = (row - col).astype(F32)
    dist_p = dist_c + float(tq)
    valid_c = col <= row
    valid_p = col >= row + jnp.where(iq > 0, 0, tq)
    lane = lax.broadcasted_iota(jnp.int32, (tq, 128), 1)
    low = lane < C_HEAD_DIM
    for hp in range(C_HEADS // 2):
        lanes = slice(hp * 128, (hp + 1) * 128)
        q2 = q_ref[:, lanes] * ATTN_SCALE
        kp2 = kp_ref[:, lanes].astype(BF16)
        kc2 = kc_ref[:, lanes].astype(BF16)
        vp2 = vp_ref[:, lanes].astype(BF16)
        vc2 = vc_ref[:, lanes].astype(BF16)
        outs, lses = [], []
        for hh in range(2):
            sel = low if hh == 0 else jnp.logical_not(low)
            qm = jnp.where(sel, q2, 0.0).astype(BF16)
            slope = _slope(g, 2 * hp + hh) * d
            sp = jnp.where(valid_p, _dot_nt(qm, kp2) - slope * dist_p, NEG_INF)
            sc = jnp.where(valid_c, _dot_nt(qm, kc2) - slope * dist_c, NEG_INF)
            mx = jnp.maximum(jnp.max(sp, axis=-1, keepdims=True), jnp.max(sc, axis=-1, keepdims=True))
            pp = jnp.exp(sp - mx)
            pc = jnp.exp(sc - mx)
            den = jnp.sum(pp, axis=-1, keepdims=True) + jnp.sum(pc, axis=-1, keepdims=True)
            acc = (jnp.dot(pp.astype(BF16), vp2, preferred_element_type=F32)
                   + jnp.dot(pc.astype(BF16), vc2, preferred_element_type=F32))
            outs.append(acc / den)
            lses.append(mx + jnp.log(den))
        o_ref[:, lanes] = jnp.where(low, outs[0], outs[1])
        l_ref[:, lanes] = jnp.where(low, lses[0], lses[1])


def _attn_prompt(h, bsz, g):
    m = h.shape[0]
    t = m // bsz
    d = C_CONFIGS[g][1]
    tq = C_WINDOW
    nq = t // d // tq
    hv = h.reshape(m // d, d * D_IN)
    cq, ck, cv = (off // C_GROUP_WIDTH + g for off in (OFF_CQ, OFF_CK, OFF_CV))

    def cur(c):
        return pl.BlockSpec((tq, C_GROUP_WIDTH), lambda b, r, i: (b * nq + i, r * COL_BLOCKS + c))

    def prev(c):
        return pl.BlockSpec((tq, C_GROUP_WIDTH),
                            lambda b, r, i: (b * nq + jnp.maximum(i - 1, 0), r * COL_BLOCKS + c))

    out_spec = pl.BlockSpec((tq, C_GROUP_WIDTH), lambda b, r, i: (b * nq + i, r))
    body = functools.partial(_attn_prompt_body, g=g, d=d, tq=tq)
    o, l = pl.pallas_call(
        body,
        grid=(bsz, d, nq),
        in_specs=[cur(cq), prev(ck), cur(ck), prev(cv), cur(cv)],
        out_specs=[out_spec, out_spec],
        out_shape=[jax.ShapeDtypeStruct((m // d, d * C_GROUP_WIDTH), F32)] * 2,
        compiler_params=_params("arbitrary", "arbitrary", "arbitrary"),
        name=f"attn_prompt_g{g}",
    )(hv, hv, hv, hv, hv)
    return o.reshape(m, C_GROUP_WIDTH), l.reshape(m, C_GROUP_WIDTH)


def _attn_sample_body(qkv_ref, slope_ref, k0_ref, v0_ref, k1_ref, v1_ref, k2_ref, v2_ref, o_ref):
    caches = ((k0_ref, v0_ref), (k1_ref, v1_ref), (k2_ref, v2_ref))
    lane = lax.broadcasted_iota(jnp.int32, (C_HEADS, C_GROUP_WIDTH), 1)
    head = lax.broadcasted_iota(jnp.int32, (C_HEADS, C_GROUP_WIDTH), 0)
    head_lanes = (lane >> 6) == head
    group_out, group_lse = [], []
    for g, (_, d) in enumerate(C_CONFIGS):
        kc_ref, vc_ref = caches[g]
        q = qkv_ref[:, g * 512:(g + 1) * 512] * ATTN_SCALE
        kn = qkv_ref[:, (3 + g) * 512:(4 + g) * 512]
        vn = qkv_ref[:, (6 + g) * 512:(7 + g) * 512]
        q_exp = [jnp.where(head_lanes, jnp.broadcast_to(q[i:i + 1, :], (C_HEADS, C_GROUP_WIDTH)), 0.0)
                 for i in range(DEC_SEQ)]
        pieces_o = [None] * DEC_SEQ
        pieces_l = [None] * DEC_SEQ
        for r in range(min(d, DEC_SEQ)):
            members = list(range(r, DEC_SEQ, d))
            rows = C_HEADS * len(members)
            qb = q_exp[members[0]] if len(members) == 1 else jnp.concatenate([q_exp[i] for i in members], axis=0)
            a_col = lax.broadcasted_iota(jnp.int32, (rows, 1), 0) >> 3
            a_f = a_col.astype(F32)
            slope = slope_ref[g, 0:rows, 0:1] * float(d)
            lanes = slice(r * 512, (r + 1) * 512)
            m_idx = lax.broadcasted_iota(jnp.int32, (rows, C_WINDOW), 1)
            dist = (float(C_WINDOW) + a_f) - m_idx.astype(F32)
            s_c = _dot_nt(qb.astype(BF16), kc_ref[:, lanes].astype(BF16))
            s_c = jnp.where(m_idx >= a_col, s_c - slope * dist, NEG_INF)
            s_n = []
            for a2, i2 in enumerate(members):
                s = jnp.sum(qb * kn[i2:i2 + 1, :], axis=-1, keepdims=True)
                s_n.append(jnp.where(a_col >= a2, s - slope * (a_f - float(a2)), NEG_INF))
            mx = jnp.max(s_c, axis=-1, keepdims=True)
            for s in s_n:
                mx = jnp.maximum(mx, s)
            p_c = jnp.exp(s_c - mx)
            den = jnp.sum(p_c, axis=-1, keepdims=True)
            acc = jnp.dot(p_c.astype(BF16), vc_ref[:, lanes].astype(BF16), preferred_element_type=F32)
            for s, i2 in zip(s_n, members):
                p = jnp.exp(s - mx)
                den = den + p
                acc = acc + p * vn[i2:i2 + 1, :]
            out = acc / den
            lse = mx + jnp.log(den)
            for a, i in enumerate(members):
                pieces_o[i] = out[a * C_HEADS:(a + 1) * C_HEADS, :]
                pieces_l[i] = lse[a * C_HEADS:(a + 1) * C_HEADS, :]
        group_out.append(jnp.concatenate(pieces_o, axis=0))
        group_lse.append(jnp.concatenate(pieces_l, axis=0))
    mx = jnp.maximum(jnp.maximum(group_lse[0], group_lse[1]), group_lse[2])
    e = [jnp.exp(l - mx) for l in group_lse]
    tot = e[0] + e[1] + e[2]
    merged = (e[0] / tot) * group_out[0] + (e[1] / tot) * group_out[1] + (e[2] / tot) * group_out[2]
    merged = merged.reshape(DEC_SEQ, C_HEADS, C_GROUP_WIDTH)
    o_ref[...] = jnp.sum(jnp.where(head_lanes[None], merged, 0.0), axis=1)


def _attn_sample(h, caches, layer):
    m = h.shape[0]
    bsz = m // DEC_SEQ
    qkv = h[:, OFF_CQ:D_IN]
    slopes = jnp.array([[_slope(g, hh) for hh in range(C_HEADS)] for g in range(C_GROUPS)], F32)
    slopes = jnp.broadcast_to(jnp.tile(slopes, (1, DEC_SEQ))[:, :, None], (C_GROUPS, DEC_SEQ * C_HEADS, 128))
    in_specs = [pl.BlockSpec((DEC_SEQ, D_IN - OFF_CQ), lambda b: (b, 0)),
                pl.BlockSpec((C_GROUPS, DEC_SEQ * C_HEADS, 128), lambda b: (0, 0, 0))]
    views = []
    for g, (w, d) in enumerate(C_CONFIGS):
        for j in range(2):
            c = caches[2 * g + j]
            assert c.shape[2] == w, "the cache must hold the last W_g positions"
            views.append(c.reshape(c.shape[0], bsz, C_WINDOW, d * C_GROUP_WIDTH))
            in_specs.append(pl.BlockSpec((None, None, C_WINDOW, min(d, DEC_SEQ) * C_GROUP_WIDTH),
                                         lambda b: (layer, b, 0, 0)))
    return pl.pallas_call(
        _attn_sample_body,
        grid=(bsz,),
        in_specs=in_specs,
        out_specs=pl.BlockSpec((DEC_SEQ, C_GROUP_WIDTH), lambda b: (b, 0)),
        out_shape=jax.ShapeDtypeStruct((m, C_GROUP_WIDTH), F32),
        compiler_params=_params("arbitrary"),
        name="attn_sample",
    )(qkv, slopes, *views)


def _merge_out_body(*refs, split_c):
    if split_c:
        (x_ref, oa_ref, ob_ref, o0_ref, o1_ref, o2_ref, l0_ref, l1_ref, l2_ref,
         gate_ref, wb_ref, wo_ref, g_ref, out_ref) = refs
        l0, l1, l2 = l0_ref[...], l1_ref[...], l2_ref[...]
        mx = jnp.maximum(jnp.maximum(l0, l1), l2)
        e0, e1, e2 = jnp.exp(l0 - mx), jnp.exp(l1 - mx), jnp.exp(l2 - mx)
        tot = e0 + e1 + e2
        oc = (e0 / tot) * o0_ref[...] + (e1 / tot) * o1_ref[...] + (e2 / tot) * o2_ref[...]
    else:
        x_ref, oa_ref, ob_ref, oc_ref, gate_ref, wb_ref, wo_ref, g_ref, out_ref = refs
        oc = oc_ref[...]
    merged = None
    for n, o in enumerate((oa_ref[...], ob_ref[...], oc)):
        proj = jnp.dot(o.astype(BF16), wb_ref[n * BRANCH_WIDTH:(n + 1) * BRANCH_WIDTH, :],
                       preferred_element_type=F32)
        term = jax.nn.sigmoid(gate_ref[:, n * D_MODEL:(n + 1) * D_MODEL]) * proj
        merged = term if merged is None else merged + term
    y = jnp.dot(merged.astype(BF16), wo_ref[...], preferred_element_type=F32)
    out_ref[...] = x_ref[...] + _rms(y, g_ref[...])


def _merge_out(x, h, o_a, o_b, o_c, w_branch_bf16, w_out_bf16, gain):
    m = x.shape[0]
    tm = min(m, 256)
    split_c = isinstance(o_c, (list, tuple))
    c_args = list(o_c) if split_c else [o_c]
    row512 = pl.BlockSpec((tm, BRANCH_WIDTH), lambda i: (i, 0))
    in_specs = ([pl.BlockSpec((tm, D_MODEL), lambda i: (i, 0)), row512, row512] + [row512] * len(c_args)
                + [pl.BlockSpec((tm, N_BRANCH * D_MODEL), lambda i: (i, OFF_G // (N_BRANCH * D_MODEL))),
                   pl.BlockSpec((N_BRANCH * BRANCH_WIDTH, D_MODEL), lambda i: (0, 0)),
                   pl.BlockSpec((D_MODEL, D_MODEL), lambda i: (0, 0)),
                   pl.BlockSpec((1, D_MODEL), lambda i: (0, 0))])
    return pl.pallas_call(
        functools.partial(_merge_out_body, split_c=split_c),
        grid=(m // tm,),
        in_specs=in_specs,
        out_specs=pl.BlockSpec((tm, D_MODEL), lambda i: (i, 0)),
        out_shape=jax.ShapeDtypeStruct((m, D_MODEL), F32),
        compiler_params=_params("arbitrary"),
        name="merge_out",
    )(x, o_a, o_b, *c_args, h, w_branch_bf16, w_out_bf16, gain.reshape(1, D_MODEL))


FF_CHUNK = D_FF // 2


def _ffn_body(x_ref, g1_ref, w1_ref, w2_ref, g2_ref, out_ref):
    x = x_ref[...]
    xn = _rms(x, g1_ref[...]).astype(BF16)
    y = None
    for c in range(D_FF // FF_CHUNK):
        gate = jnp.dot(xn, w1_ref[:, c * FF_CHUNK:(c + 1) * FF_CHUNK], preferred_element_type=F32)
        up = jnp.dot(xn, w1_ref[:, D_FF + c * FF_CHUNK:D_FF + (c + 1) * FF_CHUNK], preferred_element_type=F32)
        act = (gate * jax.nn.sigmoid(gate) * up).astype(BF16)
        part = jnp.dot(act, w2_ref[c * FF_CHUNK:(c + 1) * FF_CHUNK, :], preferred_element_type=F32)
        y = part if y is None else y + part
    out_ref[...] = x + _rms(y, g2_ref[...])


def _ffn(x, gain_pre, w1_bf16, w2_bf16, gain_post):
    m = x.shape[0]
    tm = min(m, 256)
    return pl.pallas_call(
        _ffn_body,
        grid=(m // tm,),
        in_specs=[pl.BlockSpec((tm, D_MODEL), lambda i: (i, 0)),
                  pl.BlockSpec((1, D_MODEL), lambda i: (0, 0)),
                  pl.BlockSpec((D_MODEL, 2 * D_FF), lambda i: (0, 0)),
                  pl.BlockSpec((D_FF, D_MODEL), lambda i: (0, 0)),
                  pl.BlockSpec((1, D_MODEL), lambda i: (0, 0))],
        out_specs=pl.BlockSpec((tm, D_MODEL), lambda i: (i, 0)),
        out_shape=jax.ShapeDtypeStruct((m, D_MODEL), F32),
        compiler_params=_params("arbitrary"),
        name="ffn",
    )(x, gain_pre.reshape(1, D_MODEL), w1_bf16, w2_bf16, gain_post.reshape(1, D_MODEL))


def _heads(h, bsz, off, g):
    cols = h[:, off + g * C_GROUP_WIDTH:off + (g + 1) * C_GROUP_WIDTH]
    return cols.reshape(bsz, h.shape[0] // bsz, C_HEADS, C_HEAD_DIM)


def _run_trunk(x3, is_prompt, conv_state, caches, p):
    bsz, t, _ = x3.shape
    x = x3.reshape(bsz * t, D_MODEL)
    depth = p["w_in"].shape[0]
    bufs, vns, kvs = [], [], []
    for l in range(depth):
        h = _in_proj(x, p["norm_pre_mix"][l], p["w_in"][l])
        o_a, vn = _token_mix(h, p["a_norm_g"][l], p["a_norm_b"][l], p["a_w_s"][l], p["a_b_s"][l],
                             CHUNK if is_prompt else t)
        if is_prompt:
            o_b, buf = _conv_prompt(h, bsz, p["b_conv_w"][l], p["b_conv_b"][l], p["b_norm_g"][l], p["b_norm_b"][l])
            parts = [_attn_prompt(h, bsz, g) for g in range(C_GROUPS)]
            o_c = [o for o, _ in parts] + [lse for _, lse in parts]
        else:
            o_b, buf = _conv_sample(h, conv_state[l], p["b_conv_w"][l], p["b_conv_b"][l],
                                    p["b_norm_g"][l], p["b_norm_b"][l])
            o_c = _attn_sample(h, caches, l)
        x = _merge_out(x, h, o_a, o_b, o_c, p["w_branch"][l], p["w_out"][l], p["norm_post_mix"][l])
        x = _ffn(x, p["norm_pre_ffn"][l], p["ffn_w_in"][l], p["ffn_w_out"][l], p["norm_post_ffn"][l])
        bufs.append(buf)
        vns.append(vn.reshape(bsz, t, A_WIDTH))
        layer_kv = []
        for g, (w, _) in enumerate(C_CONFIGS):
            keep = min(w, t) if is_prompt else t
            layer_kv.append(_heads(h, bsz, OFF_CK, g)[:, t - keep:])
            layer_kv.append(_heads(h, bsz, OFF_CV, g)[:, t - keep:])
        kvs.append(layer_kv)
    new_kv = [jnp.stack([kvs[l][j] for l in range(depth)], axis=0) for j in range(2 * C_GROUPS)]
    return x.reshape(bsz, t, D_MODEL), jnp.stack(bufs, axis=0), jnp.stack(vns, axis=0), new_kv


def kernel(x_prompt, x_sample, state_b_conv, cache_c0_k, cache_c0_v, cache_c1_k, cache_c1_v, cache_c2_k,
           cache_c2_v, norm_pre_mix, norm_post_mix, norm_pre_ffn, norm_post_ffn, w_in, a_norm_g, a_norm_b,
           a_w_s, a_b_s, b_conv_w, b_conv_b, b_norm_g, b_norm_b, w_branch, w_out, ffn_w_in, ffn_w_out):
    assert x_sample.shape[1] == DEC_SEQ
    w_in_cols = jnp.concatenate([w_in[..., REF_OFF_G:], w_in[..., :REF_OFF_G]], axis=-1)
    p = dict(norm_pre_mix=norm_pre_mix, norm_post_mix=norm_post_mix, norm_pre_ffn=norm_pre_ffn,
             norm_post_ffn=norm_post_ffn, w_in=w_in_cols.astype(BF16), a_norm_g=a_norm_g, a_norm_b=a_norm_b,
             a_w_s=a_w_s, a_b_s=a_b_s, b_conv_w=b_conv_w, b_conv_b=b_conv_b, b_norm_g=b_norm_g,
             b_norm_b=b_norm_b, w_branch=w_branch.astype(BF16), w_out=w_out.astype(BF16),
             ffn_w_in=ffn_w_in.astype(BF16), ffn_w_out=ffn_w_out.astype(BF16))
    caches = (cache_c0_k, cache_c0_v, cache_c1_k, cache_c1_v, cache_c2_k, cache_c2_v)
    y_prompt, buf_p, _, kv_p = _run_trunk(x_prompt, True, None, None, p)
    y_sample, buf_s, vn_s, kv_s = _run_trunk(x_sample, False, state_b_conv, caches, p)
    return (y_prompt, y_sample, buf_p, buf_s, vn_s, *kv_p, *kv_s)
```

```python
import functools

import jax
import jax.numpy as jnp
from jax import lax
from jax.experimental import pallas as pl
from jax.experimental.pallas import tpu as pltpu

F32 = jnp.float32
BF16 = jnp.bfloat16

D_MODEL = 1024
DEC_SEQ = 8
A_WIDTH = 512
A_GROUPS = 4
A_GROUP_DIM = 128
CHUNK = 128
B_WIDTH = 512
CONV_W = 31
CONV_HALO = 32
C_HEADS = 8
C_HEAD_DIM = 64
C_GROUP_WIDTH = 512
C_CONFIGS = ((128, 1), (512, 4), (2048, 16))
C_GROUPS = 3
C_WINDOW = 128
N_BRANCH = 3
BRANCH_WIDTH = 512
D_FF = 2816
EPS = 1e-6
NEG_INF = -1e30
ATTN_SCALE = C_HEAD_DIM ** -0.5
REF_OFF_G = 6656
OFF_G = 0
OFF_AU = 3072
OFF_AV = 3584
OFF_B = 4096
OFF_CQ = 5120
OFF_CK = 6656
OFF_CV = 8192
D_IN = 9728
COL_BLOCKS = D_IN // 512

VMEM_LIMIT = 56 * 1024 * 1024


def _slope(g, h):
    n = C_GROUPS * C_HEADS
    return 2.0 ** (-8.0 * (h * C_GROUPS + g + 1.0) / n)


def _params(*sem):
    return pltpu.CompilerParams(dimension_semantics=sem, vmem_limit_bytes=VMEM_LIMIT)


def _rms(x, gain):
    return x * lax.rsqrt(jnp.mean(x * x, axis=-1, keepdims=True) + EPS) * gain


def _layer_norm(x, gain, bias):
    mu = jnp.mean(x, axis=-1, keepdims=True)
    xc = x - mu
    var = jnp.mean(xc * xc, axis=-1, keepdims=True)
    return xc * lax.rsqrt(var + EPS) * gain + bias


def _in_proj_body(x_ref, g_ref, w_ref, o_ref, xn_ref):
    @pl.when(pl.program_id(1) == 0)
    def _():
        xn_ref[...] = _rms(x_ref[...], g_ref[...]).astype(BF16)

    o_ref[...] = jnp.dot(xn_ref[...], w_ref[...], preferred_element_type=F32)


def _in_proj(x, gain, w_bf16):
    m = x.shape[0]
    tm = min(m, 2048)
    tn = 512
    return pl.pallas_call(
        _in_proj_body,
        grid=(m // tm, D_IN // tn),
        in_specs=[pl.BlockSpec((tm, D_MODEL), lambda i, j: (i, 0)),
                  pl.BlockSpec((1, D_MODEL), lambda i, j: (0, 0)),
                  pl.BlockSpec((D_MODEL, tn), lambda i, j: (0, j))],
        out_specs=pl.BlockSpec((tm, tn), lambda i, j: (i, j)),
        out_shape=jax.ShapeDtypeStruct((m, D_IN), F32),
        scratch_shapes=[pltpu.VMEM((tm, D_MODEL), BF16)],
        compiler_params=_params("arbitrary", "arbitrary"),
        name="in_proj",
    )(x, gain.reshape(1, D_MODEL), w_bf16)


def _token_mix_body(u_ref, v_ref, ng_ref, nb_ref, w_ref, bias_ref, o_ref, vn_ref, *, r, chunk_len, n_sub):
    row = lax.broadcasted_iota(jnp.int32, (r, r), 0)
    col = lax.broadcasted_iota(jnp.int32, (r, r), 1)
    shift = chunk_len.bit_length() - 1
    keep = ((row >> shift) == (col >> shift)) & (col <= row)
    wm = [jnp.where(keep, w_ref[g], 0.0).astype(BF16) for g in range(A_GROUPS)]
    for c in range(n_sub):
        rows = slice(c * r, (c + 1) * r)
        vn = _layer_norm(v_ref[rows, :], ng_ref[...], nb_ref[...])
        vn_ref[rows, :] = vn
        vnb = vn.astype(BF16)
        for g in range(A_GROUPS):
            lanes = slice(g * A_GROUP_DIM, (g + 1) * A_GROUP_DIM)
            mix = jnp.dot(wm[g], vnb[:, lanes], preferred_element_type=F32) + bias_ref[:, lanes]
            o_ref[rows, lanes] = u_ref[rows, lanes] * mix


def _token_mix(h, norm_g, norm_b, w_s, b_s, chunk_len):
    m = h.shape[0]
    if chunk_len == CHUNK:
        r, n_sub = CHUNK, min(4, m // CHUNK)
        w = w_s
        bias_rows = b_s.T
    else:
        r, n_sub = m, 1
        reps = m // chunk_len
        w = jnp.tile(w_s[:, :chunk_len, :chunk_len], (1, reps, reps))
        bias_rows = jnp.tile(b_s[:, :chunk_len].T, (reps, 1))
    bias = jnp.repeat(bias_rows, A_GROUP_DIM, axis=1)
    tr = r * n_sub
    body = functools.partial(_token_mix_body, r=r, chunk_len=chunk_len, n_sub=n_sub)
    return pl.pallas_call(
        body,
        grid=(m // tr,),
        in_specs=[pl.BlockSpec((tr, A_WIDTH), lambda i: (i, OFF_AU // A_WIDTH)),
                  pl.BlockSpec((tr, A_WIDTH), lambda i: (i, OFF_AV // A_WIDTH)),
                  pl.BlockSpec((1, A_WIDTH), lambda i: (0, 0)),
                  pl.BlockSpec((1, A_WIDTH), lambda i: (0, 0)),
                  pl.BlockSpec((A_GROUPS, r, r), lambda i: (0, 0, 0)),
                  pl.BlockSpec((r, A_WIDTH), lambda i: (0, 0))],
        out_specs=[pl.BlockSpec((tr, A_WIDTH), lambda i: (i, 0)),
                   pl.BlockSpec((tr, A_WIDTH), lambda i: (i, 0))],
        out_shape=[jax.ShapeDtypeStruct((m, A_WIDTH), F32)] * 2,
        compiler_params=_params("arbitrary"),
        name="token_mix",
    )(h, h, norm_g.reshape(1, A_WIDTH), norm_b.reshape(1, A_WIDTH), w, bias)


def _glu(z):
    return z[:, :B_WIDTH] * jax.nn.sigmoid(z[:, B_WIDTH:])


def _conv_prompt_body(z_ref, zh_ref, cw_ref, cb_ref, lg_ref, lb_ref, o_ref, buf_ref, pad_ref, *, tr, tiles):
    t = pl.program_id(0) % tiles
    hg = _glu(z_ref[...])
    pad_ref[0:CONV_HALO, :] = jnp.where(t == 0, 0.0, _glu(zh_ref[...]))
    pad_ref[CONV_HALO:CONV_HALO + tr, :] = hg
    first = CONV_HALO - (CONV_W - 1)
    sub = 32
    for s in range(tr // sub):
        acc = jnp.zeros((sub, B_WIDTH), F32)
        for k in range(CONV_W):
            start = first + s * sub + k
            acc = acc + cw_ref[k:k + 1, :] * pad_ref[start:start + sub, :]
        y = _layer_norm(acc + cb_ref[...], lg_ref[...], lb_ref[...])
        o_ref[s * sub:(s + 1) * sub, :] = y * jax.nn.sigmoid(y)

    @pl.when(t == tiles - 1)
    def _():
        buf_ref[...] = pad_ref[CONV_HALO + tr - (CONV_W - 1):CONV_HALO + tr, :]


def _conv_prompt(h, bsz, conv_w, conv_b, ln_g, ln_b):
    m = h.shape[0]
    t = m // bsz
    tr = min(256, t)
    tiles = t // tr
    halo_per_tile = tr // CONV_HALO
    body = functools.partial(_conv_prompt_body, tr=tr, tiles=tiles)
    return pl.pallas_call(
        body,
        grid=(m // tr,),
        in_specs=[pl.BlockSpec((tr, 2 * B_WIDTH), lambda i: (i, OFF_B // (2 * B_WIDTH))),
                  pl.BlockSpec((CONV_HALO, 2 * B_WIDTH),
                               lambda i: (jnp.maximum(i * halo_per_tile - 1, 0), OFF_B // (2 * B_WIDTH))),
                  pl.BlockSpec((CONV_W, B_WIDTH), lambda i: (0, 0)),
                  pl.BlockSpec((1, B_WIDTH), lambda i: (0, 0)),
                  pl.BlockSpec((1, B_WIDTH), lambda i: (0, 0)),
                  pl.BlockSpec((1, B_WIDTH), lambda i: (0, 0))],
        out_specs=[pl.BlockSpec((tr, B_WIDTH), lambda i: (i, 0)),
                   pl.BlockSpec((None, CONV_W - 1, B_WIDTH), lambda i: (i // tiles, 0, 0))],
        out_shape=[jax.ShapeDtypeStruct((m, B_WIDTH), F32),
                   jax.ShapeDtypeStruct((bsz, CONV_W - 1, B_WIDTH), F32)],
        scratch_shapes=[pltpu.VMEM((CONV_HALO + tr, B_WIDTH), F32)],
        compiler_params=_params("arbitrary"),
        name="conv_prompt",
    )(h, h, conv_w, conv_b.reshape(1, B_WIDTH), ln_g.reshape(1, B_WIDTH), ln_b.reshape(1, B_WIDTH))


def _conv_sample_body(z_ref, st_ref, cw_ref, cb_ref, lg_ref, lb_ref, o_ref, buf_ref, pad_ref, *, bsz):
    hg = _glu(z_ref[...])
    pad_ref[:, 0:CONV_W - 1, :] = st_ref[...]
    pad_ref[:, CONV_W - 1:CONV_W - 1 + DEC_SEQ, :] = hg.reshape(bsz, DEC_SEQ, B_WIDTH)
    acc = jnp.zeros((bsz, DEC_SEQ, B_WIDTH), F32)
    for k in range(CONV_W):
        acc = acc + cw_ref[k:k + 1, :] * pad_ref[:, k:k + DEC_SEQ, :]
    y = _layer_norm(acc + cb_ref[...], lg_ref[...], lb_ref[...])
    o_ref[...] = (y * jax.nn.sigmoid(y)).reshape(bsz * DEC_SEQ, B_WIDTH)
    buf_ref[...] = pad_ref[:, DEC_SEQ:DEC_SEQ + CONV_W - 1, :]


def _conv_sample(h, state, conv_w, conv_b, ln_g, ln_b):
    m = h.shape[0]
    bsz = state.shape[0]
    body = functools.partial(_conv_sample_body, bsz=bsz)
    return pl.pallas_call(
        body,
        grid=(1,),
        in_specs=[pl.BlockSpec((m, 2 * B_WIDTH), lambda i: (0, OFF_B // (2 * B_WIDTH))),
                  pl.BlockSpec((bsz, CONV_W - 1, B_WIDTH), lambda i: (0, 0, 0)),
                  pl.BlockSpec((CONV_W, B_WIDTH), lambda i: (0, 0)),
                  pl.BlockSpec((1, B_WIDTH), lambda i: (0, 0)),
                  pl.BlockSpec((1, B_WIDTH), lambda i: (0, 0)),
                  pl.BlockSpec((1, B_WIDTH), lambda i: (0, 0))],
        out_specs=[pl.BlockSpec((m, B_WIDTH), lambda i: (0, 0)),
                   pl.BlockSpec((bsz, CONV_W - 1, B_WIDTH), lambda i: (0, 0, 0))],
        out_shape=[jax.ShapeDtypeStruct((m, B_WIDTH), F32),
                   jax.ShapeDtypeStruct((bsz, CONV_W - 1, B_WIDTH), F32)],
        scratch_shapes=[pltpu.VMEM((bsz, CONV_W - 1 + DEC_SEQ, B_WIDTH), F32)],
        compiler_params=_params("arbitrary"),
        name="conv_sample",
    )(h, state, conv_w, conv_b.reshape(1, B_WIDTH), ln_g.reshape(1, B_WIDTH), ln_b.reshape(1, B_WIDTH))


def _dot_nt(a, b):
    return lax.dot_general(a, b, (((1,), (1,)), ((), ())), preferred_element_type=F32)


def _attn_prompt_body(slope_ref, q_ref, kp_ref, kc_ref, vp_ref, vc_ref, o_ref, l_ref, *, g, d, n_units):
    tq = C_WINDOW
    first_tile = pl.program_id(1) == 0
    hp = pl.program_id(2)
    row = lax.broadcasted_iota(jnp.int32, (tq, tq), 0)
    col = lax.broadcasted_iota(jnp.int32, (tq, tq), 1)
    dist_c = (row - col).astype(F32)
    dist_p = dist_c + float(tq)
    valid_c = col <= row
    valid_p = col >= row
    valid_p0 = col >= row + jnp.where(first_tile, tq, 0)
    low = lax.broadcasted_iota(jnp.int32, (tq, 128), 1) < C_HEAD_DIM
    slopes = [slope_ref[g * C_HEADS + 2 * hp + hh] * float(d) for hh in range(2)]

    def unit(u, r):
        def rows(v):
            return pl.ds(v * tq * d + r, tq, stride=d) if d > 1 else pl.ds(v * tq, tq)

        cur = rows(u)
        q2 = q_ref[cur, :] * ATTN_SCALE
        kc2 = kc_ref[cur, :].astype(BF16)
        vc2 = vc_ref[cur, :].astype(BF16)
        if u == 0:
            kp2, vp2, ok_p = kp_ref[rows(0), :].astype(BF16), vp_ref[rows(0), :].astype(BF16), valid_p0
        else:
            kp2, vp2, ok_p = kc_ref[rows(u - 1), :].astype(BF16), vc_ref[rows(u - 1), :].astype(BF16), valid_p
        outs, lses = [], []
        for hh in range(2):
            sel = low if hh == 0 else jnp.logical_not(low)
            qm = jnp.where(sel, q2, 0.0).astype(BF16)
            sp = jnp.where(ok_p, _dot_nt(qm, kp2) - slopes[hh] * dist_p, NEG_INF)
            sc = jnp.where(valid_c, _dot_nt(qm, kc2) - slopes[hh] * dist_c, NEG_INF)
            mx = jnp.maximum(jnp.max(sp, axis=-1, keepdims=True), jnp.max(sc, axis=-1, keepdims=True))
            pp = jnp.exp(sp - mx)
            pc = jnp.exp(sc - mx)
            den = jnp.sum(pp, axis=-1, keepdims=True) + jnp.sum(pc, axis=-1, keepdims=True)
            acc = (jnp.dot(pp.astype(BF16), vp2, preferred_element_type=F32)
                   + jnp.dot(pc.astype(BF16), vc2, preferred_element_type=F32))
            outs.append(acc / den)
            lses.append(mx + jnp.log(den))
        o_ref[cur, :] = jnp.where(low, outs[0], outs[1])
        l_ref[cur, :] = jnp.where(low, lses[0], lses[1])

    if d == 1:
        for u in range(n_units):
            unit(u, 0)
    else:
        def per_class(r, carry):
            for u in range(n_units):
                unit(u, r)
            return carry

        lax.fori_loop(0, d, per_class, 0)


def _attn_prompt(h, bsz, g):
    m = h.shape[0]
    t = m // bsz
    d = C_CONFIGS[g][1]
    unit_rows = C_WINDOW * d
    n_units = max(1, min(1024, t) // unit_rows)
    tile = n_units * unit_rows
    tiles = t // tile
    cq, ck, cv = ((off + g * C_GROUP_WIDTH) // 128 for off in (OFF_CQ, OFF_CK, OFF_CV))

    def cur(c):
        return pl.BlockSpec((tile, 128), lambda b, i, hp: (b * tiles + i, c + hp))

    def prev(c):
        return pl.BlockSpec((unit_rows, 128),
                            lambda b, i, hp: (jnp.maximum((b * tiles + i) * n_units - 1, 0), c + hp))

    out_spec = pl.BlockSpec((tile, 128), lambda b, i, hp: (b * tiles + i, hp))
    slopes = jnp.array([_slope(gg, hh) for gg in range(C_GROUPS) for hh in range(C_HEADS)], F32)
    body = functools.partial(_attn_prompt_body, g=g, d=d, n_units=n_units)
    return pl.pallas_call(
        body,
        grid=(bsz, tiles, C_HEADS // 2),
        in_specs=[pl.BlockSpec(memory_space=pltpu.SMEM), cur(cq), prev(ck), cur(ck), prev(cv), cur(cv)],
        out_specs=[out_spec, out_spec],
        out_shape=[jax.ShapeDtypeStruct((m, C_GROUP_WIDTH), F32)] * 2,
        compiler_params=_params("arbitrary", "arbitrary", "arbitrary"),
        name=f"attn_prompt_g{g}",
    )(slopes, h, h, h, h, h)


def _attn_sample_body(qkv_ref, slope_ref, k0_ref, v0_ref, k1_ref, v1_ref, k2_ref, v2_ref, o_ref):
    caches = ((k0_ref, v0_ref), (k1_ref, v1_ref), (k2_ref, v2_ref))
    rows = C_HEADS * DEC_SEQ
    rid = lax.broadcasted_iota(jnp.int32, (rows, 1), 0)
    i_row = rid & (DEC_SEQ - 1)
    own_lanes = (lax.broadcasted_iota(jnp.int32, (rows, C_GROUP_WIDTH), 1) >> 6) == (
        lax.broadcasted_iota(jnp.int32, (rows, C_GROUP_WIDTH), 0) >> 3)
    outs, lses = [], []
    for g, (w, d) in enumerate(C_CONFIGS):
        kt_ref, vt_ref = caches[g]
        q = qkv_ref[:, g * 512:(g + 1) * 512] * ATTN_SCALE
        kn = qkv_ref[:, (3 + g) * 512:(4 + g) * 512]
        vn = qkv_ref[:, (6 + g) * 512:(7 + g) * 512]
        qbd = jnp.where(own_lanes, jnp.concatenate([q] * C_HEADS, axis=0), 0.0)
        slope = slope_ref[g, :, 0:1]
        kt = kt_ref[...].reshape(C_GROUP_WIDTH, w).astype(BF16)
        vt = vt_ref[...].reshape(C_GROUP_WIDTH, w).astype(BF16)
        pos = lax.broadcasted_iota(jnp.int32, (rows, w), 1)
        back = (w + i_row) - pos
        ok = (pos >= i_row) if d == 1 else ((back & (d - 1)) == 0) & (pos >= i_row)
        s_c = jnp.dot(qbd.astype(BF16), kt, preferred_element_type=F32)
        s_c = jnp.where(ok, s_c - slope * back.astype(F32), NEG_INF)
        s_n = []
        for i2 in range(DEC_SEQ):
            s = jnp.sum(qbd * kn[i2:i2 + 1, :], axis=-1, keepdims=True)
            back_n = i_row - i2
            ok_n = (back_n >= 0) if d == 1 else ((back_n & (d - 1)) == 0) & (back_n >= 0)
            s_n.append(jnp.where(ok_n, s - slope * back_n.astype(F32), NEG_INF))
        mx = jnp.max(s_c, axis=-1, keepdims=True)
        for s in s_n:
            mx = jnp.maximum(mx, s)
        p_c = jnp.exp(s_c - mx)
        den = jnp.sum(p_c, axis=-1, keepdims=True)
        acc = _dot_nt(p_c.astype(BF16), vt)
        for i2, s in enumerate(s_n):
            p = jnp.exp(s - mx)
            den = den + p
            acc = acc + p * vn[i2:i2 + 1, :]
        outs.append(acc / den)
        lses.append(mx + jnp.log(den))
    mx = jnp.maximum(jnp.maximum(lses[0], lses[1]), lses[2])
    e = [jnp.exp(l - mx) for l in lses]
    tot = e[0] + e[1] + e[2]
    merged = (e[0] / tot) * outs[0] + (e[1] / tot) * outs[1] + (e[2] / tot) * outs[2]
    merged = jnp.where(own_lanes, merged, 0.0).reshape(C_HEADS, DEC_SEQ, C_GROUP_WIDTH)
    o_ref[...] = jnp.sum(merged, axis=0)


def _attn_sample(h, caches, layer):
    m = h.shape[0]
    bsz = m // DEC_SEQ
    qkv = h[:, OFF_CQ:D_IN]
    slopes = jnp.array([[_slope(g, hh) for hh in range(C_HEADS)] for g in range(C_GROUPS)], F32)
    slopes = jnp.broadcast_to(jnp.repeat(slopes, DEC_SEQ, axis=1)[:, :, None], (C_GROUPS, C_HEADS * DEC_SEQ, 128))
    in_specs = [pl.BlockSpec((DEC_SEQ, D_IN - OFF_CQ), lambda b: (b, 0)),
                pl.BlockSpec((C_GROUPS, C_HEADS * DEC_SEQ, 128), lambda b: (0, 0, 0))]
    views = []
    for g, (w, _) in enumerate(C_CONFIGS):
        for j in range(2):
            c = caches[2 * g + j]
            assert c.shape[2] == w, "the cache must hold the last W_g positions"
            views.append(jnp.transpose(c, (0, 1, 3, 4, 2)))
            in_specs.append(pl.BlockSpec((None, None, C_HEADS, C_HEAD_DIM, w), lambda b: (layer, b, 0, 0, 0)))
    return pl.pallas_call(
        _attn_sample_body,
        grid=(bsz,),
        in_specs=in_specs,
        out_specs=pl.BlockSpec((DEC_SEQ, C_GROUP_WIDTH), lambda b: (b, 0)),
        out_shape=jax.ShapeDtypeStruct((m, C_GROUP_WIDTH), F32),
        compiler_params=_params("arbitrary"),
        name="attn_sample",
    )(qkv, slopes, *views)


def _merge_out_body(*refs, split_c):
    if split_c:
        (x_ref, oa_ref, ob_ref, o0_ref, o1_ref, o2_ref, l0_ref, l1_ref, l2_ref,
         gate_ref, wb_ref, wo_ref, g_ref, out_ref) = refs
        l0, l1, l2 = l0_ref[...], l1_ref[...], l2_ref[...]
        mx = jnp.maximum(jnp.maximum(l0, l1), l2)
        e0, e1, e2 = jnp.exp(l0 - mx), jnp.exp(l1 - mx), jnp.exp(l2 - mx)
        tot = e0 + e1 + e2
        oc = (e0 / tot) * o0_ref[...] + (e1 / tot) * o1_ref[...] + (e2 / tot) * o2_ref[...]
    else:
        x_ref, oa_ref, ob_ref, oc_ref, gate_ref, wb_ref, wo_ref, g_ref, out_ref = refs
        oc = oc_ref[...]
    merged = None
    for n, o in enumerate((oa_ref[...], ob_ref[...], oc)):
        proj = jnp.dot(o.astype(BF16), wb_ref[n * BRANCH_WIDTH:(n + 1) * BRANCH_WIDTH, :],
                       preferred_element_type=F32)
        term = jax.nn.sigmoid(gate_ref[:, n * D_MODEL:(n + 1) * D_MODEL]) * proj
        merged = term if merged is None else merged + term
    y = jnp.dot(merged.astype(BF16), wo_ref[...], preferred_element_type=F32)
    out_ref[...] = x_ref[...] + _rms(y, g_ref[...])


def _merge_out(x, h, o_a, o_b, o_c, w_branch_bf16, w_out_bf16, gain):
    m = x.shape[0]
    tm = min(m, 256)
    split_c = isinstance(o_c, (list, tuple))
    c_args = list(o_c) if split_c else [o_c]
    row512 = pl.BlockSpec((tm, BRANCH_WIDTH), lambda i: (i, 0))
    in_specs = ([pl.BlockSpec((tm, D_MODEL), lambda i: (i, 0)), row512, row512] + [row512] * len(c_args)
                + [pl.BlockSpec((tm, N_BRANCH * D_MODEL), lambda i: (i, OFF_G // (N_BRANCH * D_MODEL))),
                   pl.BlockSpec((N_BRANCH * BRANCH_WIDTH, D_MODEL), lambda i: (0, 0)),
                   pl.BlockSpec((D_MODEL, D_MODEL), lambda i: (0, 0)),
                   pl.BlockSpec((1, D_MODEL), lambda i: (0, 0))])
    return pl.pallas_call(
        functools.partial(_merge_out_body, split_c=split_c),
        grid=(m // tm,),
        in_specs=in_specs,
        out_specs=pl.BlockSpec((tm, D_MODEL), lambda i: (i, 0)),
        out_shape=jax.ShapeDtypeStruct((m, D_MODEL), F32),
        compiler_params=_params("arbitrary"),
        name="merge_out",
    )(x, o_a, o_b, *c_args, h, w_branch_bf16, w_out_bf16, gain.reshape(1, D_MODEL))


FF_CHUNK = D_FF // 2


def _ffn_body(x_ref, g1_ref, w1_ref, w2_ref, g2_ref, out_ref):
    x = x_ref[...]
    xn = _rms(x, g1_ref[...]).astype(BF16)
    y = None
    for c in range(D_FF // FF_CHUNK):
        gate = jnp.dot(xn, w1_ref[:, c * FF_CHUNK:(c + 1) * FF_CHUNK], preferred_element_type=F32)
        up = jnp.dot(xn, w1_ref[:, D_FF + c * FF_CHUNK:D_FF + (c + 1) * FF_CHUNK], preferred_element_type=F32)
        act = (gate * jax.nn.sigmoid(gate) * up).astype(BF16)
        part = jnp.dot(act, w2_ref[c * FF_CHUNK:(c + 1) * FF_CHUNK, :], preferred_element_type=F32)
        y = part if y is None else y + part
    out_ref[...] = x + _rms(y, g2_ref[...])


def _ffn(x, gain_pre, w1_bf16, w2_bf16, gain_post):
    m = x.shape[0]
    tm = min(m, 256)
    return pl.pallas_call(
        _ffn_body,
        grid=(m // tm,),
        in_specs=[pl.BlockSpec((tm, D_MODEL), lambda i: (i, 0)),
                  pl.BlockSpec((1, D_MODEL), lambda i: (0, 0)),
                  pl.BlockSpec((D_MODEL, 2 * D_FF), lambda i: (0, 0)),
                  pl.BlockSpec((D_FF, D_MODEL), lambda i: (0, 0)),
                  pl.BlockSpec((1, D_MODEL), lambda i: (0, 0))],
        out_specs=pl.BlockSpec((tm, D_MODEL), lambda i: (i, 0)),
        out_shape=jax.ShapeDtypeStruct((m, D_MODEL), F32),
        compiler_params=_params("arbitrary"),
        name="ffn",
    )(x, gain_pre.reshape(1, D_MODEL), w1_bf16, w2_bf16, gain_post.reshape(1, D_MODEL))


def _heads(h, bsz, off, g, keep):
    t = h.shape[0] // bsz
    cols = h.reshape(bsz, t, D_IN)[:, t - keep:, off + g * C_GROUP_WIDTH:off + (g + 1) * C_GROUP_WIDTH]
    return cols.reshape(bsz, keep, C_HEADS, C_HEAD_DIM)


def _run_trunk(x3, is_prompt, conv_state, caches, p):
    bsz, t, _ = x3.shape
    x = x3.reshape(bsz * t, D_MODEL)
    depth = p["w_in"].shape[0]
    bufs, vns, kvs = [], [], []
    for l in range(depth):
        h = _in_proj(x, p["norm_pre_mix"][l], p["w_in"][l])
        o_a, vn = _token_mix(h, p["a_norm_g"][l], p["a_norm_b"][l], p["a_w_s"][l], p["a_b_s"][l],
                             CHUNK if is_prompt else t)
        if is_prompt:
            o_b, buf = _conv_prompt(h, bsz, p["b_conv_w"][l], p["b_conv_b"][l], p["b_norm_g"][l], p["b_norm_b"][l])
            parts = [_attn_prompt(h, bsz, g) for g in range(C_GROUPS)]
            o_c = [o for o, _ in parts] + [lse for _, lse in parts]
        else:
            o_b, buf = _conv_sample(h, conv_state[l], p["b_conv_w"][l], p["b_conv_b"][l],
                                    p["b_norm_g"][l], p["b_norm_b"][l])
            o_c = _attn_sample(h, caches, l)
        x = _merge_out(x, h, o_a, o_b, o_c, p["w_branch"][l], p["w_out"][l], p["norm_post_mix"][l])
        x = _ffn(x, p["norm_pre_ffn"][l], p["ffn_w_in"][l], p["ffn_w_out"][l], p["norm_post_ffn"][l])
        bufs.append(buf)
        vns.append(vn.reshape(bsz, t, A_WIDTH))
        layer_kv = []
        for g, (w, _) in enumerate(C_CONFIGS):
            keep = min(w, t) if is_prompt else t
            layer_kv.append(_heads(h, bsz, OFF_CK, g, keep))
            layer_kv.append(_heads(h, bsz, OFF_CV, g, keep))
        kvs.append(layer_kv)
    new_kv = [jnp.stack([kvs[l][j] for l in range(depth)], axis=0) for j in range(2 * C_GROUPS)]
    return x.reshape(bsz, t, D_MODEL), jnp.stack(bufs, axis=0), jnp.stack(vns, axis=0), new_kv


def kernel(x_prompt, x_sample, state_b_conv, cache_c0_k, cache_c0_v, cache_c1_k, cache_c1_v, cache_c2_k,
           cache_c2_v, norm_pre_mix, norm_post_mix, norm_pre_ffn, norm_post_ffn, w_in, a_norm_g, a_norm_b,
           a_w_s, a_b_s, b_conv_w, b_conv_b, b_norm_g, b_norm_b, w_branch, w_out, ffn_w_in, ffn_w_out):
    assert x_sample.shape[1] == DEC_SEQ
    w_in_cols = jnp.concatenate([w_in[..., REF_OFF_G:], w_in[..., :REF_OFF_G]], axis=-1)
    p = dict(norm_pre_mix=norm_pre_mix, norm_post_mix=norm_post_mix, norm_pre_ffn=norm_pre_ffn,
             norm_post_ffn=norm_post_ffn, w_in=w_in_cols.astype(BF16), a_norm_g=a_norm_g, a_norm_b=a_norm_b,
             a_w_s=a_w_s, a_b_s=a_b_s, b_conv_w=b_conv_w, b_conv_b=b_conv_b, b_norm_g=b_norm_g,
             b_norm_b=b_norm_b, w_branch=w_branch.astype(BF16), w_out=w_out.astype(BF16),
             ffn_w_in=ffn_w_in.astype(BF16), ffn_w_out=ffn_w_out.astype(BF16))
    caches = (cache_c0_k, cache_c0_v, cache_c1_k, cache_c1_v, cache_c2_k, cache_c2_v)
    y_prompt, buf_p, _, kv_p = _run_trunk(x_prompt, True, None, None, p)
    y_sample, buf_s, vn_s, kv_s = _run_trunk(x_sample, False, state_b_conv, caches, p)
    return (y_prompt, y_sample, buf_p, buf_s, vn_s, *kv_p, *kv_s)
```

```python
import functools

import jax
import jax.numpy as jnp
from jax import lax
from jax.experimental import pallas as pl
from jax.experimental.pallas import tpu as pltpu

F32 = jnp.float32
BF16 = jnp.bfloat16

D_MODEL = 1024
DEC_SEQ = 8
A_WIDTH = 512
A_GROUPS = 4
A_GROUP_DIM = 128
CHUNK = 128
B_WIDTH = 512
CONV_W = 31
CONV_HALO = 32
C_HEADS = 8
C_HEAD_DIM = 64
C_GROUP_WIDTH = 512
C_CONFIGS = ((128, 1), (512, 4), (2048, 16))
C_GROUPS = 3
C_WINDOW = 128
N_BRANCH = 3
BRANCH_WIDTH = 512
D_FF = 2816
EPS = 1e-6
NEG_INF = -1e30
ATTN_SCALE = C_HEAD_DIM ** -0.5
REF_OFF_G = 6656
OFF_G = 0
OFF_AU = 3072
OFF_AV = 3584
OFF_B = 4096
OFF_CQ = 5120
OFF_CK = 6656
OFF_CV = 8192
D_IN = 9728
COL_BLOCKS = D_IN // 512

VMEM_LIMIT = 56 * 1024 * 1024


def _slope(g, h):
    n = C_GROUPS * C_HEADS
    return 2.0 ** (-8.0 * (h * C_GROUPS + g + 1.0) / n)


def _params(*sem):
    return pltpu.CompilerParams(dimension_semantics=sem, vmem_limit_bytes=VMEM_LIMIT)


def _rms(x, gain):
    return x * lax.rsqrt(jnp.mean(x * x, axis=-1, keepdims=True) + EPS) * gain


def _layer_norm(x, gain, bias):
    mu = jnp.mean(x, axis=-1, keepdims=True)
    xc = x - mu
    var = jnp.mean(xc * xc, axis=-1, keepdims=True)
    return xc * lax.rsqrt(var + EPS) * gain + bias


def _in_proj_body(x_ref, g_ref, w_ref, o_ref, xn_ref):
    @pl.when(pl.program_id(1) == 0)
    def _():
        xn_ref[...] = _rms(x_ref[...], g_ref[...]).astype(BF16)

    o_ref[...] = jnp.dot(xn_ref[...], w_ref[...], preferred_element_type=F32)


def _in_proj(x, gain, w_bf16):
    m = x.shape[0]
    tm = min(m, 2048)
    tn = 512
    return pl.pallas_call(
        _in_proj_body,
        grid=(m // tm, D_IN // tn),
        in_specs=[pl.BlockSpec((tm, D_MODEL), lambda i, j: (i, 0)),
                  pl.BlockSpec((1, D_MODEL), lambda i, j: (0, 0)),
                  pl.BlockSpec((D_MODEL, tn), lambda i, j: (0, j))],
        out_specs=pl.BlockSpec((tm, tn), lambda i, j: (i, j)),
        out_shape=jax.ShapeDtypeStruct((m, D_IN), F32),
        scratch_shapes=[pltpu.VMEM((tm, D_MODEL), BF16)],
        compiler_params=_params("arbitrary", "arbitrary"),
        name="in_proj",
    )(x, gain.reshape(1, D_MODEL), w_bf16)


def _token_mix_body(u_ref, v_ref, ng_ref, nb_ref, w_ref, bias_ref, o_ref, vn_ref, *, r, chunk_len, n_sub):
    row = lax.broadcasted_iota(jnp.int32, (r, r), 0)
    col = lax.broadcasted_iota(jnp.int32, (r, r), 1)
    shift = chunk_len.bit_length() - 1
    keep = ((row >> shift) == (col >> shift)) & (col <= row)
    wm = [jnp.where(keep, w_ref[g], 0.0).astype(BF16) for g in range(A_GROUPS)]
    for c in range(n_sub):
        rows = slice(c * r, (c + 1) * r)
        vn = _layer_norm(v_ref[rows, :], ng_ref[...], nb_ref[...])
        vn_ref[rows, :] = vn
        vnb = vn.astype(BF16)
        for g in range(A_GROUPS):
            lanes = slice(g * A_GROUP_DIM, (g + 1) * A_GROUP_DIM)
            mix = jnp.dot(wm[g], vnb[:, lanes], preferred_element_type=F32) + bias_ref[:, lanes]
            o_ref[rows, lanes] = u_ref[rows, lanes] * mix


def _token_mix(h, norm_g, norm_b, w_s, b_s, chunk_len):
    m = h.shape[0]
    if chunk_len == CHUNK:
        r, n_sub = CHUNK, min(4, m // CHUNK)
        w = w_s
        bias_rows = b_s.T
    else:
        r, n_sub = m, 1
        reps = m // chunk_len
        w = jnp.tile(w_s[:, :chunk_len, :chunk_len], (1, reps, reps))
        bias_rows = jnp.tile(b_s[:, :chunk_len].T, (reps, 1))
    bias = jnp.repeat(bias_rows, A_GROUP_DIM, axis=1)
    tr = r * n_sub
    body = functools.partial(_token_mix_body, r=r, chunk_len=chunk_len, n_sub=n_sub)
    return pl.pallas_call(
        body,
        grid=(m // tr,),
        in_specs=[pl.BlockSpec((tr, A_WIDTH), lambda i: (i, OFF_AU // A_WIDTH)),
                  pl.BlockSpec((tr, A_WIDTH), lambda i: (i, OFF_AV // A_WIDTH)),
                  pl.BlockSpec((1, A_WIDTH), lambda i: (0, 0)),
                  pl.BlockSpec((1, A_WIDTH), lambda i: (0, 0)),
                  pl.BlockSpec((A_GROUPS, r, r), lambda i: (0, 0, 0)),
                  pl.BlockSpec((r, A_WIDTH), lambda i: (0, 0))],
        out_specs=[pl.BlockSpec((tr, A_WIDTH), lambda i: (i, 0)),
                   pl.BlockSpec((tr, A_WIDTH), lambda i: (i, 0))],
        out_shape=[jax.ShapeDtypeStruct((m, A_WIDTH), F32)] * 2,
        compiler_params=_params("arbitrary"),
        name="token_mix",
    )(h, h, norm_g.reshape(1, A_WIDTH), norm_b.reshape(1, A_WIDTH), w, bias)


def _glu(z):
    return z[:, :B_WIDTH] * jax.nn.sigmoid(z[:, B_WIDTH:])


def _conv_prompt_body(z_ref, zh_ref, cw_ref, cb_ref, lg_ref, lb_ref, o_ref, buf_ref, pad_ref, *, tr, tiles):
    t = pl.program_id(0) % tiles
    hg = _glu(z_ref[...])
    pad_ref[0:CONV_HALO, :] = jnp.where(t == 0, 0.0, _glu(zh_ref[...]))
    pad_ref[CONV_HALO:CONV_HALO + tr, :] = hg
    first = CONV_HALO - (CONV_W - 1)
    sub = 32
    taps = [[k for k in range(CONV_W) if (first + k) % 8 == s] for s in range(8)]
    for blk in range(tr // sub):
        acc = jnp.zeros((sub // 8, 8, B_WIDTH), F32)
        for s in range(8):
            q_max = max((first + k) // 8 for k in taps[s])
            win = pad_ref[blk * sub + s:blk * sub + s + sub + 8 * q_max, :]
            for k in taps[s]:
                q = (first + k) // 8
                acc = acc + win[8 * q:8 * q + sub, :].reshape(sub // 8, 8, B_WIDTH) * cw_ref[k]
        y = _layer_norm(acc.reshape(sub, B_WIDTH) + cb_ref[...], lg_ref[...], lb_ref[...])
        o_ref[blk * sub:(blk + 1) * sub, :] = y * jax.nn.sigmoid(y)

    @pl.when(t == tiles - 1)
    def _():
        buf_ref[...] = pad_ref[CONV_HALO + tr - (CONV_W - 1):CONV_HALO + tr, :]


def _conv_prompt(h, bsz, conv_w, conv_b, ln_g, ln_b):
    m = h.shape[0]
    t = m // bsz
    tr = min(256, t)
    tiles = t // tr
    halo_per_tile = tr // CONV_HALO
    body = functools.partial(_conv_prompt_body, tr=tr, tiles=tiles)
    return pl.pallas_call(
        body,
        grid=(m // tr,),
        in_specs=[pl.BlockSpec((tr, 2 * B_WIDTH), lambda i: (i, OFF_B // (2 * B_WIDTH))),
                  pl.BlockSpec((CONV_HALO, 2 * B_WIDTH),
                               lambda i: (jnp.maximum(i * halo_per_tile - 1, 0), OFF_B // (2 * B_WIDTH))),
                  pl.BlockSpec((CONV_W, 8, B_WIDTH), lambda i: (0, 0, 0)),
                  pl.BlockSpec((1, B_WIDTH), lambda i: (0, 0)),
                  pl.BlockSpec((1, B_WIDTH), lambda i: (0, 0)),
                  pl.BlockSpec((1, B_WIDTH), lambda i: (0, 0))],
        out_specs=[pl.BlockSpec((tr, B_WIDTH), lambda i: (i, 0)),
                   pl.BlockSpec((None, CONV_W - 1, B_WIDTH), lambda i: (i // tiles, 0, 0))],
        out_shape=[jax.ShapeDtypeStruct((m, B_WIDTH), F32),
                   jax.ShapeDtypeStruct((bsz, CONV_W - 1, B_WIDTH), F32)],
        scratch_shapes=[pltpu.VMEM((CONV_HALO + tr, B_WIDTH), F32)],
        compiler_params=_params("arbitrary"),
        name="conv_prompt",
    )(h, h, jnp.broadcast_to(conv_w[:, None, :], (CONV_W, 8, B_WIDTH)), conv_b.reshape(1, B_WIDTH),
      ln_g.reshape(1, B_WIDTH), ln_b.reshape(1, B_WIDTH))


def _conv_sample_body(z_ref, st_ref, cw_ref, cb_ref, lg_ref, lb_ref, o_ref, buf_ref, pad_ref, *, bsz):
    hg = _glu(z_ref[...])
    pad_ref[:, 0:CONV_W - 1, :] = st_ref[...]
    pad_ref[:, CONV_W - 1:CONV_W - 1 + DEC_SEQ, :] = hg.reshape(bsz, DEC_SEQ, B_WIDTH)
    acc = jnp.zeros((bsz, DEC_SEQ, B_WIDTH), F32)
    for k in range(CONV_W):
        acc = acc + cw_ref[k:k + 1, :] * pad_ref[:, k:k + DEC_SEQ, :]
    y = _layer_norm(acc + cb_ref[...], lg_ref[...], lb_ref[...])
    o_ref[...] = (y * jax.nn.sigmoid(y)).reshape(bsz * DEC_SEQ, B_WIDTH)
    buf_ref[...] = pad_ref[:, DEC_SEQ:DEC_SEQ + CONV_W - 1, :]


def _conv_sample(h, state, conv_w, conv_b, ln_g, ln_b):
    m = h.shape[0]
    bsz = state.shape[0]
    body = functools.partial(_conv_sample_body, bsz=bsz)
    return pl.pallas_call(
        body,
        grid=(1,),
        in_specs=[pl.BlockSpec((m, 2 * B_WIDTH), lambda i: (0, OFF_B // (2 * B_WIDTH))),
                  pl.BlockSpec((bsz, CONV_W - 1, B_WIDTH), lambda i: (0, 0, 0)),
                  pl.BlockSpec((CONV_W, B_WIDTH), lambda i: (0, 0)),
                  pl.BlockSpec((1, B_WIDTH), lambda i: (0, 0)),
                  pl.BlockSpec((1, B_WIDTH), lambda i: (0, 0)),
                  pl.BlockSpec((1, B_WIDTH), lambda i: (0, 0))],
        out_specs=[pl.BlockSpec((m, B_WIDTH), lambda i: (0, 0)),
                   pl.BlockSpec((bsz, CONV_W - 1, B_WIDTH), lambda i: (0, 0, 0))],
        out_shape=[jax.ShapeDtypeStruct((m, B_WIDTH), F32),
                   jax.ShapeDtypeStruct((bsz, CONV_W - 1, B_WIDTH), F32)],
        scratch_shapes=[pltpu.VMEM((bsz, CONV_W - 1 + DEC_SEQ, B_WIDTH), F32)],
        compiler_params=_params("arbitrary"),
        name="conv_sample",
    )(h, state, conv_w, conv_b.reshape(1, B_WIDTH), ln_g.reshape(1, B_WIDTH), ln_b.reshape(1, B_WIDTH))


def _dot_nt(a, b):
    return lax.dot_general(a, b, (((1,), (1,)), ((), ())), preferred_element_type=F32)


def _attn_prompt_body(slope_ref, q_ref, kp_ref, kc_ref, vp_ref, vc_ref, o_ref, l_ref,
                      sm_ref, sx_ref, pp_ref, pc_ref, mb_ref, *, g, d, n_units, classes_per_pass):
    tq = C_WINDOW
    first_tile = pl.program_id(1) == 0
    hp = pl.program_id(2)
    row = lax.broadcasted_iota(jnp.int32, (tq, tq), 0)
    col = lax.broadcasted_iota(jnp.int32, (tq, tq), 1)
    upper = col > row
    diag = col == row
    diag0 = col == row + jnp.where(first_tile, tq, 0)
    valid0 = col <= row + jnp.where(first_tile, 0, tq)
    dist = jnp.where(upper, row - col + tq, row - col).astype(F32)
    low = lax.broadcasted_iota(jnp.int32, (tq, 128), 1) < C_HEAD_DIM
    sels = (low, jnp.logical_not(low))
    slopes = [slope_ref[g * C_HEADS + 2 * hp + hh] * float(d) for hh in range(2)]
    bias = [-(s * dist) for s in slopes]
    bias_x = [-(s * float(tq)) for s in slopes]
    items = [(u, rr) for u in range(n_units) for rr in range(classes_per_pass)]

    def one_pass(r0):
        def rows(v, rr):
            return pl.ds(v * tq * d + r0 + rr, tq, stride=d) if d > 1 else pl.ds(v * tq, tq)

        for n, (u, rr) in enumerate(items):
            q2 = q_ref[rows(u, rr), :] * ATTN_SCALE
            kc2 = kc_ref[rows(u, rr), :].astype(BF16)
            kp2 = (kp_ref[rows(0, rr), :] if u == 0 else kc_ref[rows(u - 1, rr), :]).astype(BF16)
            for hh in range(2):
                qm = jnp.where(sels[hh], q2, 0.0).astype(BF16)
                sp = _dot_nt(qm, kp2)
                sm = jnp.where(upper, sp, _dot_nt(qm, kc2)) + bias[hh]
                if u == 0:
                    sm = jnp.where(valid0, sm, NEG_INF)
                sm_ref[2 * n + hh] = sm
                sx_ref[2 * n + hh] = jnp.where(diag0 if u == 0 else diag, sp + bias_x[hh], NEG_INF)
        for n in range(len(items)):
            mx = []
            for hh in range(2):
                sm = sm_ref[2 * n + hh]
                sx = sx_ref[2 * n + hh]
                m = jnp.max(jnp.maximum(sm, sx), axis=-1, keepdims=True)
                p = jnp.exp(sm - m)
                pp_ref[2 * n + hh] = jnp.where(upper, p, jnp.exp(sx - m)).astype(BF16)
                pc_ref[2 * n + hh] = jnp.where(upper, 0.0, p).astype(BF16)
                mx.append(m)
            mb_ref[n] = jnp.where(low, mx[0], mx[1])
        for n, (u, rr) in enumerate(items):
            vc2 = vc_ref[rows(u, rr), :]
            vp2 = vp_ref[rows(0, rr), :] if u == 0 else vc_ref[rows(u - 1, rr), :]
            acc = []
            for hh in range(2):
                vpx = jnp.where(sels[hh], vp2, 1.0).astype(BF16)
                vcx = jnp.where(sels[hh], vc2, 1.0).astype(BF16)
                acc.append(jnp.dot(pp_ref[2 * n + hh], vpx, preferred_element_type=F32)
                           + jnp.dot(pc_ref[2 * n + hh], vcx, preferred_element_type=F32))
            den = pltpu.roll(jnp.where(low, acc[1], acc[0]), C_HEAD_DIM, axis=1)
            o_ref[rows(u, rr), :] = jnp.where(low, acc[0], acc[1]) / den
            l_ref[rows(u, rr), :] = mb_ref[n] + jnp.log(den)

    if classes_per_pass == d:
        one_pass(0)
    else:
        def body(it, carry):
            one_pass(it * classes_per_pass)
            return carry

        lax.fori_loop(0, d // classes_per_pass, body, 0)


def _attn_prompt(h, bsz, g):
    m = h.shape[0]
    t = m // bsz
    d = C_CONFIGS[g][1]
    unit_rows = C_WINDOW * d
    n_units = max(1, min(1024, t) // unit_rows)
    classes_per_pass = min(d, 8)
    n_items = 2 * n_units * classes_per_pass
    tile = n_units * unit_rows
    tiles = t // tile
    cq, ck, cv = ((off + g * C_GROUP_WIDTH) // 128 for off in (OFF_CQ, OFF_CK, OFF_CV))

    def cur(c):
        return pl.BlockSpec((tile, 128), lambda b, i, hp: (b * tiles + i, c + hp))

    def prev(c):
        return pl.BlockSpec((unit_rows, 128),
                            lambda b, i, hp: (jnp.maximum((b * tiles + i) * n_units - 1, 0), c + hp))

    out_spec = pl.BlockSpec((tile, 128), lambda b, i, hp: (b * tiles + i, hp))
    slopes = jnp.array([_slope(gg, hh) for gg in range(C_GROUPS) for hh in range(C_HEADS)], F32)
    body = functools.partial(_attn_prompt_body, g=g, d=d, n_units=n_units, classes_per_pass=classes_per_pass)
    return pl.pallas_call(
        body,
        grid=(bsz, tiles, C_HEADS // 2),
        in_specs=[pl.BlockSpec(memory_space=pltpu.SMEM), cur(cq), prev(ck), cur(ck), prev(cv), cur(cv)],
        out_specs=[out_spec, out_spec],
        out_shape=[jax.ShapeDtypeStruct((m, C_GROUP_WIDTH), F32)] * 2,
        scratch_shapes=[pltpu.VMEM((n_items, C_WINDOW, C_WINDOW), F32)] * 2
        + [pltpu.VMEM((n_items, C_WINDOW, C_WINDOW), BF16)] * 2
        + [pltpu.VMEM((n_items // 2, C_WINDOW, 128), F32)],
        compiler_params=_params("arbitrary", "arbitrary", "arbitrary"),
        name=f"attn_prompt_g{g}",
    )(slopes, h, h, h, h, h)


def _attn_sample_body(qkv_ref, slope_ref, k0_ref, v0_ref, k1_ref, v1_ref, k2_ref, v2_ref, o_ref):
    caches = ((k0_ref, v0_ref), (k1_ref, v1_ref), (k2_ref, v2_ref))
    rows = C_HEADS * DEC_SEQ
    rid = lax.broadcasted_iota(jnp.int32, (rows, 1), 0)
    i_row = rid & (DEC_SEQ - 1)
    own_lanes = (lax.broadcasted_iota(jnp.int32, (rows, C_GROUP_WIDTH), 1) >> 6) == (
        lax.broadcasted_iota(jnp.int32, (rows, C_GROUP_WIDTH), 0) >> 3)
    outs, lses = [], []
    for g, (w, d) in enumerate(C_CONFIGS):
        kt_ref, vt_ref = caches[g]
        q = qkv_ref[:, g * 512:(g + 1) * 512] * ATTN_SCALE
        kn = qkv_ref[:, (3 + g) * 512:(4 + g) * 512]
        vn = qkv_ref[:, (6 + g) * 512:(7 + g) * 512]
        qbd = jnp.where(own_lanes, jnp.concatenate([q] * C_HEADS, axis=0), 0.0)
        slope = slope_ref[g, :, 0:1]
        kt = kt_ref[...].reshape(C_GROUP_WIDTH, w).astype(BF16)
        vt = vt_ref[...].reshape(C_GROUP_WIDTH, w).astype(BF16)
        pos = lax.broadcasted_iota(jnp.int32, (rows, w), 1)
        back = (w + i_row) - pos
        ok = (pos >= i_row) if d == 1 else ((back & (d - 1)) == 0) & (pos >= i_row)
        s_c = jnp.dot(qbd.astype(BF16), kt, preferred_element_type=F32)
        s_c = jnp.where(ok, s_c - slope * back.astype(F32), NEG_INF)
        s_n = []
        for i2 in range(DEC_SEQ):
            s = jnp.sum(qbd * kn[i2:i2 + 1, :], axis=-1, keepdims=True)
            back_n = i_row - i2
            ok_n = (back_n >= 0) if d == 1 else ((back_n & (d - 1)) == 0) & (back_n >= 0)
            s_n.append(jnp.where(ok_n, s - slope * back_n.astype(F32), NEG_INF))
        mx = jnp.max(s_c, axis=-1, keepdims=True)
        for s in s_n:
            mx = jnp.maximum(mx, s)
        p_c = jnp.exp(s_c - mx)
        den = jnp.sum(p_c, axis=-1, keepdims=True)
        acc = _dot_nt(p_c.astype(BF16), vt)
        for i2, s in enumerate(s_n):
            p = jnp.exp(s - mx)
            den = den + p
            acc = acc + p * vn[i2:i2 + 1, :]
        outs.append(acc / den)
        lses.append(mx + jnp.log(den))
    mx = jnp.maximum(jnp.maximum(lses[0], lses[1]), lses[2])
    e = [jnp.exp(l - mx) for l in lses]
    tot = e[0] + e[1] + e[2]
    merged = (e[0] / tot) * outs[0] + (e[1] / tot) * outs[1] + (e[2] / tot) * outs[2]
    merged = jnp.where(own_lanes, merged, 0.0).reshape(C_HEADS, DEC_SEQ, C_GROUP_WIDTH)
    o_ref[...] = jnp.sum(merged, axis=0)


def _attn_sample(h, caches, layer):
    m = h.shape[0]
    bsz = m // DEC_SEQ
    qkv = h[:, OFF_CQ:D_IN]
    slopes = jnp.array([[_slope(g, hh) for hh in range(C_HEADS)] for g in range(C_GROUPS)], F32)
    slopes = jnp.broadcast_to(jnp.repeat(slopes, DEC_SEQ, axis=1)[:, :, None], (C_GROUPS, C_HEADS * DEC_SEQ, 128))
    in_specs = [pl.BlockSpec((DEC_SEQ, D_IN - OFF_CQ), lambda b: (b, 0)),
                pl.BlockSpec((C_GROUPS, C_HEADS * DEC_SEQ, 128), lambda b: (0, 0, 0))]
    views = []
    for g, (w, _) in enumerate(C_CONFIGS):
        for j in range(2):
            c = caches[2 * g + j]
            assert c.shape[2] == w, "the cache must hold the last W_g positions"
            views.append(jnp.transpose(c, (0, 1, 3, 4, 2)))
            in_specs.append(pl.BlockSpec((None, None, C_HEADS, C_HEAD_DIM, w), lambda b: (layer, b, 0, 0, 0)))
    return pl.pallas_call(
        _attn_sample_body,
        grid=(bsz,),
        in_specs=in_specs,
        out_specs=pl.BlockSpec((DEC_SEQ, C_GROUP_WIDTH), lambda b: (b, 0)),
        out_shape=jax.ShapeDtypeStruct((m, C_GROUP_WIDTH), F32),
        compiler_params=_params("arbitrary"),
        name="attn_sample",
    )(qkv, slopes, *views)


def _merge_out_body(*refs, split_c):
    if split_c:
        (x_ref, oa_ref, ob_ref, o0_ref, o1_ref, o2_ref, l0_ref, l1_ref, l2_ref,
         gate_ref, wb_ref, wo_ref, g_ref, out_ref) = refs
        l0, l1, l2 = l0_ref[...], l1_ref[...], l2_ref[...]
        mx = jnp.maximum(jnp.maximum(l0, l1), l2)
        e0, e1, e2 = jnp.exp(l0 - mx), jnp.exp(l1 - mx), jnp.exp(l2 - mx)
        tot = e0 + e1 + e2
        oc = (e0 / tot) * o0_ref[...] + (e1 / tot) * o1_ref[...] + (e2 / tot) * o2_ref[...]
    else:
        x_ref, oa_ref, ob_ref, oc_ref, gate_ref, wb_ref, wo_ref, g_ref, out_ref = refs
        oc = oc_ref[...]
    merged = None
    for n, o in enumerate((oa_ref[...], ob_ref[...], oc)):
        proj = jnp.dot(o.astype(BF16), wb_ref[n * BRANCH_WIDTH:(n + 1) * BRANCH_WIDTH, :],
                       preferred_element_type=F32)
        term = jax.nn.sigmoid(gate_ref[:, n * D_MODEL:(n + 1) * D_MODEL]) * proj
        merged = term if merged is None else merged + term
    y = jnp.dot(merged.astype(BF16), wo_ref[...], preferred_element_type=F32)
    out_ref[...] = x_ref[...] + _rms(y, g_ref[...])


def _merge_out(x, h, o_a, o_b, o_c, w_branch_bf16, w_out_bf16, gain):
    m = x.shape[0]
    tm = min(m, 256)
    split_c = isinstance(o_c, (list, tuple))
    c_args = list(o_c) if split_c else [o_c]
    row512 = pl.BlockSpec((tm, BRANCH_WIDTH), lambda i: (i, 0))
    in_specs = ([pl.BlockSpec((tm, D_MODEL), lambda i: (i, 0)), row512, row512] + [row512] * len(c_args)
                + [pl.BlockSpec((tm, N_BRANCH * D_MODEL), lambda i: (i, OFF_G // (N_BRANCH * D_MODEL))),
                   pl.BlockSpec((N_BRANCH * BRANCH_WIDTH, D_MODEL), lambda i: (0, 0)),
                   pl.BlockSpec((D_MODEL, D_MODEL), lambda i: (0, 0)),
                   pl.BlockSpec((1, D_MODEL), lambda i: (0, 0))])
    return pl.pallas_call(
        functools.partial(_merge_out_body, split_c=split_c),
        grid=(m // tm,),
        in_specs=in_specs,
        out_specs=pl.BlockSpec((tm, D_MODEL), lambda i: (i, 0)),
        out_shape=jax.ShapeDtypeStruct((m, D_MODEL), F32),
        compiler_params=_params("arbitrary"),
        name="merge_out",
    )(x, o_a, o_b, *c_args, h, w_branch_bf16, w_out_bf16, gain.reshape(1, D_MODEL))


FF_CHUNK = D_FF // 2


def _ffn_body(x_ref, g1_ref, w1_ref, w2_ref, g2_ref, out_ref):
    x = x_ref[...]
    xn = _rms(x, g1_ref[...]).astype(BF16)
    y = None
    for c in range(D_FF // FF_CHUNK):
        gate = jnp.dot(xn, w1_ref[:, c * FF_CHUNK:(c + 1) * FF_CHUNK], preferred_element_type=F32)
        up = jnp.dot(xn, w1_ref[:, D_FF + c * FF_CHUNK:D_FF + (c + 1) * FF_CHUNK], preferred_element_type=F32)
        act = (gate * jax.nn.sigmoid(gate) * up).astype(BF16)
        part = jnp.dot(act, w2_ref[c * FF_CHUNK:(c + 1) * FF_CHUNK, :], preferred_element_type=F32)
        y = part if y is None else y + part
    out_ref[...] = x + _rms(y, g2_ref[...])


def _ffn(x, gain_pre, w1_bf16, w2_bf16, gain_post):
    m = x.shape[0]
    tm = min(m, 256)
    return pl.pallas_call(
        _ffn_body,
        grid=(m // tm,),
        in_specs=[pl.BlockSpec((tm, D_MODEL), lambda i: (i, 0)),
                  pl.BlockSpec((1, D_MODEL), lambda i: (0, 0)),
                  pl.BlockSpec((D_MODEL, 2 * D_FF), lambda i: (0, 0)),
                  pl.BlockSpec((D_FF, D_MODEL), lambda i: (0, 0)),
                  pl.BlockSpec((1, D_MODEL), lambda i: (0, 0))],
        out_specs=pl.BlockSpec((tm, D_MODEL), lambda i: (i, 0)),
        out_shape=jax.ShapeDtypeStruct((m, D_MODEL), F32),
        compiler_params=_params("arbitrary"),
        name="ffn",
    )(x, gain_pre.reshape(1, D_MODEL), w1_bf16, w2_bf16, gain_post.reshape(1, D_MODEL))


def _heads(h, bsz, off, g, keep):
    t = h.shape[0] // bsz
    cols = h.reshape(bsz, t, D_IN)[:, t - keep:, off + g * C_GROUP_WIDTH:off + (g + 1) * C_GROUP_WIDTH]
    return cols.reshape(bsz, keep, C_HEADS, C_HEAD_DIM)


def _run_trunk(x3, is_prompt, conv_state, caches, p):
    bsz, t, _ = x3.shape
    x = x3.reshape(bsz * t, D_MODEL)
    depth = p["w_in"].shape[0]
    bufs, vns, kvs = [], [], []
    for l in range(depth):
        h = _in_proj(x, p["norm_pre_mix"][l], p["w_in"][l])
        o_a, vn = _token_mix(h, p["a_norm_g"][l], p["a_norm_b"][l], p["a_w_s"][l], p["a_b_s"][l],
                             CHUNK if is_prompt else t)
        if is_prompt:
            o_b, buf = _conv_prompt(h, bsz, p["b_conv_w"][l], p["b_conv_b"][l], p["b_norm_g"][l], p["b_norm_b"][l])
            parts = [_attn_prompt(h, bsz, g) for g in range(C_GROUPS)]
            o_c = [o for o, _ in parts] + [lse for _, lse in parts]
        else:
            o_b, buf = _conv_sample(h, conv_state[l], p["b_conv_w"][l], p["b_conv_b"][l],
                                    p["b_norm_g"][l], p["b_norm_b"][l])
            o_c = _attn_sample(h, caches, l)
        x = _merge_out(x, h, o_a, o_b, o_c, p["w_branch"][l], p["w_out"][l], p["norm_post_mix"][l])
        x = _ffn(x, p["norm_pre_ffn"][l], p["ffn_w_in"][l], p["ffn_w_out"][l], p["norm_post_ffn"][l])
        bufs.append(buf)
        vns.append(vn.reshape(bsz, t, A_WIDTH))
        layer_kv = []
        for g, (w, _) in enumerate(C_CONFIGS):
            keep = min(w, t) if is_prompt else t
            layer_kv.append(_heads(h, bsz, OFF_CK, g, keep))
            layer_kv.append(_heads(h, bsz, OFF_CV, g, keep))
        kvs.append(layer_kv)
    new_kv = [jnp.stack([kvs[l][j] for l in range(depth)], axis=0) for j in range(2 * C_GROUPS)]
    return x.reshape(bsz, t, D_MODEL), jnp.stack(bufs, axis=0), jnp.stack(vns, axis=0), new_kv


def kernel(x_prompt, x_sample, state_b_conv, cache_c0_k, cache_c0_v, cache_c1_k, cache_c1_v, cache_c2_k,
           cache_c2_v, norm_pre_mix, norm_post_mix, norm_pre_ffn, norm_post_ffn, w_in, a_norm_g, a_norm_b,
           a_w_s, a_b_s, b_conv_w, b_conv_b, b_norm_g, b_norm_b, w_branch, w_out, ffn_w_in, ffn_w_out):
    assert x_sample.shape[1] == DEC_SEQ
    w_in_cols = jnp.concatenate([w_in[..., REF_OFF_G:], w_in[..., :REF_OFF_G]], axis=-1)
    p = dict(norm_pre_mix=norm_pre_mix, norm_post_mix=norm_post_mix, norm_pre_ffn=norm_pre_ffn,
             norm_post_ffn=norm_post_ffn, w_in=w_in_cols.astype(BF16), a_norm_g=a_norm_g, a_norm_b=a_norm_b,
             a_w_s=a_w_s, a_b_s=a_b_s, b_conv_w=b_conv_w, b_conv_b=b_conv_b, b_norm_g=b_norm_g,
             b_norm_b=b_norm_b, w_branch=w_branch.astype(BF16), w_out=w_out.astype(BF16),
             ffn_w_in=ffn_w_in.astype(BF16), ffn_w_out=ffn_w_out.astype(BF16))
    caches = (cache_c0_k, cache_c0_v, cache_c1_k, cache_c1_v, cache_c2_k, cache_c2_v)
    y_prompt, buf_p, _, kv_p = _run_trunk(x_prompt, True, None, None, p)
    y_sample, buf_s, vn_s, kv_s = _run_trunk(x_sample, False, state_b_conv, caches, p)
    return (y_prompt, y_sample, buf_p, buf_s, vn_s, *kv_p, *kv_s)
```

```python
import functools

import jax
import jax.numpy as jnp
from jax import lax
from jax.experimental import pallas as pl
from jax.experimental.pallas import tpu as pltpu

F32 = jnp.float32
BF16 = jnp.bfloat16

D_MODEL = 1024
DEC_SEQ = 8
A_WIDTH = 512
A_GROUPS = 4
A_GROUP_DIM = 128
CHUNK = 128
B_WIDTH = 512
CONV_W = 31
CONV_HALO = 32
C_HEADS = 8
C_HEAD_DIM = 64
C_GROUP_WIDTH = 512
C_CONFIGS = ((128, 1), (512, 4), (2048, 16))
C_GROUPS = 3
C_WINDOW = 128
N_BRANCH = 3
BRANCH_WIDTH = 512
D_FF = 2816
EPS = 1e-6
NEG_INF = -1e30
ATTN_SCALE = C_HEAD_DIM ** -0.5
REF_OFF_G = 6656
OFF_G = 0
OFF_AU = 3072
OFF_AV = 3584
OFF_B = 4096
OFF_CQ = 5120
OFF_CK = 6656
OFF_CV = 8192
D_IN = 9728
COL_BLOCKS = D_IN // 512

VMEM_LIMIT = 56 * 1024 * 1024


def _slope(g, h):
    n = C_GROUPS * C_HEADS
    return 2.0 ** (-8.0 * (h * C_GROUPS + g + 1.0) / n)


def _params(*sem):
    return pltpu.CompilerParams(dimension_semantics=sem, vmem_limit_bytes=VMEM_LIMIT)


def _rms(x, gain):
    return x * lax.rsqrt(jnp.mean(x * x, axis=-1, keepdims=True) + EPS) * gain


def _layer_norm(x, gain, bias):
    mu = jnp.mean(x, axis=-1, keepdims=True)
    xc = x - mu
    var = jnp.mean(xc * xc, axis=-1, keepdims=True)
    return xc * lax.rsqrt(var + EPS) * gain + bias


Q_TILE0 = OFF_CQ // 512
KV_TILE0 = OFF_CK // 512
KV_WIDTH = D_IN - OFF_CK
TAIL = 2048


def _in_proj_sample_body(x_ref, g_ref, w_ref, o_ref, xn_ref):
    @pl.when(pl.program_id(1) == 0)
    def _():
        xn_ref[...] = _rms(x_ref[...], g_ref[...]).astype(BF16)

    o_ref[...] = jnp.dot(xn_ref[...], w_ref[...], preferred_element_type=F32)


def _in_proj_sample(x, gain, w_bf16):
    m = x.shape[0]
    tn = 512
    return pl.pallas_call(
        _in_proj_sample_body,
        grid=(1, D_IN // tn),
        in_specs=[pl.BlockSpec((m, D_MODEL), lambda i, j: (0, 0)),
                  pl.BlockSpec((1, D_MODEL), lambda i, j: (0, 0)),
                  pl.BlockSpec((D_MODEL, tn), lambda i, j: (0, j))],
        out_specs=pl.BlockSpec((m, tn), lambda i, j: (0, j)),
        out_shape=jax.ShapeDtypeStruct((m, D_IN), F32),
        scratch_shapes=[pltpu.VMEM((m, D_MODEL), BF16)],
        compiler_params=_params("arbitrary", "arbitrary"),
        name="in_proj_sample",
    )(x, gain.reshape(1, D_MODEL), w_bf16)


def _in_proj_body(x_ref, g_ref, w_ref, o_ref, kv_ref, xn_ref, res_ref, *, tm, tiles_per_seq):
    i = pl.program_id(0)
    j = pl.program_id(1)

    @pl.when(j == 0)
    def _():
        xn_ref[...] = _rms(x_ref[...], g_ref[...]).astype(BF16)

    res = jnp.dot(xn_ref[...], w_ref[...], preferred_element_type=F32)
    group = lax.rem(jnp.maximum(j - Q_TILE0, 0), C_GROUPS)
    dilated = j >= Q_TILE0

    @pl.when(jnp.logical_and(i % tiles_per_seq == tiles_per_seq - 1, j >= KV_TILE0))
    def _():
        kv_ref[...] = res

    @pl.when(jnp.logical_or(jnp.logical_not(dilated), group == 0))
    def _():
        o_ref[...] = res.astype(BF16)

    for g in range(1, C_GROUPS):
        d = C_CONFIGS[g][1]

        @pl.when(jnp.logical_and(dilated, group == g))
        def _():
            for c in range(res_ref.shape[0]):
                res_ref[c] = res[:, c * 128:(c + 1) * 128]
            for u in range(tm // (C_WINDOW * d)):
                for r in range(d):
                    dst = u * C_WINDOW * d + r * C_WINDOW
                    for c in range(res_ref.shape[0]):
                        o_ref[dst:dst + C_WINDOW, c * 128:(c + 1) * 128] = (
                            res_ref[c, pl.ds(u * C_WINDOW * d + r, C_WINDOW, stride=d), :].astype(BF16))


def _in_proj(x, gain, w_bf16, bsz):
    m = x.shape[0]
    t = m // bsz
    tm = TAIL
    tn = 512
    assert t % tm == 0
    tiles_per_seq = t // tm

    def kv_index(i, j):
        is_tail = i % tiles_per_seq == tiles_per_seq - 1
        return i // tiles_per_seq, jnp.where(is_tail, jnp.maximum(j - KV_TILE0, 0), 0)

    return pl.pallas_call(
        functools.partial(_in_proj_body, tm=tm, tiles_per_seq=tiles_per_seq),
        grid=(m // tm, D_IN // tn),
        in_specs=[pl.BlockSpec((tm, D_MODEL), lambda i, j: (i, 0)),
                  pl.BlockSpec((1, D_MODEL), lambda i, j: (0, 0)),
                  pl.BlockSpec((D_MODEL, tn), lambda i, j: (0, j))],
        out_specs=[pl.BlockSpec((tm, tn), lambda i, j: (i, j)),
                   pl.BlockSpec((tm, tn), kv_index)],
        out_shape=[jax.ShapeDtypeStruct((m, D_IN), BF16),
                   jax.ShapeDtypeStruct((bsz * TAIL, KV_WIDTH), F32)],
        scratch_shapes=[pltpu.VMEM((tm, D_MODEL), BF16), pltpu.VMEM((tn // 128, tm, 128), F32)],
        compiler_params=_params("arbitrary", "arbitrary"),
        name="in_proj",
    )(x, gain.reshape(1, D_MODEL), w_bf16)


def _token_mix_body(u_ref, v_ref, ng_ref, nb_ref, w_ref, bias_ref, o_ref, *maybe_vn_ref, r, chunk_len, n_sub):
    row = lax.broadcasted_iota(jnp.int32, (r, r), 0)
    col = lax.broadcasted_iota(jnp.int32, (r, r), 1)
    shift = chunk_len.bit_length() - 1
    keep = ((row >> shift) == (col >> shift)) & (col <= row)
    wm = [jnp.where(keep, w_ref[g], 0.0).astype(BF16) for g in range(A_GROUPS)]
    for c in range(n_sub):
        rows = slice(c * r, (c + 1) * r)
        vn = _layer_norm(v_ref[rows, :].astype(F32), ng_ref[...], nb_ref[...])
        for vn_ref in maybe_vn_ref:
            vn_ref[rows, :] = vn
        vnb = vn.astype(BF16)
        for g in range(A_GROUPS):
            lanes = slice(g * A_GROUP_DIM, (g + 1) * A_GROUP_DIM)
            mix = jnp.dot(wm[g], vnb[:, lanes], preferred_element_type=F32) + bias_ref[:, lanes]
            o_ref[rows, lanes] = u_ref[rows, lanes].astype(F32) * mix


def _token_mix(h, norm_g, norm_b, w_s, b_s, chunk_len, emit_vn):
    m = h.shape[0]
    if chunk_len == CHUNK:
        r, n_sub = CHUNK, min(4, m // CHUNK)
        w = w_s
        bias_rows = b_s.T
    else:
        r, n_sub = m, 1
        reps = m // chunk_len
        w = jnp.tile(w_s[:, :chunk_len, :chunk_len], (1, reps, reps))
        bias_rows = jnp.tile(b_s[:, :chunk_len].T, (reps, 1))
    bias = jnp.repeat(bias_rows, A_GROUP_DIM, axis=1)
    tr = r * n_sub
    body = functools.partial(_token_mix_body, r=r, chunk_len=chunk_len, n_sub=n_sub)
    return pl.pallas_call(
        body,
        grid=(m // tr,),
        in_specs=[pl.BlockSpec((tr, A_WIDTH), lambda i: (i, OFF_AU // A_WIDTH)),
                  pl.BlockSpec((tr, A_WIDTH), lambda i: (i, OFF_AV // A_WIDTH)),
                  pl.BlockSpec((1, A_WIDTH), lambda i: (0, 0)),
                  pl.BlockSpec((1, A_WIDTH), lambda i: (0, 0)),
                  pl.BlockSpec((A_GROUPS, r, r), lambda i: (0, 0, 0)),
                  pl.BlockSpec((r, A_WIDTH), lambda i: (0, 0))],
        out_specs=[pl.BlockSpec((tr, A_WIDTH), lambda i: (i, 0))] * (2 if emit_vn else 1),
        out_shape=[jax.ShapeDtypeStruct((m, A_WIDTH), F32)] * (2 if emit_vn else 1),
        compiler_params=_params("arbitrary"),
        name="token_mix",
    )(h, h, norm_g.reshape(1, A_WIDTH), norm_b.reshape(1, A_WIDTH), w, bias)


def _glu(z):
    return z[:, :B_WIDTH] * jax.nn.sigmoid(z[:, B_WIDTH:])


def _conv_prompt_body(z_ref, zh_ref, cw_ref, cb_ref, lg_ref, lb_ref, o_ref, buf_ref, pad_ref, *, tr, tiles):
    t = pl.program_id(0) % tiles
    hg = _glu(z_ref[...].astype(F32))
    pad_ref[0:CONV_HALO, :] = jnp.where(t == 0, 0.0, _glu(zh_ref[...].astype(F32)))
    pad_ref[CONV_HALO:CONV_HALO + tr, :] = hg
    first = CONV_HALO - (CONV_W - 1)
    sub = 32
    taps = [[k for k in range(CONV_W) if (first + k) % 8 == s] for s in range(8)]
    for blk in range(tr // sub):
        acc = jnp.zeros((sub // 8, 8, B_WIDTH), F32)
        for s in range(8):
            q_max = max((first + k) // 8 for k in taps[s])
            win = pad_ref[blk * sub + s:blk * sub + s + sub + 8 * q_max, :]
            for k in taps[s]:
                q = (first + k) // 8
                acc = acc + win[8 * q:8 * q + sub, :].reshape(sub // 8, 8, B_WIDTH) * cw_ref[k]
        y = _layer_norm(acc.reshape(sub, B_WIDTH) + cb_ref[...], lg_ref[...], lb_ref[...])
        o_ref[blk * sub:(blk + 1) * sub, :] = y * jax.nn.sigmoid(y)

    @pl.when(t == tiles - 1)
    def _():
        buf_ref[...] = pad_ref[CONV_HALO + tr - (CONV_W - 1):CONV_HALO + tr, :]


def _conv_prompt(h, bsz, conv_w, conv_b, ln_g, ln_b):
    m = h.shape[0]
    t = m // bsz
    tr = min(256, t)
    tiles = t // tr
    halo_per_tile = tr // CONV_HALO
    body = functools.partial(_conv_prompt_body, tr=tr, tiles=tiles)
    return pl.pallas_call(
        body,
        grid=(m // tr,),
        in_specs=[pl.BlockSpec((tr, 2 * B_WIDTH), lambda i: (i, OFF_B // (2 * B_WIDTH))),
                  pl.BlockSpec((CONV_HALO, 2 * B_WIDTH),
                               lambda i: (jnp.maximum(i * halo_per_tile - 1, 0), OFF_B // (2 * B_WIDTH))),
                  pl.BlockSpec((CONV_W, 8, B_WIDTH), lambda i: (0, 0, 0)),
                  pl.BlockSpec((1, B_WIDTH), lambda i: (0, 0)),
                  pl.BlockSpec((1, B_WIDTH), lambda i: (0, 0)),
                  pl.BlockSpec((1, B_WIDTH), lambda i: (0, 0))],
        out_specs=[pl.BlockSpec((tr, B_WIDTH), lambda i: (i, 0)),
                   pl.BlockSpec((None, CONV_W - 1, B_WIDTH), lambda i: (i // tiles, 0, 0))],
        out_shape=[jax.ShapeDtypeStruct((m, B_WIDTH), F32),
                   jax.ShapeDtypeStruct((bsz, CONV_W - 1, B_WIDTH), F32)],
        scratch_shapes=[pltpu.VMEM((CONV_HALO + tr, B_WIDTH), F32)],
        compiler_params=_params("arbitrary"),
        name="conv_prompt",
    )(h, h, jnp.broadcast_to(conv_w[:, None, :], (CONV_W, 8, B_WIDTH)), conv_b.reshape(1, B_WIDTH),
      ln_g.reshape(1, B_WIDTH), ln_b.reshape(1, B_WIDTH))


def _conv_sample_body(z_ref, st_ref, cw_ref, cb_ref, lg_ref, lb_ref, o_ref, buf_ref, pad_ref, *, bsz):
    hg = _glu(z_ref[...])
    pad_ref[:, 0:CONV_W - 1, :] = st_ref[...]
    pad_ref[:, CONV_W - 1:CONV_W - 1 + DEC_SEQ, :] = hg.reshape(bsz, DEC_SEQ, B_WIDTH)
    acc = jnp.zeros((bsz, DEC_SEQ, B_WIDTH), F32)
    for k in range(CONV_W):
        acc = acc + cw_ref[k:k + 1, :] * pad_ref[:, k:k + DEC_SEQ, :]
    y = _layer_norm(acc + cb_ref[...], lg_ref[...], lb_ref[...])
    o_ref[...] = (y * jax.nn.sigmoid(y)).reshape(bsz * DEC_SEQ, B_WIDTH)
    buf_ref[...] = pad_ref[:, DEC_SEQ:DEC_SEQ + CONV_W - 1, :]


def _conv_sample(h, state, conv_w, conv_b, ln_g, ln_b):
    m = h.shape[0]
    bsz = state.shape[0]
    body = functools.partial(_conv_sample_body, bsz=bsz)
    return pl.pallas_call(
        body,
        grid=(1,),
        in_specs=[pl.BlockSpec((m, 2 * B_WIDTH), lambda i: (0, OFF_B // (2 * B_WIDTH))),
                  pl.BlockSpec((bsz, CONV_W - 1, B_WIDTH), lambda i: (0, 0, 0)),
                  pl.BlockSpec((CONV_W, B_WIDTH), lambda i: (0, 0)),
                  pl.BlockSpec((1, B_WIDTH), lambda i: (0, 0)),
                  pl.BlockSpec((1, B_WIDTH), lambda i: (0, 0)),
                  pl.BlockSpec((1, B_WIDTH), lambda i: (0, 0))],
        out_specs=[pl.BlockSpec((m, B_WIDTH), lambda i: (0, 0)),
                   pl.BlockSpec((bsz, CONV_W - 1, B_WIDTH), lambda i: (0, 0, 0))],
        out_shape=[jax.ShapeDtypeStruct((m, B_WIDTH), F32),
                   jax.ShapeDtypeStruct((bsz, CONV_W - 1, B_WIDTH), F32)],
        scratch_shapes=[pltpu.VMEM((bsz, CONV_W - 1 + DEC_SEQ, B_WIDTH), F32)],
        compiler_params=_params("arbitrary"),
        name="conv_sample",
    )(h, state, conv_w, conv_b.reshape(1, B_WIDTH), ln_g.reshape(1, B_WIDTH), ln_b.reshape(1, B_WIDTH))


def _dot_nt(a, b):
    return lax.dot_general(a, b, (((1,), (1,)), ((), ())), preferred_element_type=F32)


def _attn_prompt_body(slope_ref, q_ref, kp_ref, kc_ref, vp_ref, vc_ref, o_ref, l_ref,
                      sm_ref, sx_ref, pp_ref, pc_ref, mb_ref, *, g, d, n_units, classes_per_pass):
    tq = C_WINDOW
    first_tile = pl.program_id(1) == 0
    hp = pl.program_id(2)
    row = lax.broadcasted_iota(jnp.int32, (tq, tq), 0)
    col = lax.broadcasted_iota(jnp.int32, (tq, tq), 1)
    upper = col > row
    diag = col == row
    diag0 = col == row + jnp.where(first_tile, tq, 0)
    valid0 = col <= row + jnp.where(first_tile, 0, tq)
    dist = jnp.where(upper, row - col + tq, row - col).astype(F32)
    low = lax.broadcasted_iota(jnp.int32, (tq, 128), 1) < C_HEAD_DIM
    sels = (low, jnp.logical_not(low))
    slopes = [slope_ref[g * C_HEADS + 2 * hp + hh] * float(d) for hh in range(2)]
    bias = [-(s * dist) for s in slopes]
    bias_x = [-(s * float(tq)) for s in slopes]
    items = [(u, rr) for u in range(n_units) for rr in range(classes_per_pass)]

    def one_pass(r0):
        def rows(v, rr):
            return pl.ds(pl.multiple_of(v * tq * d + (r0 + rr) * tq, tq), tq)

        def out_rows(v, rr):
            return pl.ds(v * tq * d + r0 + rr, tq, stride=d) if d > 1 else pl.ds(v * tq, tq)

        for n, (u, rr) in enumerate(items):
            q2 = q_ref[rows(u, rr), :] * ATTN_SCALE
            kc2 = kc_ref[rows(u, rr), :]
            kp2 = kp_ref[rows(0, rr), :] if u == 0 else kc_ref[rows(u - 1, rr), :]
            for hh in range(2):
                qm = jnp.where(sels[hh], q2, 0.0)
                sp = _dot_nt(qm, kp2)
                sm = jnp.where(upper, sp, _dot_nt(qm, kc2)) + bias[hh]
                if u == 0:
                    sm = jnp.where(valid0, sm, NEG_INF)
                sm_ref[2 * n + hh] = sm
                sx_ref[2 * n + hh] = jnp.where(diag0 if u == 0 else diag, sp + bias_x[hh], NEG_INF)
        for n in range(len(items)):
            mx = []
            for hh in range(2):
                sm = sm_ref[2 * n + hh]
                sx = sx_ref[2 * n + hh]
                m = jnp.max(jnp.maximum(sm, sx), axis=-1, keepdims=True)
                p = jnp.exp(sm - m)
                pp_ref[2 * n + hh] = jnp.where(upper, p, jnp.exp(sx - m)).astype(BF16)
                pc_ref[2 * n + hh] = jnp.where(upper, 0.0, p).astype(BF16)
                mx.append(m)
            mb_ref[n] = jnp.where(low, mx[0], mx[1])
        for n, (u, rr) in enumerate(items):
            vc2 = vc_ref[rows(u, rr), :]
            vp2 = vp_ref[rows(0, rr), :] if u == 0 else vc_ref[rows(u - 1, rr), :]
            acc = []
            for hh in range(2):
                vpx = jnp.where(sels[hh], vp2, 1.0)
                vcx = jnp.where(sels[hh], vc2, 1.0)
                acc.append(jnp.dot(pp_ref[2 * n + hh], vpx, preferred_element_type=F32)
                           + jnp.dot(pc_ref[2 * n + hh], vcx, preferred_element_type=F32))
            den = pltpu.roll(jnp.where(low, acc[1], acc[0]), C_HEAD_DIM, axis=1)
            o_ref[out_rows(u, rr), :] = jnp.where(low, acc[0], acc[1]) / den
            l_ref[out_rows(u, rr), :] = mb_ref[n] + jnp.log(den)

    if classes_per_pass == d:
        one_pass(0)
    else:
        def body(it, carry):
            one_pass(it * classes_per_pass)
            return carry

        lax.fori_loop(0, d // classes_per_pass, body, 0)


def _attn_prompt(h, bsz, g):
    m = h.shape[0]
    t = m // bsz
    d = C_CONFIGS[g][1]
    unit_rows = C_WINDOW * d
    n_units = max(1, min(1024, t) // unit_rows)
    classes_per_pass = min(d, 8)
    n_items = 2 * n_units * classes_per_pass
    tile = n_units * unit_rows
    tiles = t // tile
    cq, ck, cv = ((off + g * C_GROUP_WIDTH) // 128 for off in (OFF_CQ, OFF_CK, OFF_CV))

    def cur(c):
        return pl.BlockSpec((tile, 128), lambda b, i, hp: (b * tiles + i, c + hp))

    def prev(c):
        return pl.BlockSpec((unit_rows, 128),
                            lambda b, i, hp: (jnp.maximum((b * tiles + i) * n_units - 1, 0), c + hp))

    out_spec = pl.BlockSpec((tile, 128), lambda b, i, hp: (b * tiles + i, hp))
    slopes = jnp.array([_slope(gg, hh) for gg in range(C_GROUPS) for hh in range(C_HEADS)], F32)
    body = functools.partial(_attn_prompt_body, g=g, d=d, n_units=n_units, classes_per_pass=classes_per_pass)
    return pl.pallas_call(
        body,
        grid=(bsz, tiles, C_HEADS // 2),
        in_specs=[pl.BlockSpec(memory_space=pltpu.SMEM), cur(cq), prev(ck), cur(ck), prev(cv), cur(cv)],
        out_specs=[out_spec, out_spec],
        out_shape=[jax.ShapeDtypeStruct((m, C_GROUP_WIDTH), F32)] * 2,
        scratch_shapes=[pltpu.VMEM((n_items, C_WINDOW, C_WINDOW), F32)] * 2
        + [pltpu.VMEM((n_items, C_WINDOW, C_WINDOW), BF16)] * 2
        + [pltpu.VMEM((n_items // 2, C_WINDOW, 128), F32)],
        compiler_params=_params("arbitrary", "arbitrary", "arbitrary"),
        name=f"attn_prompt_g{g}",
    )(slopes, h, h, h, h, h)


def _attn_sample_body(qkv_ref, slope_ref, k0_ref, v0_ref, k1_ref, v1_ref, k2_ref, v2_ref, o_ref):
    caches = ((k0_ref, v0_ref), (k1_ref, v1_ref), (k2_ref, v2_ref))
    rows = C_HEADS * DEC_SEQ
    rid = lax.broadcasted_iota(jnp.int32, (rows, 1), 0)
    i_row = rid & (DEC_SEQ - 1)
    own_lanes = (lax.broadcasted_iota(jnp.int32, (rows, C_GROUP_WIDTH), 1) >> 6) == (
        lax.broadcasted_iota(jnp.int32, (rows, C_GROUP_WIDTH), 0) >> 3)
    outs, lses = [], []
    for g, (w, d) in enumerate(C_CONFIGS):
        kt_ref, vt_ref = caches[g]
        q = qkv_ref[:, g * 512:(g + 1) * 512] * ATTN_SCALE
        kn = qkv_ref[:, (3 + g) * 512:(4 + g) * 512]
        vn = qkv_ref[:, (6 + g) * 512:(7 + g) * 512]
        qbd = jnp.where(own_lanes, jnp.concatenate([q] * C_HEADS, axis=0), 0.0)
        slope = slope_ref[g, :, 0:1]
        kt = kt_ref[...].reshape(C_GROUP_WIDTH, w).astype(BF16)
        vt = vt_ref[...].reshape(C_GROUP_WIDTH, w).astype(BF16)
        pos = lax.broadcasted_iota(jnp.int32, (rows, w), 1)
        back = (w + i_row) - pos
        ok = (pos >= i_row) if d == 1 else ((back & (d - 1)) == 0) & (pos >= i_row)
        s_c = jnp.dot(qbd.astype(BF16), kt, preferred_element_type=F32)
        s_c = jnp.where(ok, s_c - slope * back.astype(F32), NEG_INF)
        s_n = []
        for i2 in range(DEC_SEQ):
            s = jnp.sum(qbd * kn[i2:i2 + 1, :], axis=-1, keepdims=True)
            back_n = i_row - i2
            ok_n = (back_n >= 0) if d == 1 else ((back_n & (d - 1)) == 0) & (back_n >= 0)
            s_n.append(jnp.where(ok_n, s - slope * back_n.astype(F32), NEG_INF))
        mx = jnp.max(s_c, axis=-1, keepdims=True)
        for s in s_n:
            mx = jnp.maximum(mx, s)
        p_c = jnp.exp(s_c - mx)
        den = jnp.sum(p_c, axis=-1, keepdims=True)
        acc = _dot_nt(p_c.astype(BF16), vt)
        for i2, s in enumerate(s_n):
            p = jnp.exp(s - mx)
            den = den + p
            acc = acc + p * vn[i2:i2 + 1, :]
        outs.append(acc / den)
        lses.append(mx + jnp.log(den))
    mx = jnp.maximum(jnp.maximum(lses[0], lses[1]), lses[2])
    e = [jnp.exp(l - mx) for l in lses]
    tot = e[0] + e[1] + e[2]
    merged = (e[0] / tot) * outs[0] + (e[1] / tot) * outs[1] + (e[2] / tot) * outs[2]
    merged = jnp.where(own_lanes, merged, 0.0).reshape(C_HEADS, DEC_SEQ, C_GROUP_WIDTH)
    o_ref[...] = jnp.sum(merged, axis=0)


def _attn_sample(h, caches, layer):
    m = h.shape[0]
    bsz = m // DEC_SEQ
    qkv = h[:, OFF_CQ:D_IN]
    slopes = jnp.array([[_slope(g, hh) for hh in range(C_HEADS)] for g in range(C_GROUPS)], F32)
    slopes = jnp.broadcast_to(jnp.repeat(slopes, DEC_SEQ, axis=1)[:, :, None], (C_GROUPS, C_HEADS * DEC_SEQ, 128))
    in_specs = [pl.BlockSpec((DEC_SEQ, D_IN - OFF_CQ), lambda b: (b, 0)),
                pl.BlockSpec((C_GROUPS, C_HEADS * DEC_SEQ, 128), lambda b: (0, 0, 0))]
    views = []
    for g, (w, _) in enumerate(C_CONFIGS):
        for j in range(2):
            c = caches[2 * g + j]
            assert c.shape[2] == w, "the cache must hold the last W_g positions"
            views.append(jnp.transpose(c, (0, 1, 3, 4, 2)))
            in_specs.append(pl.BlockSpec((None, None, C_HEADS, C_HEAD_DIM, w), lambda b: (layer, b, 0, 0, 0)))
    return pl.pallas_call(
        _attn_sample_body,
        grid=(bsz,),
        in_specs=in_specs,
        out_specs=pl.BlockSpec((DEC_SEQ, C_GROUP_WIDTH), lambda b: (b, 0)),
        out_shape=jax.ShapeDtypeStruct((m, C_GROUP_WIDTH), F32),
        compiler_params=_params("arbitrary"),
        name="attn_sample",
    )(qkv, slopes, *views)


def _merge_out_body(*refs, split_c):
    if split_c:
        (x_ref, oa_ref, ob_ref, o0_ref, o1_ref, o2_ref, l0_ref, l1_ref, l2_ref,
         gate_ref, wb_ref, wo_ref, g_ref, out_ref) = refs
        l0, l1, l2 = l0_ref[...], l1_ref[...], l2_ref[...]
        mx = jnp.maximum(jnp.maximum(l0, l1), l2)
        e0, e1, e2 = jnp.exp(l0 - mx), jnp.exp(l1 - mx), jnp.exp(l2 - mx)
        tot = e0 + e1 + e2
        oc = (e0 / tot) * o0_ref[...] + (e1 / tot) * o1_ref[...] + (e2 / tot) * o2_ref[...]
    else:
        x_ref, oa_ref, ob_ref, oc_ref, gate_ref, wb_ref, wo_ref, g_ref, out_ref = refs
        oc = oc_ref[...]
    merged = None
    for n, o in enumerate((oa_ref[...], ob_ref[...], oc)):
        proj = jnp.dot(o.astype(BF16), wb_ref[n * BRANCH_WIDTH:(n + 1) * BRANCH_WIDTH, :],
                       preferred_element_type=F32)
        term = jax.nn.sigmoid(gate_ref[:, n * D_MODEL:(n + 1) * D_MODEL].astype(F32)) * proj
        merged = term if merged is None else merged + term
    y = jnp.dot(merged.astype(BF16), wo_ref[...], preferred_element_type=F32)
    out_ref[...] = x_ref[...] + _rms(y, g_ref[...])


def _merge_out(x, h, o_a, o_b, o_c, w_branch_bf16, w_out_bf16, gain):
    m = x.shape[0]
    tm = min(m, 512)
    split_c = isinstance(o_c, (list, tuple))
    c_args = list(o_c) if split_c else [o_c]
    row512 = pl.BlockSpec((tm, BRANCH_WIDTH), lambda i: (i, 0))
    in_specs = ([pl.BlockSpec((tm, D_MODEL), lambda i: (i, 0)), row512, row512] + [row512] * len(c_args)
                + [pl.BlockSpec((tm, N_BRANCH * D_MODEL), lambda i: (i, OFF_G // (N_BRANCH * D_MODEL))),
                   pl.BlockSpec((N_BRANCH * BRANCH_WIDTH, D_MODEL), lambda i: (0, 0)),
                   pl.BlockSpec((D_MODEL, D_MODEL), lambda i: (0, 0)),
                   pl.BlockSpec((1, D_MODEL), lambda i: (0, 0))])
    return pl.pallas_call(
        functools.partial(_merge_out_body, split_c=split_c),
        grid=(m // tm,),
        in_specs=in_specs,
        out_specs=pl.BlockSpec((tm, D_MODEL), lambda i: (i, 0)),
        out_shape=jax.ShapeDtypeStruct((m, D_MODEL), F32),
        compiler_params=_params("arbitrary"),
        name="merge_out",
    )(x, o_a, o_b, *c_args, h, w_branch_bf16, w_out_bf16, gain.reshape(1, D_MODEL))


FF_CHUNK = D_FF // 2


def _ffn_body(x_ref, g1_ref, w1_ref, w2_ref, g2_ref, out_ref):
    x = x_ref[...]
    xn = _rms(x, g1_ref[...]).astype(BF16)
    y = None
    for c in range(D_FF // FF_CHUNK):
        gate = jnp.dot(xn, w1_ref[:, c * FF_CHUNK:(c + 1) * FF_CHUNK], preferred_element_type=F32)
        up = jnp.dot(xn, w1_ref[:, D_FF + c * FF_CHUNK:D_FF + (c + 1) * FF_CHUNK], preferred_element_type=F32)
        act = (gate * jax.nn.sigmoid(gate) * up).astype(BF16)
        part = jnp.dot(act, w2_ref[c * FF_CHUNK:(c + 1) * FF_CHUNK, :], preferred_element_type=F32)
        y = part if y is None else y + part
    out_ref[...] = x + _rms(y, g2_ref[...])


def _ffn(x, gain_pre, w1_bf16, w2_bf16, gain_post):
    m = x.shape[0]
    tm = min(m, 512)
    resident = dict(pipeline_mode=pl.Buffered(1))
    return pl.pallas_call(
        _ffn_body,
        grid=(m // tm,),
        in_specs=[pl.BlockSpec((tm, D_MODEL), lambda i: (i, 0)),
                  pl.BlockSpec((1, D_MODEL), lambda i: (0, 0)),
                  pl.BlockSpec((D_MODEL, 2 * D_FF), lambda i: (0, 0), **resident),
                  pl.BlockSpec((D_FF, D_MODEL), lambda i: (0, 0), **resident),
                  pl.BlockSpec((1, D_MODEL), lambda i: (0, 0))],
        out_specs=pl.BlockSpec((tm, D_MODEL), lambda i: (i, 0)),
        out_shape=jax.ShapeDtypeStruct((m, D_MODEL), F32),
        compiler_params=_params("arbitrary"),
        name="ffn",
    )(x, gain_pre.reshape(1, D_MODEL), w1_bf16, w2_bf16, gain_post.reshape(1, D_MODEL))


def _split_heads(cols):
    return cols.reshape(cols.shape[0], cols.shape[1], C_HEADS, C_HEAD_DIM)


def _prompt_trunk(x3, p):
    bsz, t, _ = x3.shape
    x = x3.reshape(bsz * t, D_MODEL)
    depth = p["w_in"].shape[0]
    bufs, kvs = [], []
    for l in range(depth):
        h, kv_tail = _in_proj(x, p["norm_pre_mix"][l], p["w_in"][l], bsz)
        (o_a,) = _token_mix(h, p["a_norm_g"][l], p["a_norm_b"][l], p["a_w_s"][l], p["a_b_s"][l], CHUNK, False)
        o_b, buf = _conv_prompt(h, bsz, p["b_conv_w"][l], p["b_conv_b"][l], p["b_norm_g"][l], p["b_norm_b"][l])
        parts = [_attn_prompt(h, bsz, g) for g in range(C_GROUPS)]
        o_c = [o for o, _ in parts] + [lse for _, lse in parts]
        x = _merge_out(x, h, o_a, o_b, o_c, p["w_branch"][l], p["w_out"][l], p["norm_post_mix"][l])
        x = _ffn(x, p["norm_pre_ffn"][l], p["ffn_w_in"][l], p["ffn_w_out"][l], p["norm_post_ffn"][l])
        bufs.append(buf)
        kv3 = kv_tail.reshape(bsz, TAIL, KV_WIDTH)
        layer_kv = []
        for g, (w, _) in enumerate(C_CONFIGS):
            keep = min(w, t)
            for off in (g * C_GROUP_WIDTH, C_GROUPS * C_GROUP_WIDTH + g * C_GROUP_WIDTH):
                layer_kv.append(_split_heads(kv3[:, TAIL - keep:, off:off + C_GROUP_WIDTH]))
        kvs.append(layer_kv)
    new_kv = [jnp.stack([kvs[l][j] for l in range(depth)], axis=0) for j in range(2 * C_GROUPS)]
    return x.reshape(bsz, t, D_MODEL), jnp.stack(bufs, axis=0), new_kv


def _sample_trunk(x3, conv_state, caches, p):
    bsz, t, _ = x3.shape
    x = x3.reshape(bsz * t, D_MODEL)
    depth = p["w_in"].shape[0]
    bufs, vns, kvs = [], [], []
    for l in range(depth):
        h = _in_proj_sample(x, p["norm_pre_mix"][l], p["w_in"][l])
        o_a, vn = _token_mix(h, p["a_norm_g"][l], p["a_norm_b"][l], p["a_w_s"][l], p["a_b_s"][l], t, True)
        o_b, buf = _conv_sample(h, conv_state[l], p["b_conv_w"][l], p["b_conv_b"][l],
                                p["b_norm_g"][l], p["b_norm_b"][l])
        o_c = _attn_sample(h, caches, l)
        x = _merge_out(x, h, o_a, o_b, o_c, p["w_branch"][l], p["w_out"][l], p["norm_post_mix"][l])
        x = _ffn(x, p["norm_pre_ffn"][l], p["ffn_w_in"][l], p["ffn_w_out"][l], p["norm_post_ffn"][l])
        bufs.append(buf)
        vns.append(vn.reshape(bsz, t, A_WIDTH))
        h3 = h.reshape(bsz, t, D_IN)
        kvs.append([_split_heads(h3[:, :, off + g * C_GROUP_WIDTH:off + (g + 1) * C_GROUP_WIDTH])
                    for g in range(C_GROUPS) for off in (OFF_CK, OFF_CV)])
    new_kv = [jnp.stack([kvs[l][j] for l in range(depth)], axis=0) for j in range(2 * C_GROUPS)]
    return x.reshape(bsz, t, D_MODEL), jnp.stack(bufs, axis=0), jnp.stack(vns, axis=0), new_kv


def kernel(x_prompt, x_sample, state_b_conv, cache_c0_k, cache_c0_v, cache_c1_k, cache_c1_v, cache_c2_k,
           cache_c2_v, norm_pre_mix, norm_post_mix, norm_pre_ffn, norm_post_ffn, w_in, a_norm_g, a_norm_b,
           a_w_s, a_b_s, b_conv_w, b_conv_b, b_norm_g, b_norm_b, w_branch, w_out, ffn_w_in, ffn_w_out):
    assert x_sample.shape[1] == DEC_SEQ
    w_in_cols = jnp.concatenate([w_in[..., REF_OFF_G:], w_in[..., :REF_OFF_G]], axis=-1)
    p = dict(norm_pre_mix=norm_pre_mix, norm_post_mix=norm_post_mix, norm_pre_ffn=norm_pre_ffn,
             norm_post_ffn=norm_post_ffn, w_in=w_in_cols.astype(BF16), a_norm_g=a_norm_g, a_norm_b=a_norm_b,
             a_w_s=a_w_s, a_b_s=a_b_s, b_conv_w=b_conv_w, b_conv_b=b_conv_b, b_norm_g=b_norm_g,
             b_norm_b=b_norm_b, w_branch=w_branch.astype(BF16), w_out=w_out.astype(BF16),
             ffn_w_in=ffn_w_in.astype(BF16), ffn_w_out=ffn_w_out.astype(BF16))
    caches = (cache_c0_k, cache_c0_v, cache_c1_k, cache_c1_v, cache_c2_k, cache_c2_v)
    y_prompt, buf_p, kv_p = _prompt_trunk(x_prompt, p)
    y_sample, buf_s, vn_s, kv_s = _sample_trunk(x_sample, state_b_conv, caches, p)
    return (y_prompt, y_sample, buf_p, buf_s, vn_s, *kv_p, *kv_s)
```

```python
import functools

import jax
import jax.numpy as jnp
from jax import lax
from jax.experimental import pallas as pl
from jax.experimental.pallas import tpu as pltpu

F32 = jnp.float32
BF16 = jnp.bfloat16

D_MODEL = 1024
DEC_SEQ = 8
A_WIDTH = 512
A_GROUPS = 4
A_GROUP_DIM = 128
CHUNK = 128
B_WIDTH = 512
CONV_W = 31
CONV_HALO = 32
C_HEADS = 8
C_HEAD_DIM = 64
C_GROUP_WIDTH = 512
C_CONFIGS = ((128, 1), (512, 4), (2048, 16))
C_GROUPS = 3
C_WINDOW = 128
N_BRANCH = 3
BRANCH_WIDTH = 512
D_FF = 2816
EPS = 1e-6
NEG_INF = -1e30
ATTN_SCALE = C_HEAD_DIM ** -0.5
REF_OFF_G = 6656
OFF_G = 0
OFF_AU = 3072
OFF_AV = 3584
OFF_B = 4096
OFF_CQ = 5120
OFF_CK = 6656
OFF_CV = 8192
D_IN = 9728
COL_BLOCKS = D_IN // 512

VMEM_LIMIT = 56 * 1024 * 1024


def _slope(g, h):
    n = C_GROUPS * C_HEADS
    return 2.0 ** (-8.0 * (h * C_GROUPS + g + 1.0) / n)


def _params(*sem):
    return pltpu.CompilerParams(dimension_semantics=sem, vmem_limit_bytes=VMEM_LIMIT)


def _rms(x, gain):
    return x * lax.rsqrt(jnp.mean(x * x, axis=-1, keepdims=True) + EPS) * gain


def _layer_norm(x, gain, bias):
    mu = jnp.mean(x, axis=-1, keepdims=True)
    xc = x - mu
    var = jnp.mean(xc * xc, axis=-1, keepdims=True)
    return xc * lax.rsqrt(var + EPS) * gain + bias


Q_TILE0 = OFF_CQ // 512
KV_TILE0 = OFF_CK // 512
KV_WIDTH = D_IN - OFF_CK
TAIL = 2048


def _in_proj_sample_body(x_ref, g_ref, w_ref, o_ref, xn_ref):
    @pl.when(pl.program_id(1) == 0)
    def _():
        xn_ref[...] = _rms(x_ref[...], g_ref[...]).astype(BF16)

    o_ref[...] = jnp.dot(xn_ref[...], w_ref[...].astype(BF16), preferred_element_type=F32)


def _w_in_tile(j):
    return lax.rem(j + REF_OFF_G // 512, COL_BLOCKS)


def _in_proj_sample(x, gain, w_in, layer):
    m = x.shape[0]
    tn = 512
    return pl.pallas_call(
        _in_proj_sample_body,
        grid=(1, D_IN // tn),
        in_specs=[pl.BlockSpec((m, D_MODEL), lambda i, j: (0, 0)),
                  pl.BlockSpec((1, D_MODEL), lambda i, j: (0, 0)),
                  pl.BlockSpec((None, D_MODEL, tn), lambda i, j: (layer, 0, _w_in_tile(j)))],
        out_specs=pl.BlockSpec((m, tn), lambda i, j: (0, j)),
        out_shape=jax.ShapeDtypeStruct((m, D_IN), F32),
        scratch_shapes=[pltpu.VMEM((m, D_MODEL), BF16)],
        compiler_params=_params("arbitrary", "arbitrary"),
        name="in_proj_sample",
    )(x, gain.reshape(1, D_MODEL), w_in)


def _in_proj_body(x_ref, g_ref, w_ref, o_ref, kv_ref, xn_ref, res_ref, *, tm, tiles_per_seq):
    i = pl.program_id(0)
    j = pl.program_id(1)

    @pl.when(j == 0)
    def _():
        xn_ref[...] = _rms(x_ref[...], g_ref[...]).astype(BF16)

    res = jnp.dot(xn_ref[...], w_ref[...].astype(BF16), preferred_element_type=F32)
    group = lax.rem(jnp.maximum(j - Q_TILE0, 0), C_GROUPS)
    dilated = j >= Q_TILE0

    @pl.when(jnp.logical_and(i % tiles_per_seq == tiles_per_seq - 1, j >= KV_TILE0))
    def _():
        kv_ref[...] = res

    @pl.when(jnp.logical_or(jnp.logical_not(dilated), group == 0))
    def _():
        o_ref[...] = res.astype(BF16)

    for g in range(1, C_GROUPS):
        d = C_CONFIGS[g][1]

        @pl.when(jnp.logical_and(dilated, group == g))
        def _():
            for c in range(res_ref.shape[0]):
                res_ref[c] = res[:, c * 128:(c + 1) * 128]
            for u in range(tm // (C_WINDOW * d)):
                for r in range(d):
                    dst = u * C_WINDOW * d + r * C_WINDOW
                    for c in range(res_ref.shape[0]):
                        o_ref[dst:dst + C_WINDOW, c * 128:(c + 1) * 128] = (
                            res_ref[c, pl.ds(u * C_WINDOW * d + r, C_WINDOW, stride=d), :].astype(BF16))


def _in_proj(x, gain, w_in, layer, bsz):
    m = x.shape[0]
    t = m // bsz
    tm = TAIL
    tn = 512
    assert t % tm == 0
    tiles_per_seq = t // tm

    def kv_index(i, j):
        is_tail = i % tiles_per_seq == tiles_per_seq - 1
        return i // tiles_per_seq, jnp.where(is_tail, jnp.maximum(j - KV_TILE0, 0), 0)

    return pl.pallas_call(
        functools.partial(_in_proj_body, tm=tm, tiles_per_seq=tiles_per_seq),
        grid=(m // tm, D_IN // tn),
        in_specs=[pl.BlockSpec((tm, D_MODEL), lambda i, j: (i, 0)),
                  pl.BlockSpec((1, D_MODEL), lambda i, j: (0, 0)),
                  pl.BlockSpec((None, D_MODEL, tn), lambda i, j: (layer, 0, _w_in_tile(j)))],
        out_specs=[pl.BlockSpec((tm, tn), lambda i, j: (i, j)),
                   pl.BlockSpec((tm, tn), kv_index)],
        out_shape=[jax.ShapeDtypeStruct((m, D_IN), BF16),
                   jax.ShapeDtypeStruct((bsz * TAIL, KV_WIDTH), F32)],
        scratch_shapes=[pltpu.VMEM((tm, D_MODEL), BF16), pltpu.VMEM((tn // 128, tm, 128), F32)],
        compiler_params=_params("arbitrary", "arbitrary"),
        name="in_proj",
    )(x, gain.reshape(1, D_MODEL), w_in)


def _token_mix_body(u_ref, v_ref, ng_ref, nb_ref, w_ref, bias_ref, o_ref, *maybe_vn_ref, r, chunk_len, n_sub):
    row = lax.broadcasted_iota(jnp.int32, (r, r), 0)
    col = lax.broadcasted_iota(jnp.int32, (r, r), 1)
    shift = chunk_len.bit_length() - 1
    keep = ((row >> shift) == (col >> shift)) & (col <= row)
    wm = [jnp.where(keep, w_ref[g], 0.0).astype(BF16) for g in range(A_GROUPS)]
    for c in range(n_sub):
        rows = slice(c * r, (c + 1) * r)
        vn = _layer_norm(v_ref[rows, :].astype(F32), ng_ref[...], nb_ref[...])
        for vn_ref in maybe_vn_ref:
            vn_ref[rows, :] = vn
        vnb = vn.astype(BF16)
        for g in range(A_GROUPS):
            lanes = slice(g * A_GROUP_DIM, (g + 1) * A_GROUP_DIM)
            mix = jnp.dot(wm[g], vnb[:, lanes], preferred_element_type=F32) + bias_ref[:, lanes]
            o_ref[rows, lanes] = u_ref[rows, lanes].astype(F32) * mix


def _token_mix(h, norm_g, norm_b, w_s, b_s, chunk_len, emit_vn):
    m = h.shape[0]
    if chunk_len == CHUNK:
        r, n_sub = CHUNK, min(4, m // CHUNK)
        w = w_s
        bias_rows = b_s.T
    else:
        r, n_sub = m, 1
        reps = m // chunk_len
        w = jnp.tile(w_s[:, :chunk_len, :chunk_len], (1, reps, reps))
        bias_rows = jnp.tile(b_s[:, :chunk_len].T, (reps, 1))
    bias = jnp.repeat(bias_rows, A_GROUP_DIM, axis=1)
    tr = r * n_sub
    body = functools.partial(_token_mix_body, r=r, chunk_len=chunk_len, n_sub=n_sub)
    return pl.pallas_call(
        body,
        grid=(m // tr,),
        in_specs=[pl.BlockSpec((tr, A_WIDTH), lambda i: (i, OFF_AU // A_WIDTH)),
                  pl.BlockSpec((tr, A_WIDTH), lambda i: (i, OFF_AV // A_WIDTH)),
                  pl.BlockSpec((1, A_WIDTH), lambda i: (0, 0)),
                  pl.BlockSpec((1, A_WIDTH), lambda i: (0, 0)),
                  pl.BlockSpec((A_GROUPS, r, r), lambda i: (0, 0, 0)),
                  pl.BlockSpec((r, A_WIDTH), lambda i: (0, 0))],
        out_specs=[pl.BlockSpec((tr, A_WIDTH), lambda i: (i, 0))] * (2 if emit_vn else 1),
        out_shape=[jax.ShapeDtypeStruct((m, A_WIDTH), F32)] * (2 if emit_vn else 1),
        compiler_params=_params("arbitrary"),
        name="token_mix",
    )(h, h, norm_g.reshape(1, A_WIDTH), norm_b.reshape(1, A_WIDTH), w, bias)


def _glu(z):
    return z[:, :B_WIDTH] * jax.nn.sigmoid(z[:, B_WIDTH:])


def _conv_prompt_body(z_ref, zh_ref, cw_ref, cb_ref, lg_ref, lb_ref, o_ref, buf_ref, pad_ref, *, tr, tiles):
    t = pl.program_id(0) % tiles
    hg = _glu(z_ref[...].astype(F32))
    pad_ref[0:CONV_HALO, :] = jnp.where(t == 0, 0.0, _glu(zh_ref[...].astype(F32)))
    pad_ref[CONV_HALO:CONV_HALO + tr, :] = hg
    first = CONV_HALO - (CONV_W - 1)
    sub = 32
    taps = [[k for k in range(CONV_W) if (first + k) % 8 == s] for s in range(8)]
    for blk in range(tr // sub):
        acc = jnp.zeros((sub // 8, 8, B_WIDTH), F32)
        for s in range(8):
            q_max = max((first + k) // 8 for k in taps[s])
            win = pad_ref[blk * sub + s:blk * sub + s + sub + 8 * q_max, :]
            for k in taps[s]:
                q = (first + k) // 8
                acc = acc + win[8 * q:8 * q + sub, :].reshape(sub // 8, 8, B_WIDTH) * cw_ref[k]
        y = _layer_norm(acc.reshape(sub, B_WIDTH) + cb_ref[...], lg_ref[...], lb_ref[...])
        o_ref[blk * sub:(blk + 1) * sub, :] = y * jax.nn.sigmoid(y)

    @pl.when(t == tiles - 1)
    def _():
        buf_ref[...] = pad_ref[CONV_HALO + tr - (CONV_W - 1):CONV_HALO + tr, :]


def _conv_prompt(h, bsz, conv_w, conv_b, ln_g, ln_b):
    m = h.shape[0]
    t = m // bsz
    tr = min(256, t)
    tiles = t // tr
    halo_per_tile = tr // CONV_HALO
    body = functools.partial(_conv_prompt_body, tr=tr, tiles=tiles)
    return pl.pallas_call(
        body,
        grid=(m // tr,),
        in_specs=[pl.BlockSpec((tr, 2 * B_WIDTH), lambda i: (i, OFF_B // (2 * B_WIDTH))),
                  pl.BlockSpec((CONV_HALO, 2 * B_WIDTH),
                               lambda i: (jnp.maximum(i * halo_per_tile - 1, 0), OFF_B // (2 * B_WIDTH))),
                  pl.BlockSpec((CONV_W, 8, B_WIDTH), lambda i: (0, 0, 0)),
                  pl.BlockSpec((1, B_WIDTH), lambda i: (0, 0)),
                  pl.BlockSpec((1, B_WIDTH), lambda i: (0, 0)),
                  pl.BlockSpec((1, B_WIDTH), lambda i: (0, 0))],
        out_specs=[pl.BlockSpec((tr, B_WIDTH), lambda i: (i, 0)),
                   pl.BlockSpec((None, CONV_W - 1, B_WIDTH), lambda i: (i // tiles, 0, 0))],
        out_shape=[jax.ShapeDtypeStruct((m, B_WIDTH), F32),
                   jax.ShapeDtypeStruct((bsz, CONV_W - 1, B_WIDTH), F32)],
        scratch_shapes=[pltpu.VMEM((CONV_HALO + tr, B_WIDTH), F32)],
        compiler_params=_params("arbitrary"),
        name="conv_prompt",
    )(h, h, jnp.broadcast_to(conv_w[:, None, :], (CONV_W, 8, B_WIDTH)), conv_b.reshape(1, B_WIDTH),
      ln_g.reshape(1, B_WIDTH), ln_b.reshape(1, B_WIDTH))


def _conv_sample_body(z_ref, st_ref, cw_ref, cb_ref, lg_ref, lb_ref, o_ref, buf_ref, pad_ref, *, bsz):
    hg = _glu(z_ref[...])
    pad_ref[:, 0:CONV_W - 1, :] = st_ref[...]
    pad_ref[:, CONV_W - 1:CONV_W - 1 + DEC_SEQ, :] = hg.reshape(bsz, DEC_SEQ, B_WIDTH)
    acc = jnp.zeros((bsz, DEC_SEQ, B_WIDTH), F32)
    for k in range(CONV_W):
        acc = acc + cw_ref[k:k + 1, :] * pad_ref[:, k:k + DEC_SEQ, :]
    y = _layer_norm(acc + cb_ref[...], lg_ref[...], lb_ref[...])
    o_ref[...] = (y * jax.nn.sigmoid(y)).reshape(bsz * DEC_SEQ, B_WIDTH)
    buf_ref[...] = pad_ref[:, DEC_SEQ:DEC_SEQ + CONV_W - 1, :]


def _conv_sample(h, state, conv_w, conv_b, ln_g, ln_b):
    m = h.shape[0]
    bsz = state.shape[0]
    body = functools.partial(_conv_sample_body, bsz=bsz)
    return pl.pallas_call(
        body,
        grid=(1,),
        in_specs=[pl.BlockSpec((m, 2 * B_WIDTH), lambda i: (0, OFF_B // (2 * B_WIDTH))),
                  pl.BlockSpec((bsz, CONV_W - 1, B_WIDTH), lambda i: (0, 0, 0)),
                  pl.BlockSpec((CONV_W, B_WIDTH), lambda i: (0, 0)),
                  pl.BlockSpec((1, B_WIDTH), lambda i: (0, 0)),
                  pl.BlockSpec((1, B_WIDTH), lambda i: (0, 0)),
                  pl.BlockSpec((1, B_WIDTH), lambda i: (0, 0))],
        out_specs=[pl.BlockSpec((m, B_WIDTH), lambda i: (0, 0)),
                   pl.BlockSpec((bsz, CONV_W - 1, B_WIDTH), lambda i: (0, 0, 0))],
        out_shape=[jax.ShapeDtypeStruct((m, B_WIDTH), F32),
                   jax.ShapeDtypeStruct((bsz, CONV_W - 1, B_WIDTH), F32)],
        scratch_shapes=[pltpu.VMEM((bsz, CONV_W - 1 + DEC_SEQ, B_WIDTH), F32)],
        compiler_params=_params("arbitrary"),
        name="conv_sample",
    )(h, state, conv_w, conv_b.reshape(1, B_WIDTH), ln_g.reshape(1, B_WIDTH), ln_b.reshape(1, B_WIDTH))


def _dot_nt(a, b):
    return lax.dot_general(a, b, (((1,), (1,)), ((), ())), preferred_element_type=F32)


ATTN_TILE = 2048
ATTN_ITEMS = 8


def _attn_prompt_body(slope_ref, *refs):
    group_refs = [refs[5 * g:5 * g + 5] for g in range(C_GROUPS)]
    o_ref, og_ref, lg_ref, sm_ref, sx_ref, pp_ref, pc_ref, mb_ref = refs[5 * C_GROUPS:]
    tq = C_WINDOW
    first_tile = pl.program_id(1) == 0
    hp = pl.program_id(2)
    row = lax.broadcasted_iota(jnp.int32, (tq, tq), 0)
    col = lax.broadcasted_iota(jnp.int32, (tq, tq), 1)
    upper = col > row
    diag = col == row
    diag0 = col == row + jnp.where(first_tile, tq, 0)
    valid0 = col <= row + jnp.where(first_tile, 0, tq)
    dist = jnp.where(upper, row - col + tq, row - col).astype(F32)
    low = lax.broadcasted_iota(jnp.int32, (tq, 128), 1) < C_HEAD_DIM
    sels = (low, jnp.logical_not(low))

    for g, (_, d) in enumerate(C_CONFIGS):
        q_ref, kp_ref, kc_ref, vp_ref, vc_ref = group_refs[g]
        slopes = [slope_ref[g * C_HEADS + 2 * hp + hh] * float(d) for hh in range(2)]
        bias = [-(s * dist) for s in slopes]
        bias_x = [-(s * float(tq)) for s in slopes]
        blocks = [(u, r) for u in range(ATTN_TILE // (tq * d)) for r in range(d)]

        def rows(u, r):
            return pl.ds(u * tq * d + r * tq, tq)

        def out_rows(u, r):
            return pl.ds(u * tq * d + r, tq, stride=d) if d > 1 else pl.ds(u * tq, tq)

        for first in range(0, len(blocks), ATTN_ITEMS):
            items = blocks[first:first + ATTN_ITEMS]
            for n, (u, r) in enumerate(items):
                q2 = q_ref[rows(u, r), :] * ATTN_SCALE
                kc2 = kc_ref[rows(u, r), :]
                kp2 = kp_ref[rows(0, r), :] if u == 0 else kc_ref[rows(u - 1, r), :]
                for hh in range(2):
                    qm = jnp.where(sels[hh], q2, 0.0)
                    sp = _dot_nt(qm, kp2)
                    sm = jnp.where(upper, sp, _dot_nt(qm, kc2)) + bias[hh]
                    if u == 0:
                        sm = jnp.where(valid0, sm, NEG_INF)
                    sm_ref[2 * n + hh] = sm
                    sx_ref[2 * n + hh] = jnp.where(diag0 if u == 0 else diag, sp + bias_x[hh], NEG_INF)
            for n in range(len(items)):
                mx = []
                for hh in range(2):
                    sm = sm_ref[2 * n + hh]
                    sx = sx_ref[2 * n + hh]
                    m = jnp.max(jnp.maximum(sm, sx), axis=-1, keepdims=True)
                    p = jnp.exp(sm - m)
                    pp_ref[2 * n + hh] = jnp.where(upper, p, jnp.exp(sx - m)).astype(BF16)
                    pc_ref[2 * n + hh] = jnp.where(upper, 0.0, p).astype(BF16)
                    mx.append(m)
                mb_ref[n] = jnp.where(low, mx[0], mx[1])
            for n, (u, r) in enumerate(items):
                vc2 = vc_ref[rows(u, r), :]
                vp2 = vp_ref[rows(0, r), :] if u == 0 else vc_ref[rows(u - 1, r), :]
                acc = []
                for hh in range(2):
                    vpx = jnp.where(sels[hh], vp2, 1.0)
                    vcx = jnp.where(sels[hh], vc2, 1.0)
                    acc.append(jnp.dot(pp_ref[2 * n + hh], vpx, preferred_element_type=F32)
                               + jnp.dot(pc_ref[2 * n + hh], vcx, preferred_element_type=F32))
                den = pltpu.roll(jnp.where(low, acc[1], acc[0]), C_HEAD_DIM, axis=1)
                og_ref[g, out_rows(u, r), :] = jnp.where(low, acc[0], acc[1]) / den
                lg_ref[g, out_rows(u, r), :] = mb_ref[n] + jnp.log(den)

    chunk = 256
    for c in range(ATTN_TILE // chunk):
        rs = slice(c * chunk, (c + 1) * chunk)
        l0, l1, l2 = lg_ref[0, rs, :], lg_ref[1, rs, :], lg_ref[2, rs, :]
        mx = jnp.maximum(jnp.maximum(l0, l1), l2)
        e0, e1, e2 = jnp.exp(l0 - mx), jnp.exp(l1 - mx), jnp.exp(l2 - mx)
        tot = e0 + e1 + e2
        o_ref[rs, :] = (e0 / tot) * og_ref[0, rs, :] + (e1 / tot) * og_ref[1, rs, :] + (e2 / tot) * og_ref[2, rs, :]


def _attn_prompt(h, bsz):
    m = h.shape[0]
    t = m // bsz
    assert t % ATTN_TILE == 0
    tiles = t // ATTN_TILE
    in_specs = [pl.BlockSpec(memory_space=pltpu.SMEM)]
    for g, (_, d) in enumerate(C_CONFIGS):
        unit_rows = C_WINDOW * d
        units = ATTN_TILE // unit_rows
        cq, ck, cv = ((off + g * C_GROUP_WIDTH) // 128 for off in (OFF_CQ, OFF_CK, OFF_CV))

        def cur(c):
            return pl.BlockSpec((ATTN_TILE, 128), lambda b, i, hp, c=c: (b * tiles + i, c + hp))

        def prev(c, units=units, unit_rows=unit_rows):
            return pl.BlockSpec((unit_rows, 128),
                                lambda b, i, hp, c=c: (jnp.maximum((b * tiles + i) * units - 1, 0), c + hp))

        in_specs += [cur(cq), prev(ck), cur(ck), prev(cv), cur(cv)]
    slopes = jnp.array([_slope(gg, hh) for gg in range(C_GROUPS) for hh in range(C_HEADS)], F32)
    n_items = 2 * ATTN_ITEMS
    return pl.pallas_call(
        _attn_prompt_body,
        grid=(bsz, tiles, C_HEADS // 2),
        in_specs=in_specs,
        out_specs=pl.BlockSpec((ATTN_TILE, 128), lambda b, i, hp: (b * tiles + i, hp)),
        out_shape=jax.ShapeDtypeStruct((m, C_GROUP_WIDTH), F32),
        scratch_shapes=[pltpu.VMEM((C_GROUPS, ATTN_TILE, 128), F32)] * 2
        + [pltpu.VMEM((n_items, C_WINDOW, C_WINDOW), F32)] * 2
        + [pltpu.VMEM((n_items, C_WINDOW, C_WINDOW), BF16)] * 2
        + [pltpu.VMEM((n_items // 2, C_WINDOW, 128), F32)],
        compiler_params=_params("arbitrary", "arbitrary", "arbitrary"),
        name="attn_prompt",
    )(slopes, *([h] * (5 * C_GROUPS)))


def _attn_sample_body(qkv_ref, slope_ref, k0_ref, v0_ref, k1_ref, v1_ref, k2_ref, v2_ref, o_ref):
    caches = ((k0_ref, v0_ref), (k1_ref, v1_ref), (k2_ref, v2_ref))
    rows = C_HEADS * DEC_SEQ
    rid = lax.broadcasted_iota(jnp.int32, (rows, 1), 0)
    i_row = rid & (DEC_SEQ - 1)
    own_lanes = (lax.broadcasted_iota(jnp.int32, (rows, C_GROUP_WIDTH), 1) >> 6) == (
        lax.broadcasted_iota(jnp.int32, (rows, C_GROUP_WIDTH), 0) >> 3)
    outs, lses = [], []
    for g, (w, d) in enumerate(C_CONFIGS):
        kt_ref, vt_ref = caches[g]
        q = qkv_ref[:, g * 512:(g + 1) * 512] * ATTN_SCALE
        kn = qkv_ref[:, (3 + g) * 512:(4 + g) * 512]
        vn = qkv_ref[:, (6 + g) * 512:(7 + g) * 512]
        qbd = jnp.where(own_lanes, jnp.concatenate([q] * C_HEADS, axis=0), 0.0)
        slope = slope_ref[g, :, 0:1]
        kt = kt_ref[...].reshape(C_GROUP_WIDTH, w).astype(BF16)
        vt = vt_ref[...].reshape(C_GROUP_WIDTH, w).astype(BF16)
        pos = lax.broadcasted_iota(jnp.int32, (rows, w), 1)
        back = (w + i_row) - pos
        ok = (pos >= i_row) if d == 1 else ((back & (d - 1)) == 0) & (pos >= i_row)
        s_c = jnp.dot(qbd.astype(BF16), kt, preferred_element_type=F32)
        s_c = jnp.where(ok, s_c - slope * back.astype(F32), NEG_INF)
        s_n = []
        for i2 in range(DEC_SEQ):
            s = jnp.sum(qbd * kn[i2:i2 + 1, :], axis=-1, keepdims=True)
            back_n = i_row - i2
            ok_n = (back_n >= 0) if d == 1 else ((back_n & (d - 1)) == 0) & (back_n >= 0)
            s_n.append(jnp.where(ok_n, s - slope * back_n.astype(F32), NEG_INF))
        mx = jnp.max(s_c, axis=-1, keepdims=True)
        for s in s_n:
            mx = jnp.maximum(mx, s)
        p_c = jnp.exp(s_c - mx)
        den = jnp.sum(p_c, axis=-1, keepdims=True)
        acc = _dot_nt(p_c.astype(BF16), vt)
        for i2, s in enumerate(s_n):
            p = jnp.exp(s - mx)
            den = den + p
            acc = acc + p * vn[i2:i2 + 1, :]
        outs.append(acc / den)
        lses.append(mx + jnp.log(den))
    mx = jnp.maximum(jnp.maximum(lses[0], lses[1]), lses[2])
    e = [jnp.exp(l - mx) for l in lses]
    tot = e[0] + e[1] + e[2]
    merged = (e[0] / tot) * outs[0] + (e[1] / tot) * outs[1] + (e[2] / tot) * outs[2]
    merged = jnp.where(own_lanes, merged, 0.0).reshape(C_HEADS, DEC_SEQ, C_GROUP_WIDTH)
    o_ref[...] = jnp.sum(merged, axis=0)


def _attn_sample(h, caches, layer):
    m = h.shape[0]
    bsz = m // DEC_SEQ
    qkv = h[:, OFF_CQ:D_IN]
    slopes = jnp.array([[_slope(g, hh) for hh in range(C_HEADS)] for g in range(C_GROUPS)], F32)
    slopes = jnp.broadcast_to(jnp.repeat(slopes, DEC_SEQ, axis=1)[:, :, None], (C_GROUPS, C_HEADS * DEC_SEQ, 128))
    in_specs = [pl.BlockSpec((DEC_SEQ, D_IN - OFF_CQ), lambda b: (b, 0)),
                pl.BlockSpec((C_GROUPS, C_HEADS * DEC_SEQ, 128), lambda b: (0, 0, 0))]
    views = []
    for g, (w, _) in enumerate(C_CONFIGS):
        for j in range(2):
            c = caches[2 * g + j]
            assert c.shape[2] == w, "the cache must hold the last W_g positions"
            views.append(jnp.transpose(c, (0, 1, 3, 4, 2)))
            in_specs.append(pl.BlockSpec((None, None, C_HEADS, C_HEAD_DIM, w), lambda b: (layer, b, 0, 0, 0)))
    return pl.pallas_call(
        _attn_sample_body,
        grid=(bsz,),
        in_specs=in_specs,
        out_specs=pl.BlockSpec((DEC_SEQ, C_GROUP_WIDTH), lambda b: (b, 0)),
        out_shape=jax.ShapeDtypeStruct((m, C_GROUP_WIDTH), F32),
        compiler_params=_params("arbitrary"),
        name="attn_sample",
    )(qkv, slopes, *views)


def _merge_out_body(x_ref, oa_ref, ob_ref, oc_ref, gate_ref, wb_ref, wo_ref, g_ref, out_ref):
    merged = None
    for n, o_ref in enumerate((oa_ref, ob_ref, oc_ref)):
        proj = jnp.dot(o_ref[...].astype(BF16), wb_ref[n * BRANCH_WIDTH:(n + 1) * BRANCH_WIDTH, :],
                       preferred_element_type=F32)
        term = jax.nn.sigmoid(gate_ref[:, n * D_MODEL:(n + 1) * D_MODEL].astype(F32)) * proj
        merged = term if merged is None else merged + term
    y = jnp.dot(merged.astype(BF16), wo_ref[...], preferred_element_type=F32)
    out_ref[...] = x_ref[...] + _rms(y, g_ref[...])


def _merge_out(x, h, o_a, o_b, o_c, w_branch_bf16, w_out_bf16, gain):
    m = x.shape[0]
    tm = min(m, 512)
    row512 = pl.BlockSpec((tm, BRANCH_WIDTH), lambda i: (i, 0))
    return pl.pallas_call(
        _merge_out_body,
        grid=(m // tm,),
        in_specs=[pl.BlockSpec((tm, D_MODEL), lambda i: (i, 0)), row512, row512, row512,
                  pl.BlockSpec((tm, N_BRANCH * D_MODEL), lambda i: (i, OFF_G // (N_BRANCH * D_MODEL))),
                  pl.BlockSpec((N_BRANCH * BRANCH_WIDTH, D_MODEL), lambda i: (0, 0)),
                  pl.BlockSpec((D_MODEL, D_MODEL), lambda i: (0, 0)),
                  pl.BlockSpec((1, D_MODEL), lambda i: (0, 0))],
        out_specs=pl.BlockSpec((tm, D_MODEL), lambda i: (i, 0)),
        out_shape=jax.ShapeDtypeStruct((m, D_MODEL), F32),
        compiler_params=_params("arbitrary"),
        name="merge_out",
    )(x, o_a, o_b, o_c, h, w_branch_bf16, w_out_bf16, gain.reshape(1, D_MODEL))


FF_CHUNK = D_FF // 2


def _ffn_body(x_ref, g1_ref, w1_ref, w2_ref, g2_ref, out_ref):
    x = x_ref[...]
    xn = _rms(x, g1_ref[...]).astype(BF16)
    y = None
    for c in range(D_FF // FF_CHUNK):
        gate = jnp.dot(xn, w1_ref[:, c * FF_CHUNK:(c + 1) * FF_CHUNK], preferred_element_type=F32)
        up = jnp.dot(xn, w1_ref[:, D_FF + c * FF_CHUNK:D_FF + (c + 1) * FF_CHUNK], preferred_element_type=F32)
        act = (gate * jax.nn.sigmoid(gate) * up).astype(BF16)
        part = jnp.dot(act, w2_ref[c * FF_CHUNK:(c + 1) * FF_CHUNK, :], preferred_element_type=F32)
        y = part if y is None else y + part
    out_ref[...] = x + _rms(y, g2_ref[...])


def _ffn(x, gain_pre, w1_bf16, w2_bf16, gain_post):
    m = x.shape[0]
    tm = min(m, 512)
    resident = dict(pipeline_mode=pl.Buffered(1))
    return pl.pallas_call(
        _ffn_body,
        grid=(m // tm,),
        in_specs=[pl.BlockSpec((tm, D_MODEL), lambda i: (i, 0)),
                  pl.BlockSpec((1, D_MODEL), lambda i: (0, 0)),
                  pl.BlockSpec((D_MODEL, 2 * D_FF), lambda i: (0, 0), **resident),
                  pl.BlockSpec((D_FF, D_MODEL), lambda i: (0, 0), **resident),
                  pl.BlockSpec((1, D_MODEL), lambda i: (0, 0))],
        out_specs=pl.BlockSpec((tm, D_MODEL), lambda i: (i, 0)),
        out_shape=jax.ShapeDtypeStruct((m, D_MODEL), F32),
        compiler_params=_params("arbitrary"),
        name="ffn",
    )(x, gain_pre.reshape(1, D_MODEL), w1_bf16, w2_bf16, gain_post.reshape(1, D_MODEL))


def _split_heads(cols):
    return cols.reshape(cols.shape[0], cols.shape[1], C_HEADS, C_HEAD_DIM)


def _prompt_trunk(x3, p):
    bsz, t, _ = x3.shape
    x = x3.reshape(bsz * t, D_MODEL)
    depth = p["w_in"].shape[0]
    bufs, kvs = [], []
    for l in range(depth):
        h, kv_tail = _in_proj(x, p["norm_pre_mix"][l], p["w_in"], l, bsz)
        (o_a,) = _token_mix(h, p["a_norm_g"][l], p["a_norm_b"][l], p["a_w_s"][l], p["a_b_s"][l], CHUNK, False)
        o_b, buf = _conv_prompt(h, bsz, p["b_conv_w"][l], p["b_conv_b"][l], p["b_norm_g"][l], p["b_norm_b"][l])
        o_c = _attn_prompt(h, bsz)
        x = _merge_out(x, h, o_a, o_b, o_c, p["w_branch"][l], p["w_out"][l], p["norm_post_mix"][l])
        x = _ffn(x, p["norm_pre_ffn"][l], p["ffn_w_in"][l], p["ffn_w_out"][l], p["norm_post_ffn"][l])
        bufs.append(buf)
        kv3 = kv_tail.reshape(bsz, TAIL, KV_WIDTH)
        layer_kv = []
        for g, (w, _) in enumerate(C_CONFIGS):
            keep = min(w, t)
            for off in (g * C_GROUP_WIDTH, C_GROUPS * C_GROUP_WIDTH + g * C_GROUP_WIDTH):
                layer_kv.append(_split_heads(kv3[:, TAIL - keep:, off:off + C_GROUP_WIDTH]))
        kvs.append(layer_kv)
    new_kv = [jnp.stack([kvs[l][j] for l in range(depth)], axis=0) for j in range(2 * C_GROUPS)]
    return x.reshape(bsz, t, D_MODEL), jnp.stack(bufs, axis=0), new_kv


def _sample_trunk(x3, conv_state, caches, p):
    bsz, t, _ = x3.shape
    x = x3.reshape(bsz * t, D_MODEL)
    depth = p["w_in"].shape[0]
    bufs, vns, kvs = [], [], []
    for l in range(depth):
        h = _in_proj_sample(x, p["norm_pre_mix"][l], p["w_in"], l)
        o_a, vn = _token_mix(h, p["a_norm_g"][l], p["a_norm_b"][l], p["a_w_s"][l], p["a_b_s"][l], t, True)
        o_b, buf = _conv_sample(h, conv_state[l], p["b_conv_w"][l], p["b_conv_b"][l],
                                p["b_norm_g"][l], p["b_norm_b"][l])
        o_c = _attn_sample(h, caches, l)
        x = _merge_out(x, h, o_a, o_b, o_c, p["w_branch"][l], p["w_out"][l], p["norm_post_mix"][l])
        x = _ffn(x, p["norm_pre_ffn"][l], p["ffn_w_in"][l], p["ffn_w_out"][l], p["norm_post_ffn"][l])
        bufs.append(buf)
        vns.append(vn.reshape(bsz, t, A_WIDTH))
        h3 = h.reshape(bsz, t, D_IN)
        kvs.append([_split_heads(h3[:, :, off + g * C_GROUP_WIDTH:off + (g + 1) * C_GROUP_WIDTH])
                    for g in range(C_GROUPS) for off in (OFF_CK, OFF_CV)])
    new_kv = [jnp.stack([kvs[l][j] for l in range(depth)], axis=0) for j in range(2 * C_GROUPS)]
    return x.reshape(bsz, t, D_MODEL), jnp.stack(bufs, axis=0), jnp.stack(vns, axis=0), new_kv


def kernel(x_prompt, x_sample, state_b_conv, cache_c0_k, cache_c0_v, cache_c1_k, cache_c1_v, cache_c2_k,
           cache_c2_v, norm_pre_mix, norm_post_mix, norm_pre_ffn, norm_post_ffn, w_in, a_norm_g, a_norm_b,
           a_w_s, a_b_s, b_conv_w, b_conv_b, b_norm_g, b_norm_b, w_branch, w_out, ffn_w_in, ffn_w_out):
    assert x_sample.shape[1] == DEC_SEQ
    p = dict(norm_pre_mix=norm_pre_mix, norm_post_mix=norm_post_mix, norm_pre_ffn=norm_pre_ffn,
             norm_post_ffn=norm_post_ffn, w_in=w_in, a_norm_g=a_norm_g, a_norm_b=a_norm_b,
             a_w_s=a_w_s, a_b_s=a_b_s, b_conv_w=b_conv_w, b_conv_b=b_conv_b, b_norm_g=b_norm_g,
             b_norm_b=b_norm_b, w_branch=w_branch.astype(BF16), w_out=w_out.astype(BF16),
             ffn_w_in=ffn_w_in.astype(BF16), ffn_w_out=ffn_w_out.astype(BF16))
    caches = (cache_c0_k, cache_c0_v, cache_c1_k, cache_c1_v, cache_c2_k, cache_c2_v)
    y_prompt, buf_p, kv_p = _prompt_trunk(x_prompt, p)
    y_sample, buf_s, vn_s, kv_s = _sample_trunk(x_sample, state_b_conv, caches, p)
    return (y_prompt, y_sample, buf_p, buf_s, vn_s, *kv_p, *kv_s)
```

```python
import functools

import jax
import jax.numpy as jnp
from jax import lax
from jax.experimental import pallas as pl
from jax.experimental.pallas import tpu as pltpu

F32 = jnp.float32
BF16 = jnp.bfloat16

D_MODEL = 1024
DEC_SEQ = 8
A_WIDTH = 512
A_GROUPS = 4
A_GROUP_DIM = 128
CHUNK = 128
B_WIDTH = 512
CONV_W = 31
CONV_HALO = 32
C_HEADS = 8
C_HEAD_DIM = 64
C_GROUP_WIDTH = 512
C_CONFIGS = ((128, 1), (512, 4), (2048, 16))
C_GROUPS = 3
C_WINDOW = 128
N_BRANCH = 3
BRANCH_WIDTH = 512
D_FF = 2816
EPS = 1e-6
NEG_INF = -1e30
ATTN_SCALE = C_HEAD_DIM ** -0.5
REF_OFF_G = 6656
OFF_G = 0
OFF_AU = 3072
OFF_AV = 3584
OFF_B = 4096
OFF_CQ = 5120
OFF_CK = 6656
OFF_CV = 8192
D_IN = 9728
COL_BLOCKS = D_IN // 512

VMEM_LIMIT = 56 * 1024 * 1024


def _slope(g, h):
    n = C_GROUPS * C_HEADS
    return 2.0 ** (-8.0 * (h * C_GROUPS + g + 1.0) / n)


def _params(*sem):
    return pltpu.CompilerParams(dimension_semantics=sem, vmem_limit_bytes=VMEM_LIMIT)


def _rms(x, gain):
    return x * lax.rsqrt(jnp.mean(x * x, axis=-1, keepdims=True) + EPS) * gain


def _layer_norm(x, gain, bias):
    mu = jnp.mean(x, axis=-1, keepdims=True)
    xc = x - mu
    var = jnp.mean(xc * xc, axis=-1, keepdims=True)
    return xc * lax.rsqrt(var + EPS) * gain + bias


Q_TILE0 = OFF_CQ // 512
KV_TILE0 = OFF_CK // 512
KV_WIDTH = D_IN - OFF_CK
TAIL = 2048


def _in_proj_sample_body(x_ref, g_ref, w_ref, o_ref, xn_ref):
    @pl.when(pl.program_id(1) == 0)
    def _():
        xn_ref[...] = _rms(x_ref[...], g_ref[...]).astype(BF16)

    o_ref[...] = jnp.dot(xn_ref[...], w_ref[...].astype(BF16), preferred_element_type=F32)


def _w_in_tile(j):
    return lax.rem(j + REF_OFF_G // 512, COL_BLOCKS)


def _in_proj_sample(x, gain, w_in, layer):
    m = x.shape[0]
    tn = 512
    return pl.pallas_call(
        _in_proj_sample_body,
        grid=(1, D_IN // tn),
        in_specs=[pl.BlockSpec((m, D_MODEL), lambda i, j: (0, 0)),
                  pl.BlockSpec((1, D_MODEL), lambda i, j: (0, 0)),
                  pl.BlockSpec((None, D_MODEL, tn), lambda i, j: (layer, 0, _w_in_tile(j)))],
        out_specs=pl.BlockSpec((m, tn), lambda i, j: (0, j)),
        out_shape=jax.ShapeDtypeStruct((m, D_IN), F32),
        scratch_shapes=[pltpu.VMEM((m, D_MODEL), BF16)],
        compiler_params=_params("arbitrary", "arbitrary"),
        name="in_proj_sample",
    )(x, gain.reshape(1, D_MODEL), w_in)


def _in_proj_body(x_ref, g_ref, w_ref, o_ref, kv_ref, xn_ref, res_ref, tmp_ref, *, tm, tiles_per_seq):
    i = pl.program_id(0)
    j = pl.program_id(1)
    lanes = res_ref.shape[1]

    def matmul_tile(slot):
        res = jnp.dot(xn_ref[...], w_ref[...].astype(BF16), preferred_element_type=F32)
        for c in range(lanes):
            res_ref[slot, c] = res[:, c * 128:(c + 1) * 128]

    def emit(slot, d):
        if d == 1:
            for c in range(lanes):
                o_ref[:, c * 128:(c + 1) * 128] = res_ref[slot, c].astype(BF16)
            return
        if d == 16:
            quarter = tm // 4
            for b4 in range(4):
                for c in range(lanes):
                    tmp_ref[c, b4 * quarter:(b4 + 1) * quarter, :] = res_ref[slot, c, pl.ds(b4, quarter, stride=4), :]
            for r in range(d):
                for c in range(lanes):
                    o_ref[r * C_WINDOW:(r + 1) * C_WINDOW, c * 128:(c + 1) * 128] = (
                        tmp_ref[c, pl.ds((r % 4) * quarter + r // 4, C_WINDOW, stride=4), :].astype(BF16))
            return
        for u in range(tm // (C_WINDOW * d)):
            for r in range(d):
                dst = u * C_WINDOW * d + r * C_WINDOW
                for c in range(lanes):
                    o_ref[dst:dst + C_WINDOW, c * 128:(c + 1) * 128] = (
                        res_ref[slot, c, pl.ds(u * C_WINDOW * d + r, C_WINDOW, stride=d), :].astype(BF16))

    @pl.when(j == 0)
    def _():
        xn_ref[...] = _rms(x_ref[...], g_ref[...]).astype(BF16)
        matmul_tile(0)

    prev_group = lax.rem(jnp.maximum(j - 1 - Q_TILE0, 0), C_GROUPS)
    prev_dilation_group = jnp.where(j - 1 >= Q_TILE0, prev_group, 0)
    in_range = jnp.logical_and(j >= 1, j < COL_BLOCKS)
    for parity in range(2):
        for g, (_, d) in enumerate(C_CONFIGS):
            @pl.when(jnp.logical_and(jnp.logical_and(in_range, lax.rem(j, 2) == parity), prev_dilation_group == g))
            def _():
                emit(1 - parity, d)
                matmul_tile(parity)

    @pl.when(j == COL_BLOCKS)
    def _():
        emit((COL_BLOCKS - 1) % 2, C_CONFIGS[(COL_BLOCKS - 1 - Q_TILE0) % C_GROUPS][1])

    @pl.when(jnp.logical_and(i % tiles_per_seq == tiles_per_seq - 1,
                             jnp.logical_and(j >= KV_TILE0, j < COL_BLOCKS)))
    def _():
        for c in range(lanes):
            kv_ref[:, c * 128:(c + 1) * 128] = res_ref[lax.rem(j, 2), c]


def _in_proj(x, gain, w_in, layer, bsz):
    m = x.shape[0]
    t = m // bsz
    tm = TAIL
    tn = 512
    assert t % tm == 0
    tiles_per_seq = t // tm
    last = COL_BLOCKS - 1

    def kv_index(i, j):
        is_tail = i % tiles_per_seq == tiles_per_seq - 1
        return i // tiles_per_seq, jnp.where(is_tail, jnp.clip(j - KV_TILE0, 0, last - KV_TILE0), 0)

    return pl.pallas_call(
        functools.partial(_in_proj_body, tm=tm, tiles_per_seq=tiles_per_seq),
        grid=(m // tm, COL_BLOCKS + 1),
        in_specs=[pl.BlockSpec((tm, D_MODEL), lambda i, j: (i, 0)),
                  pl.BlockSpec((1, D_MODEL), lambda i, j: (0, 0)),
                  pl.BlockSpec((None, D_MODEL, tn), lambda i, j: (layer, 0, _w_in_tile(jnp.minimum(j, last))))],
        out_specs=[pl.BlockSpec((tm, tn), lambda i, j: (i, jnp.maximum(j - 1, 0))),
                   pl.BlockSpec((tm, tn), kv_index)],
        out_shape=[jax.ShapeDtypeStruct((m, D_IN), BF16),
                   jax.ShapeDtypeStruct((bsz * TAIL, KV_WIDTH), F32)],
        scratch_shapes=[pltpu.VMEM((tm, D_MODEL), BF16), pltpu.VMEM((2, tn // 128, tm, 128), F32),
                        pltpu.VMEM((tn // 128, tm, 128), F32)],
        compiler_params=_params("arbitrary", "arbitrary"),
        name="in_proj",
    )(x, gain.reshape(1, D_MODEL), w_in)


def _token_mix_body(u_ref, v_ref, ng_ref, nb_ref, w_ref, bias_ref, o_ref, *maybe_vn_ref, r, chunk_len, n_sub):
    row = lax.broadcasted_iota(jnp.int32, (r, r), 0)
    col = lax.broadcasted_iota(jnp.int32, (r, r), 1)
    shift = chunk_len.bit_length() - 1
    keep = ((row >> shift) == (col >> shift)) & (col <= row)
    wm = [jnp.where(keep, w_ref[g], 0.0).astype(BF16) for g in range(A_GROUPS)]
    for c in range(n_sub):
        rows = slice(c * r, (c + 1) * r)
        vn = _layer_norm(v_ref[rows, :].astype(F32), ng_ref[...], nb_ref[...])
        for vn_ref in maybe_vn_ref:
            vn_ref[rows, :] = vn
        vnb = vn.astype(BF16)
        for g in range(A_GROUPS):
            lanes = slice(g * A_GROUP_DIM, (g + 1) * A_GROUP_DIM)
            mix = jnp.dot(wm[g], vnb[:, lanes], preferred_element_type=F32) + bias_ref[:, lanes]
            o_ref[rows, lanes] = u_ref[rows, lanes].astype(F32) * mix


def _token_mix(h, norm_g, norm_b, w_s, b_s, chunk_len, emit_vn):
    m = h.shape[0]
    if chunk_len == CHUNK:
        r, n_sub = CHUNK, min(4, m // CHUNK)
        w = w_s
        bias_rows = b_s.T
    else:
        r, n_sub = m, 1
        reps = m // chunk_len
        w = jnp.tile(w_s[:, :chunk_len, :chunk_len], (1, reps, reps))
        bias_rows = jnp.tile(b_s[:, :chunk_len].T, (reps, 1))
    bias = jnp.repeat(bias_rows, A_GROUP_DIM, axis=1)
    tr = r * n_sub
    body = functools.partial(_token_mix_body, r=r, chunk_len=chunk_len, n_sub=n_sub)
    return pl.pallas_call(
        body,
        grid=(m // tr,),
        in_specs=[pl.BlockSpec((tr, A_WIDTH), lambda i: (i, OFF_AU // A_WIDTH)),
                  pl.BlockSpec((tr, A_WIDTH), lambda i: (i, OFF_AV // A_WIDTH)),
                  pl.BlockSpec((1, A_WIDTH), lambda i: (0, 0)),
                  pl.BlockSpec((1, A_WIDTH), lambda i: (0, 0)),
                  pl.BlockSpec((A_GROUPS, r, r), lambda i: (0, 0, 0)),
                  pl.BlockSpec((r, A_WIDTH), lambda i: (0, 0))],
        out_specs=[pl.BlockSpec((tr, A_WIDTH), lambda i: (i, 0))] * (2 if emit_vn else 1),
        out_shape=[jax.ShapeDtypeStruct((m, A_WIDTH), F32)] * (2 if emit_vn else 1),
        compiler_params=_params("arbitrary"),
        name="token_mix",
    )(h, h, norm_g.reshape(1, A_WIDTH), norm_b.reshape(1, A_WIDTH), w, bias)


def _glu(z):
    return z[:, :B_WIDTH] * jax.nn.sigmoid(z[:, B_WIDTH:])


def _conv_prompt_body(z_ref, zh_ref, cw_ref, cb_ref, lg_ref, lb_ref, o_ref, buf_ref, pad_ref, *, tr, tiles):
    t = pl.program_id(0) % tiles
    hg = _glu(z_ref[...].astype(F32))
    pad_ref[0:CONV_HALO, :] = jnp.where(t == 0, 0.0, _glu(zh_ref[...].astype(F32)))
    pad_ref[CONV_HALO:CONV_HALO + tr, :] = hg
    first = CONV_HALO - (CONV_W - 1)
    sub = 32
    taps = [[k for k in range(CONV_W) if (first + k) % 8 == s] for s in range(8)]
    for blk in range(tr // sub):
        acc = jnp.zeros((sub // 8, 8, B_WIDTH), F32)
        for s in range(8):
            q_max = max((first + k) // 8 for k in taps[s])
            win = pad_ref[blk * sub + s:blk * sub + s + sub + 8 * q_max, :]
            for k in taps[s]:
                q = (first + k) // 8
                acc = acc + win[8 * q:8 * q + sub, :].reshape(sub // 8, 8, B_WIDTH) * cw_ref[k]
        y = _layer_norm(acc.reshape(sub, B_WIDTH) + cb_ref[...], lg_ref[...], lb_ref[...])
        o_ref[blk * sub:(blk + 1) * sub, :] = y * jax.nn.sigmoid(y)

    @pl.when(t == tiles - 1)
    def _():
        buf_ref[...] = pad_ref[CONV_HALO + tr - (CONV_W - 1):CONV_HALO + tr, :]


def _conv_prompt(h, bsz, conv_w, conv_b, ln_g, ln_b):
    m = h.shape[0]
    t = m // bsz
    tr = min(256, t)
    tiles = t // tr
    halo_per_tile = tr // CONV_HALO
    body = functools.partial(_conv_prompt_body, tr=tr, tiles=tiles)
    return pl.pallas_call(
        body,
        grid=(m // tr,),
        in_specs=[pl.BlockSpec((tr, 2 * B_WIDTH), lambda i: (i, OFF_B // (2 * B_WIDTH))),
                  pl.BlockSpec((CONV_HALO, 2 * B_WIDTH),
                               lambda i: (jnp.maximum(i * halo_per_tile - 1, 0), OFF_B // (2 * B_WIDTH))),
                  pl.BlockSpec((CONV_W, 8, B_WIDTH), lambda i: (0, 0, 0)),
                  pl.BlockSpec((1, B_WIDTH), lambda i: (0, 0)),
                  pl.BlockSpec((1, B_WIDTH), lambda i: (0, 0)),
                  pl.BlockSpec((1, B_WIDTH), lambda i: (0, 0))],
        out_specs=[pl.BlockSpec((tr, B_WIDTH), lambda i: (i, 0)),
                   pl.BlockSpec((None, CONV_W - 1, B_WIDTH), lambda i: (i // tiles, 0, 0))],
        out_shape=[jax.ShapeDtypeStruct((m, B_WIDTH), F32),
                   jax.ShapeDtypeStruct((bsz, CONV_W - 1, B_WIDTH), F32)],
        scratch_shapes=[pltpu.VMEM((CONV_HALO + tr, B_WIDTH), F32)],
        compiler_params=_params("arbitrary"),
        name="conv_prompt",
    )(h, h, jnp.broadcast_to(conv_w[:, None, :], (CONV_W, 8, B_WIDTH)), conv_b.reshape(1, B_WIDTH),
      ln_g.reshape(1, B_WIDTH), ln_b.reshape(1, B_WIDTH))


def _conv_sample_body(z_ref, st_ref, cw_ref, cb_ref, lg_ref, lb_ref, o_ref, buf_ref, pad_ref, *, bsz):
    hg = _glu(z_ref[...])
    pad_ref[:, 0:CONV_W - 1, :] = st_ref[...]
    pad_ref[:, CONV_W - 1:CONV_W - 1 + DEC_SEQ, :] = hg.reshape(bsz, DEC_SEQ, B_WIDTH)
    acc = jnp.zeros((bsz, DEC_SEQ, B_WIDTH), F32)
    for k in range(CONV_W):
        acc = acc + cw_ref[k:k + 1, :] * pad_ref[:, k:k + DEC_SEQ, :]
    y = _layer_norm(acc + cb_ref[...], lg_ref[...], lb_ref[...])
    o_ref[...] = (y * jax.nn.sigmoid(y)).reshape(bsz * DEC_SEQ, B_WIDTH)
    buf_ref[...] = pad_ref[:, DEC_SEQ:DEC_SEQ + CONV_W - 1, :]


def _conv_sample(h, state, conv_w, conv_b, ln_g, ln_b):
    m = h.shape[0]
    bsz = state.shape[0]
    body = functools.partial(_conv_sample_body, bsz=bsz)
    return pl.pallas_call(
        body,
        grid=(1,),
        in_specs=[pl.BlockSpec((m, 2 * B_WIDTH), lambda i: (0, OFF_B // (2 * B_WIDTH))),
                  pl.BlockSpec((bsz, CONV_W - 1, B_WIDTH), lambda i: (0, 0, 0)),
                  pl.BlockSpec((CONV_W, B_WIDTH), lambda i: (0, 0)),
                  pl.BlockSpec((1, B_WIDTH), lambda i: (0, 0)),
                  pl.BlockSpec((1, B_WIDTH), lambda i: (0, 0)),
                  pl.BlockSpec((1, B_WIDTH), lambda i: (0, 0))],
        out_specs=[pl.BlockSpec((m, B_WIDTH), lambda i: (0, 0)),
                   pl.BlockSpec((bsz, CONV_W - 1, B_WIDTH), lambda i: (0, 0, 0))],
        out_shape=[jax.ShapeDtypeStruct((m, B_WIDTH), F32),
                   jax.ShapeDtypeStruct((bsz, CONV_W - 1, B_WIDTH), F32)],
        scratch_shapes=[pltpu.VMEM((bsz, CONV_W - 1 + DEC_SEQ, B_WIDTH), F32)],
        compiler_params=_params("arbitrary"),
        name="conv_sample",
    )(h, state, conv_w, conv_b.reshape(1, B_WIDTH), ln_g.reshape(1, B_WIDTH), ln_b.reshape(1, B_WIDTH))


def _dot_nt(a, b):
    return lax.dot_general(a, b, (((1,), (1,)), ((), ())), preferred_element_type=F32)


ATTN_TILE = 2048
ATTN_ITEMS = 8


def _attn_prompt_body(slope_ref, *refs):
    group_refs = [refs[5 * g:5 * g + 5] for g in range(C_GROUPS)]
    o_ref, og_ref, lg_ref, sm_ref, sx_ref, pp_ref, pc_ref, mb_ref = refs[5 * C_GROUPS:]
    tq = C_WINDOW
    first_tile = pl.program_id(1) == 0
    hp = pl.program_id(2)
    row = lax.broadcasted_iota(jnp.int32, (tq, tq), 0)
    col = lax.broadcasted_iota(jnp.int32, (tq, tq), 1)
    upper = col > row
    diag = col == row
    diag0 = col == row + jnp.where(first_tile, tq, 0)
    valid0 = col <= row + jnp.where(first_tile, 0, tq)
    dist = jnp.where(upper, row - col + tq, row - col).astype(F32)
    low = lax.broadcasted_iota(jnp.int32, (tq, 128), 1) < C_HEAD_DIM
    sels = (low, jnp.logical_not(low))

    for g, (_, d) in enumerate(C_CONFIGS):
        q_ref, kp_ref, kc_ref, vp_ref, vc_ref = group_refs[g]
        slopes = [slope_ref[g * C_HEADS + 2 * hp + hh] * float(d) for hh in range(2)]
        bias = [-(s * dist) for s in slopes]
        bias_x = [-(s * float(tq)) for s in slopes]
        blocks = [(u, r) for u in range(ATTN_TILE // (tq * d)) for r in range(d)]

        def rows(u, r):
            return pl.ds(u * tq * d + r * tq, tq)

        def out_rows(u, r):
            return pl.ds(u * tq * d + r, tq, stride=d) if d > 1 else pl.ds(u * tq, tq)

        for first in range(0, len(blocks), ATTN_ITEMS):
            items = blocks[first:first + ATTN_ITEMS]
            for n, (u, r) in enumerate(items):
                q2 = q_ref[rows(u, r), :] * ATTN_SCALE
                kc2 = kc_ref[rows(u, r), :]
                kp2 = kp_ref[rows(0, r), :] if u == 0 else kc_ref[rows(u - 1, r), :]
                for hh in range(2):
                    qm = jnp.where(sels[hh], q2, 0.0)
                    sp = _dot_nt(qm, kp2)
                    sm = jnp.where(upper, sp, _dot_nt(qm, kc2)) + bias[hh]
                    if u == 0:
                        sm = jnp.where(valid0, sm, NEG_INF)
                    sm_ref[2 * n + hh] = sm
                    sx_ref[2 * n + hh] = jnp.where(diag0 if u == 0 else diag, sp + bias_x[hh], NEG_INF)
            for n in range(len(items)):
                mx = []
                for hh in range(2):
                    sm = sm_ref[2 * n + hh]
                    sx = sx_ref[2 * n + hh]
                    m = jnp.max(jnp.maximum(sm, sx), axis=-1, keepdims=True)
                    p = jnp.exp(sm - m)
                    pp_ref[2 * n + hh] = jnp.where(upper, p, jnp.exp(sx - m)).astype(BF16)
                    pc_ref[2 * n + hh] = jnp.where(upper, 0.0, p).astype(BF16)
                    mx.append(m)
                mb_ref[n] = jnp.where(low, mx[0], mx[1])
            for n, (u, r) in enumerate(items):
                vc2 = vc_ref[rows(u, r), :]
                vp2 = vp_ref[rows(0, r), :] if u == 0 else vc_ref[rows(u - 1, r), :]
                acc = []
                for hh in range(2):
                    vpx = jnp.where(sels[hh], vp2, 1.0)
                    vcx = jnp.where(sels[hh], vc2, 1.0)
                    acc.append(jnp.dot(pp_ref[2 * n + hh], vpx, preferred_element_type=F32)
                               + jnp.dot(pc_ref[2 * n + hh], vcx, preferred_element_type=F32))
                den = pltpu.roll(jnp.where(low, acc[1], acc[0]), C_HEAD_DIM, axis=1)
                og_ref[g, out_rows(u, r), :] = jnp.where(low, acc[0], acc[1]) / den
                lg_ref[g, out_rows(u, r), :] = mb_ref[n] + jnp.log(den)

    chunk = 256
    for c in range(ATTN_TILE // chunk):
        rs = slice(c * chunk, (c + 1) * chunk)
        l0, l1, l2 = lg_ref[0, rs, :], lg_ref[1, rs, :], lg_ref[2, rs, :]
        mx = jnp.maximum(jnp.maximum(l0, l1), l2)
        e0, e1, e2 = jnp.exp(l0 - mx), jnp.exp(l1 - mx), jnp.exp(l2 - mx)
        tot = e0 + e1 + e2
        o_ref[rs, :] = (e0 / tot) * og_ref[0, rs, :] + (e1 / tot) * og_ref[1, rs, :] + (e2 / tot) * og_ref[2, rs, :]


def _attn_prompt(h, bsz):
    m = h.shape[0]
    t = m // bsz
    assert t % ATTN_TILE == 0
    tiles = t // ATTN_TILE
    in_specs = [pl.BlockSpec(memory_space=pltpu.SMEM)]
    for g, (_, d) in enumerate(C_CONFIGS):
        unit_rows = C_WINDOW * d
        units = ATTN_TILE // unit_rows
        cq, ck, cv = ((off + g * C_GROUP_WIDTH) // 128 for off in (OFF_CQ, OFF_CK, OFF_CV))

        def cur(c):
            return pl.BlockSpec((ATTN_TILE, 128), lambda b, i, hp, c=c: (b * tiles + i, c + hp))

        def prev(c, units=units, unit_rows=unit_rows):
            return pl.BlockSpec((unit_rows, 128),
                                lambda b, i, hp, c=c: (jnp.maximum((b * tiles + i) * units - 1, 0), c + hp))

        in_specs += [cur(cq), prev(ck), cur(ck), prev(cv), cur(cv)]
    slopes = jnp.array([_slope(gg, hh) for gg in range(C_GROUPS) for hh in range(C_HEADS)], F32)
    n_items = 2 * ATTN_ITEMS
    return pl.pallas_call(
        _attn_prompt_body,
        grid=(bsz, tiles, C_HEADS // 2),
        in_specs=in_specs,
        out_specs=pl.BlockSpec((ATTN_TILE, 128), lambda b, i, hp: (b * tiles + i, hp)),
        out_shape=jax.ShapeDtypeStruct((m, C_GROUP_WIDTH), F32),
        scratch_shapes=[pltpu.VMEM((C_GROUPS, ATTN_TILE, 128), F32)] * 2
        + [pltpu.VMEM((n_items, C_WINDOW, C_WINDOW), F32)] * 2
        + [pltpu.VMEM((n_items, C_WINDOW, C_WINDOW), BF16)] * 2
        + [pltpu.VMEM((n_items // 2, C_WINDOW, 128), F32)],
        compiler_params=_params("arbitrary", "arbitrary", "arbitrary"),
        name="attn_prompt",
    )(slopes, *([h] * (5 * C_GROUPS)))


def _attn_sample_body(qkv_ref, slope_ref, k0_ref, v0_ref, k1_ref, v1_ref, k2_ref, v2_ref, o_ref):
    caches = ((k0_ref, v0_ref), (k1_ref, v1_ref), (k2_ref, v2_ref))
    rows = C_HEADS * DEC_SEQ
    rid = lax.broadcasted_iota(jnp.int32, (rows, 1), 0)
    i_row = rid & (DEC_SEQ - 1)
    own_lanes = (lax.broadcasted_iota(jnp.int32, (rows, C_GROUP_WIDTH), 1) >> 6) == (
        lax.broadcasted_iota(jnp.int32, (rows, C_GROUP_WIDTH), 0) >> 3)
    outs, lses = [], []
    for g, (w, d) in enumerate(C_CONFIGS):
        kt_ref, vt_ref = caches[g]
        q = qkv_ref[:, g * 512:(g + 1) * 512] * ATTN_SCALE
        kn = qkv_ref[:, (3 + g) * 512:(4 + g) * 512]
        vn = qkv_ref[:, (6 + g) * 512:(7 + g) * 512]
        qbd = jnp.where(own_lanes, jnp.concatenate([q] * C_HEADS, axis=0), 0.0)
        slope = slope_ref[g, :, 0:1]
        kt = kt_ref[...].reshape(C_GROUP_WIDTH, w).astype(BF16)
        vt = vt_ref[...].reshape(C_GROUP_WIDTH, w).astype(BF16)
        pos = lax.broadcasted_iota(jnp.int32, (rows, w), 1)
        back = (w + i_row) - pos
        ok = (pos >= i_row) if d == 1 else ((back & (d - 1)) == 0) & (pos >= i_row)
        s_c = jnp.dot(qbd.astype(BF16), kt, preferred_element_type=F32)
        s_c = jnp.where(ok, s_c - slope * back.astype(F32), NEG_INF)
        s_n = []
        for i2 in range(DEC_SEQ):
            s = jnp.sum(qbd * kn[i2:i2 + 1, :], axis=-1, keepdims=True)
            back_n = i_row - i2
            ok_n = (back_n >= 0) if d == 1 else ((back_n & (d - 1)) == 0) & (back_n >= 0)
            s_n.append(jnp.where(ok_n, s - slope * back_n.astype(F32), NEG_INF))
        mx = jnp.max(s_c, axis=-1, keepdims=True)
        for s in s_n:
            mx = jnp.maximum(mx, s)
        p_c = jnp.exp(s_c - mx)
        den = jnp.sum(p_c, axis=-1, keepdims=True)
        acc = _dot_nt(p_c.astype(BF16), vt)
        for i2, s in enumerate(s_n):
            p = jnp.exp(s - mx)
            den = den + p
            acc = acc + p * vn[i2:i2 + 1, :]
        outs.append(acc / den)
        lses.append(mx + jnp.log(den))
    mx = jnp.maximum(jnp.maximum(lses[0], lses[1]), lses[2])
    e = [jnp.exp(l - mx) for l in lses]
    tot = e[0] + e[1] + e[2]
    merged = (e[0] / tot) * outs[0] + (e[1] / tot) * outs[1] + (e[2] / tot) * outs[2]
    merged = jnp.where(own_lanes, merged, 0.0).reshape(C_HEADS, DEC_SEQ, C_GROUP_WIDTH)
    o_ref[...] = jnp.sum(merged, axis=0)


def _attn_sample(h, caches, layer):
    m = h.shape[0]
    bsz = m // DEC_SEQ
    qkv = h[:, OFF_CQ:D_IN]
    slopes = jnp.array([[_slope(g, hh) for hh in range(C_HEADS)] for g in range(C_GROUPS)], F32)
    slopes = jnp.broadcast_to(jnp.repeat(slopes, DEC_SEQ, axis=1)[:, :, None], (C_GROUPS, C_HEADS * DEC_SEQ, 128))
    in_specs = [pl.BlockSpec((DEC_SEQ, D_IN - OFF_CQ), lambda b: (b, 0)),
                pl.BlockSpec((C_GROUPS, C_HEADS * DEC_SEQ, 128), lambda b: (0, 0, 0))]
    views = []
    for g, (w, _) in enumerate(C_CONFIGS):
        for j in range(2):
            c = caches[2 * g + j]
            assert c.shape[2] == w, "the cache must hold the last W_g positions"
            views.append(jnp.transpose(c, (0, 1, 3, 4, 2)))
            in_specs.append(pl.BlockSpec((None, None, C_HEADS, C_HEAD_DIM, w), lambda b: (layer, b, 0, 0, 0)))
    return pl.pallas_call(
        _attn_sample_body,
        grid=(bsz,),
        in_specs=in_specs,
        out_specs=pl.BlockSpec((DEC_SEQ, C_GROUP_WIDTH), lambda b: (b, 0)),
        out_shape=jax.ShapeDtypeStruct((m, C_GROUP_WIDTH), F32),
        compiler_params=_params("arbitrary"),
        name="attn_sample",
    )(qkv, slopes, *views)


def _merge_out_body(x_ref, oa_ref, ob_ref, oc_ref, gate_ref, wb_ref, wo_ref, g_ref, out_ref):
    merged = None
    for n, o_ref in enumerate((oa_ref, ob_ref, oc_ref)):
        proj = jnp.dot(o_ref[...].astype(BF16), wb_ref[n * BRANCH_WIDTH:(n + 1) * BRANCH_WIDTH, :],
                       preferred_element_type=F32)
        term = jax.nn.sigmoid(gate_ref[:, n * D_MODEL:(n + 1) * D_MODEL].astype(F32)) * proj
        merged = term if merged is None else merged + term
    y = jnp.dot(merged.astype(BF16), wo_ref[...], preferred_element_type=F32)
    out_ref[...] = x_ref[...] + _rms(y, g_ref[...])


def _merge_out(x, h, o_a, o_b, o_c, w_branch_bf16, w_out_bf16, gain):
    m = x.shape[0]
    tm = min(m, 512)
    row512 = pl.BlockSpec((tm, BRANCH_WIDTH), lambda i: (i, 0))
    return pl.pallas_call(
        _merge_out_body,
        grid=(m // tm,),
        in_specs=[pl.BlockSpec((tm, D_MODEL), lambda i: (i, 0)), row512, row512, row512,
                  pl.BlockSpec((tm, N_BRANCH * D_MODEL), lambda i: (i, OFF_G // (N_BRANCH * D_MODEL))),
                  pl.BlockSpec((N_BRANCH * BRANCH_WIDTH, D_MODEL), lambda i: (0, 0)),
                  pl.BlockSpec((D_MODEL, D_MODEL), lambda i: (0, 0)),
                  pl.BlockSpec((1, D_MODEL), lambda i: (0, 0))],
        out_specs=pl.BlockSpec((tm, D_MODEL), lambda i: (i, 0)),
        out_shape=jax.ShapeDtypeStruct((m, D_MODEL), F32),
        compiler_params=_params("arbitrary"),
        name="merge_out",
    )(x, o_a, o_b, o_c, h, w_branch_bf16, w_out_bf16, gain.reshape(1, D_MODEL))


FF_CHUNK = D_FF // 2


def _ffn_body(x_ref, g1_ref, w1_ref, w2_ref, g2_ref, out_ref):
    x = x_ref[...]
    xn = _rms(x, g1_ref[...]).astype(BF16)
    y = None
    for c in range(D_FF // FF_CHUNK):
        gate = jnp.dot(xn, w1_ref[:, c * FF_CHUNK:(c + 1) * FF_CHUNK], preferred_element_type=F32)
        up = jnp.dot(xn, w1_ref[:, D_FF + c * FF_CHUNK:D_FF + (c + 1) * FF_CHUNK], preferred_element_type=F32)
        act = (gate * jax.nn.sigmoid(gate) * up).astype(BF16)
        part = jnp.dot(act, w2_ref[c * FF_CHUNK:(c + 1) * FF_CHUNK, :], preferred_element_type=F32)
        y = part if y is None else y + part
    out_ref[...] = x + _rms(y, g2_ref[...])


def _ffn(x, gain_pre, w1_bf16, w2_bf16, gain_post):
    m = x.shape[0]
    tm = min(m, 512)
    resident = dict(pipeline_mode=pl.Buffered(1))
    return pl.pallas_call(
        _ffn_body,
        grid=(m // tm,),
        in_specs=[pl.BlockSpec((tm, D_MODEL), lambda i: (i, 0)),
                  pl.BlockSpec((1, D_MODEL), lambda i: (0, 0)),
                  pl.BlockSpec((D_MODEL, 2 * D_FF), lambda i: (0, 0), **resident),
                  pl.BlockSpec((D_FF, D_MODEL), lambda i: (0, 0), **resident),
                  pl.BlockSpec((1, D_MODEL), lambda i: (0, 0))],
        out_specs=pl.BlockSpec((tm, D_MODEL), lambda i: (i, 0)),
        out_shape=jax.ShapeDtypeStruct((m, D_MODEL), F32),
        compiler_params=_params("arbitrary"),
        name="ffn",
    )(x, gain_pre.reshape(1, D_MODEL), w1_bf16, w2_bf16, gain_post.reshape(1, D_MODEL))


def _split_heads(cols):
    return cols.reshape(cols.shape[0], cols.shape[1], C_HEADS, C_HEAD_DIM)


def _prompt_trunk(x3, p):
    bsz, t, _ = x3.shape
    x = x3.reshape(bsz * t, D_MODEL)
    depth = p["w_in"].shape[0]
    bufs, kvs = [], []
    for l in range(depth):
        h, kv_tail = _in_proj(x, p["norm_pre_mix"][l], p["w_in"], l, bsz)
        (o_a,) = _token_mix(h, p["a_norm_g"][l], p["a_norm_b"][l], p["a_w_s"][l], p["a_b_s"][l], CHUNK, False)
        o_b, buf = _conv_prompt(h, bsz, p["b_conv_w"][l], p["b_conv_b"][l], p["b_norm_g"][l], p["b_norm_b"][l])
        o_c = _attn_prompt(h, bsz)
        x = _merge_out(x, h, o_a, o_b, o_c, p["w_branch"][l], p["w_out"][l], p["norm_post_mix"][l])
        x = _ffn(x, p["norm_pre_ffn"][l], p["ffn_w_in"][l], p["ffn_w_out"][l], p["norm_post_ffn"][l])
        bufs.append(buf)
        kv3 = kv_tail.reshape(bsz, TAIL, KV_WIDTH)
        layer_kv = []
        for g, (w, _) in enumerate(C_CONFIGS):
            keep = min(w, t)
            for off in (g * C_GROUP_WIDTH, C_GROUPS * C_GROUP_WIDTH + g * C_GROUP_WIDTH):
                layer_kv.append(_split_heads(kv3[:, TAIL - keep:, off:off + C_GROUP_WIDTH]))
        kvs.append(layer_kv)
    new_kv = [jnp.stack([kvs[l][j] for l in range(depth)], axis=0) for j in range(2 * C_GROUPS)]
    return x.reshape(bsz, t, D_MODEL), jnp.stack(bufs, axis=0), new_kv


def _sample_trunk(x3, conv_state, caches, p):
    bsz, t, _ = x3.shape
    x = x3.reshape(bsz * t, D_MODEL)
    depth = p["w_in"].shape[0]
    bufs, vns, kvs = [], [], []
    for l in range(depth):
        h = _in_proj_sample(x, p["norm_pre_mix"][l], p["w_in"], l)
        o_a, vn = _token_mix(h, p["a_norm_g"][l], p["a_norm_b"][l], p["a_w_s"][l], p["a_b_s"][l], t, True)
        o_b, buf = _conv_sample(h, conv_state[l], p["b_conv_w"][l], p["b_conv_b"][l],
                                p["b_norm_g"][l], p["b_norm_b"][l])
        o_c = _attn_sample(h, caches, l)
        x = _merge_out(x, h, o_a, o_b, o_c, p["w_branch"][l], p["w_out"][l], p["norm_post_mix"][l])
        x = _ffn(x, p["norm_pre_ffn"][l], p["ffn_w_in"][l], p["ffn_w_out"][l], p["norm_post_ffn"][l])
        bufs.append(buf)
        vns.append(vn.reshape(bsz, t, A_WIDTH))
        h3 = h.reshape(bsz, t, D_IN)
        kvs.append([_split_heads(h3[:, :, off + g * C_GROUP_WIDTH:off + (g + 1) * C_GROUP_WIDTH])
                    for g in range(C_GROUPS) for off in (OFF_CK, OFF_CV)])
    new_kv = [jnp.stack([kvs[l][j] for l in range(depth)], axis=0) for j in range(2 * C_GROUPS)]
    return x.reshape(bsz, t, D_MODEL), jnp.stack(bufs, axis=0), jnp.stack(vns, axis=0), new_kv


def kernel(x_prompt, x_sample, state_b_conv, cache_c0_k, cache_c0_v, cache_c1_k, cache_c1_v, cache_c2_k,
           cache_c2_v, norm_pre_mix, norm_post_mix, norm_pre_ffn, norm_post_ffn, w_in, a_norm_g, a_norm_b,
           a_w_s, a_b_s, b_conv_w, b_conv_b, b_norm_g, b_norm_b, w_branch, w_out, ffn_w_in, ffn_w_out):
    assert x_sample.shape[1] == DEC_SEQ
    p = dict(norm_pre_mix=norm_pre_mix, norm_post_mix=norm_post_mix, norm_pre_ffn=norm_pre_ffn,
             norm_post_ffn=norm_post_ffn, w_in=w_in, a_norm_g=a_norm_g, a_norm_b=a_norm_b,
             a_w_s=a_w_s, a_b_s=a_b_s, b_conv_w=b_conv_w, b_conv_b=b_conv_b, b_norm_g=b_norm_g,
             b_norm_b=b_norm_b, w_branch=w_branch.astype(BF16), w_out=w_out.astype(BF16),
             ffn_w_in=ffn_w_in.astype(BF16), ffn_w_out=ffn_w_out.astype(BF16))
    caches = (cache_c0_k, cache_c0_v, cache_c1_k, cache_c1_v, cache_c2_k, cache_c2_v)
    y_prompt, buf_p, kv_p = _prompt_trunk(x_prompt, p)
    y_sample, buf_s, vn_s, kv_s = _sample_trunk(x_sample, state_b_conv, caches, p)
    return (y_prompt, y_sample, buf_p, buf_s, vn_s, *kv_p, *kv_s)
```

```python
import functools

import jax
import jax.numpy as jnp
from jax import lax
from jax.experimental import pallas as pl
from jax.experimental.pallas import tpu as pltpu

F32 = jnp.float32
BF16 = jnp.bfloat16

D_MODEL = 1024
DEC_SEQ = 8
A_WIDTH = 512
A_GROUPS = 4
A_GROUP_DIM = 128
CHUNK = 128
B_WIDTH = 512
CONV_W = 31
CONV_HALO = 32
C_HEADS = 8
C_HEAD_DIM = 64
C_GROUP_WIDTH = 512
C_CONFIGS = ((128, 1), (512, 4), (2048, 16))
C_GROUPS = 3
C_WINDOW = 128
N_BRANCH = 3
BRANCH_WIDTH = 512
D_FF = 2816
EPS = 1e-6
NEG_INF = -1e30
ATTN_SCALE = C_HEAD_DIM ** -0.5
REF_OFF_G = 6656
OFF_G = 0
OFF_AU = 3072
OFF_AV = 3584
OFF_B = 4096
OFF_CQ = 5120
OFF_CK = 6656
OFF_CV = 8192
D_IN = 9728
COL_BLOCKS = D_IN // 512

VMEM_LIMIT = 56 * 1024 * 1024


def _slope(g, h):
    n = C_GROUPS * C_HEADS
    return 2.0 ** (-8.0 * (h * C_GROUPS + g + 1.0) / n)


def _params(*sem):
    return pltpu.CompilerParams(dimension_semantics=sem, vmem_limit_bytes=VMEM_LIMIT)


def _rms(x, gain):
    return x * lax.rsqrt(jnp.mean(x * x, axis=-1, keepdims=True) + EPS) * gain


def _layer_norm(x, gain, bias):
    mu = jnp.mean(x, axis=-1, keepdims=True)
    xc = x - mu
    var = jnp.mean(xc * xc, axis=-1, keepdims=True)
    return xc * lax.rsqrt(var + EPS) * gain + bias


Q_TILE0 = OFF_CQ // 512
KV_TILE0 = OFF_CK // 512
KV_WIDTH = D_IN - OFF_CK
TAIL = 2048


def _in_proj_sample_body(x_ref, g_ref, w_ref, o_ref, xn_ref):
    @pl.when(pl.program_id(1) == 0)
    def _():
        xn_ref[...] = _rms(x_ref[...], g_ref[...]).astype(BF16)

    o_ref[...] = jnp.dot(xn_ref[...], w_ref[...].astype(BF16), preferred_element_type=F32)


def _w_in_tile(j):
    return lax.rem(j + REF_OFF_G // 512, COL_BLOCKS)


def _in_proj_sample(x, gain, w_in, layer):
    m = x.shape[0]
    tn = 512
    return pl.pallas_call(
        _in_proj_sample_body,
        grid=(1, D_IN // tn),
        in_specs=[pl.BlockSpec((m, D_MODEL), lambda i, j: (0, 0)),
                  pl.BlockSpec((1, D_MODEL), lambda i, j: (0, 0)),
                  pl.BlockSpec((None, D_MODEL, tn), lambda i, j: (layer, 0, _w_in_tile(j)))],
        out_specs=pl.BlockSpec((m, tn), lambda i, j: (0, j)),
        out_shape=jax.ShapeDtypeStruct((m, D_IN), F32),
        scratch_shapes=[pltpu.VMEM((m, D_MODEL), BF16)],
        compiler_params=_params("arbitrary", "arbitrary"),
        name="in_proj_sample",
    )(x, gain.reshape(1, D_MODEL), w_in)


def _in_proj_body(x_ref, g_ref, w_ref, o_ref, kv_ref, xn_ref, res_ref, tmp_ref, *, tm, tiles_per_seq):
    i = pl.program_id(0)
    j = pl.program_id(1)
    lanes = res_ref.shape[1]

    def matmul_tile(slot):
        res = jnp.dot(xn_ref[...], w_ref[...].astype(BF16), preferred_element_type=F32)
        for c in range(lanes):
            res_ref[slot, c] = res[:, c * 128:(c + 1) * 128]

    def emit(slot, d):
        if d == 1:
            for c in range(lanes):
                o_ref[:, c * 128:(c + 1) * 128] = res_ref[slot, c].astype(BF16)
            return
        if d == 16:
            quarter = tm // 4
            for b4 in range(4):
                for c in range(lanes):
                    tmp_ref[c, b4 * quarter:(b4 + 1) * quarter, :] = res_ref[slot, c, pl.ds(b4, quarter, stride=4), :]
            for r in range(d):
                for c in range(lanes):
                    o_ref[r * C_WINDOW:(r + 1) * C_WINDOW, c * 128:(c + 1) * 128] = (
                        tmp_ref[c, pl.ds((r % 4) * quarter + r // 4, C_WINDOW, stride=4), :].astype(BF16))
            return
        for u in range(tm // (C_WINDOW * d)):
            for r in range(d):
                dst = u * C_WINDOW * d + r * C_WINDOW
                for c in range(lanes):
                    o_ref[dst:dst + C_WINDOW, c * 128:(c + 1) * 128] = (
                        res_ref[slot, c, pl.ds(u * C_WINDOW * d + r, C_WINDOW, stride=d), :].astype(BF16))

    @pl.when(j == 0)
    def _():
        xn_ref[...] = _rms(x_ref[...], g_ref[...]).astype(BF16)
        matmul_tile(0)

    prev_group = lax.rem(jnp.maximum(j - 1 - Q_TILE0, 0), C_GROUPS)
    prev_dilation_group = jnp.where(j - 1 >= Q_TILE0, prev_group, 0)
    in_range = jnp.logical_and(j >= 1, j < COL_BLOCKS)
    for parity in range(2):
        for g, (_, d) in enumerate(C_CONFIGS):
            @pl.when(jnp.logical_and(jnp.logical_and(in_range, lax.rem(j, 2) == parity), prev_dilation_group == g))
            def _():
                emit(1 - parity, d)
                matmul_tile(parity)

    @pl.when(j == COL_BLOCKS)
    def _():
        emit((COL_BLOCKS - 1) % 2, C_CONFIGS[(COL_BLOCKS - 1 - Q_TILE0) % C_GROUPS][1])

    @pl.when(jnp.logical_and(i % tiles_per_seq == tiles_per_seq - 1,
                             jnp.logical_and(j >= KV_TILE0, j < COL_BLOCKS)))
    def _():
        for c in range(lanes):
            kv_ref[:, c * 128:(c + 1) * 128] = res_ref[lax.rem(j, 2), c]


def _in_proj(x, gain, w_in, layer, bsz):
    m = x.shape[0]
    t = m // bsz
    tm = TAIL
    tn = 512
    assert t % tm == 0
    tiles_per_seq = t // tm
    last = COL_BLOCKS - 1

    def kv_index(i, j):
        is_tail = i % tiles_per_seq == tiles_per_seq - 1
        return i // tiles_per_seq, jnp.where(is_tail, jnp.clip(j - KV_TILE0, 0, last - KV_TILE0), 0)

    return pl.pallas_call(
        functools.partial(_in_proj_body, tm=tm, tiles_per_seq=tiles_per_seq),
        grid=(m // tm, COL_BLOCKS + 1),
        in_specs=[pl.BlockSpec((tm, D_MODEL), lambda i, j: (i, 0)),
                  pl.BlockSpec((1, D_MODEL), lambda i, j: (0, 0)),
                  pl.BlockSpec((None, D_MODEL, tn), lambda i, j: (layer, 0, _w_in_tile(jnp.minimum(j, last))))],
        out_specs=[pl.BlockSpec((tm, tn), lambda i, j: (i, jnp.maximum(j - 1, 0))),
                   pl.BlockSpec((tm, tn), kv_index)],
        out_shape=[jax.ShapeDtypeStruct((m, D_IN), BF16),
                   jax.ShapeDtypeStruct((bsz * TAIL, KV_WIDTH), F32)],
        scratch_shapes=[pltpu.VMEM((tm, D_MODEL), BF16), pltpu.VMEM((2, tn // 128, tm, 128), F32),
                        pltpu.VMEM((tn // 128, tm, 128), F32)],
        compiler_params=_params("arbitrary", "arbitrary"),
        name="in_proj",
    )(x, gain.reshape(1, D_MODEL), w_in)


def _token_mix_body(u_ref, v_ref, ng_ref, nb_ref, w_ref, bias_ref, o_ref, *maybe_vn_ref, r, chunk_len, n_sub):
    row = lax.broadcasted_iota(jnp.int32, (r, r), 0)
    col = lax.broadcasted_iota(jnp.int32, (r, r), 1)
    shift = chunk_len.bit_length() - 1
    keep = ((row >> shift) == (col >> shift)) & (col <= row)
    wm = [jnp.where(keep, w_ref[g], 0.0).astype(BF16) for g in range(A_GROUPS)]
    for c in range(n_sub):
        rows = slice(c * r, (c + 1) * r)
        vn = _layer_norm(v_ref[rows, :].astype(F32), ng_ref[...], nb_ref[...])
        for vn_ref in maybe_vn_ref:
            vn_ref[rows, :] = vn
        vnb = vn.astype(BF16)
        for g in range(A_GROUPS):
            lanes = slice(g * A_GROUP_DIM, (g + 1) * A_GROUP_DIM)
            mix = jnp.dot(wm[g], vnb[:, lanes], preferred_element_type=F32) + bias_ref[:, lanes]
            o_ref[rows, lanes] = u_ref[rows, lanes].astype(F32) * mix


def _token_mix(h, norm_g, norm_b, w_s, b_s, chunk_len, emit_vn):
    m = h.shape[0]
    if chunk_len == CHUNK:
        r, n_sub = CHUNK, min(4, m // CHUNK)
        w = w_s
        bias_rows = b_s.T
    else:
        r, n_sub = m, 1
        reps = m // chunk_len
        w = jnp.tile(w_s[:, :chunk_len, :chunk_len], (1, reps, reps))
        bias_rows = jnp.tile(b_s[:, :chunk_len].T, (reps, 1))
    bias = jnp.repeat(bias_rows, A_GROUP_DIM, axis=1)
    tr = r * n_sub
    body = functools.partial(_token_mix_body, r=r, chunk_len=chunk_len, n_sub=n_sub)
    return pl.pallas_call(
        body,
        grid=(m // tr,),
        in_specs=[pl.BlockSpec((tr, A_WIDTH), lambda i: (i, OFF_AU // A_WIDTH)),
                  pl.BlockSpec((tr, A_WIDTH), lambda i: (i, OFF_AV // A_WIDTH)),
                  pl.BlockSpec((1, A_WIDTH), lambda i: (0, 0)),
                  pl.BlockSpec((1, A_WIDTH), lambda i: (0, 0)),
                  pl.BlockSpec((A_GROUPS, r, r), lambda i: (0, 0, 0)),
                  pl.BlockSpec((r, A_WIDTH), lambda i: (0, 0))],
        out_specs=[pl.BlockSpec((tr, A_WIDTH), lambda i: (i, 0))] * (2 if emit_vn else 1),
        out_shape=[jax.ShapeDtypeStruct((m, A_WIDTH), F32)] * (2 if emit_vn else 1),
        compiler_params=_params("arbitrary"),
        name="token_mix",
    )(h, h, norm_g.reshape(1, A_WIDTH), norm_b.reshape(1, A_WIDTH), w, bias)


def _glu(z):
    return z[:, :B_WIDTH] * jax.nn.sigmoid(z[:, B_WIDTH:])


def _local_mixers_body(u_ref, v_ref, ng_ref, nb_ref, w_ref, bias_ref, z_ref, zh_ref, cw_ref, cb_ref, lg_ref, lb_ref,
                       oa_ref, ob_ref, buf_ref, pad_ref, *, tr, tiles):
    row = lax.broadcasted_iota(jnp.int32, (CHUNK, CHUNK), 0)
    col = lax.broadcasted_iota(jnp.int32, (CHUNK, CHUNK), 1)
    wm = [jnp.where(col <= row, w_ref[g], 0.0).astype(BF16) for g in range(A_GROUPS)]
    for c in range(tr // CHUNK):
        rows = slice(c * CHUNK, (c + 1) * CHUNK)
        vnb = _layer_norm(v_ref[rows, :].astype(F32), ng_ref[...], nb_ref[...]).astype(BF16)
        for g in range(A_GROUPS):
            lanes = slice(g * A_GROUP_DIM, (g + 1) * A_GROUP_DIM)
            mix = jnp.dot(wm[g], vnb[:, lanes], preferred_element_type=F32) + bias_ref[:, lanes]
            oa_ref[rows, lanes] = u_ref[rows, lanes].astype(F32) * mix

    t = pl.program_id(0) % tiles
    hg = _glu(z_ref[...].astype(F32))
    pad_ref[0:CONV_HALO, :] = jnp.where(t == 0, 0.0, _glu(zh_ref[...].astype(F32)))
    pad_ref[CONV_HALO:CONV_HALO + tr, :] = hg
    first = CONV_HALO - (CONV_W - 1)
    sub = 32
    taps = [[k for k in range(CONV_W) if (first + k) % 8 == s] for s in range(8)]
    for blk in range(tr // sub):
        acc = jnp.zeros((sub // 8, 8, B_WIDTH), F32)
        for s in range(8):
            q_max = max((first + k) // 8 for k in taps[s])
            win = pad_ref[blk * sub + s:blk * sub + s + sub + 8 * q_max, :]
            for k in taps[s]:
                q = (first + k) // 8
                acc = acc + win[8 * q:8 * q + sub, :].reshape(sub // 8, 8, B_WIDTH) * cw_ref[k]
        y = _layer_norm(acc.reshape(sub, B_WIDTH) + cb_ref[...], lg_ref[...], lb_ref[...])
        ob_ref[blk * sub:(blk + 1) * sub, :] = y * jax.nn.sigmoid(y)

    @pl.when(t == tiles - 1)
    def _():
        buf_ref[...] = pad_ref[CONV_HALO + tr - (CONV_W - 1):CONV_HALO + tr, :]


def _local_mixers(h, bsz, norm_g, norm_b, w_s, b_s, conv_w, conv_b, ln_g, ln_b):
    m = h.shape[0]
    t = m // bsz
    tr = min(512, t)
    tiles = t // tr
    halo_per_tile = tr // CONV_HALO
    bias = jnp.repeat(b_s.T, A_GROUP_DIM, axis=1)
    vec = pl.BlockSpec((1, B_WIDTH), lambda i: (0, 0))
    body = functools.partial(_local_mixers_body, tr=tr, tiles=tiles)
    return pl.pallas_call(
        body,
        grid=(m // tr,),
        in_specs=[pl.BlockSpec((tr, A_WIDTH), lambda i: (i, OFF_AU // A_WIDTH)),
                  pl.BlockSpec((tr, A_WIDTH), lambda i: (i, OFF_AV // A_WIDTH)),
                  vec, vec,
                  pl.BlockSpec((A_GROUPS, CHUNK, CHUNK), lambda i: (0, 0, 0)),
                  pl.BlockSpec((CHUNK, A_WIDTH), lambda i: (0, 0)),
                  pl.BlockSpec((tr, 2 * B_WIDTH), lambda i: (i, OFF_B // (2 * B_WIDTH))),
                  pl.BlockSpec((CONV_HALO, 2 * B_WIDTH),
                               lambda i: (jnp.maximum(i * halo_per_tile - 1, 0), OFF_B // (2 * B_WIDTH))),
                  pl.BlockSpec((CONV_W, 8, B_WIDTH), lambda i: (0, 0, 0)),
                  vec, vec, vec],
        out_specs=[pl.BlockSpec((tr, A_WIDTH), lambda i: (i, 0)),
                   pl.BlockSpec((tr, B_WIDTH), lambda i: (i, 0)),
                   pl.BlockSpec((None, CONV_W - 1, B_WIDTH), lambda i: (i // tiles, 0, 0))],
        out_shape=[jax.ShapeDtypeStruct((m, A_WIDTH), F32),
                   jax.ShapeDtypeStruct((m, B_WIDTH), F32),
                   jax.ShapeDtypeStruct((bsz, CONV_W - 1, B_WIDTH), F32)],
        scratch_shapes=[pltpu.VMEM((CONV_HALO + tr, B_WIDTH), F32)],
        compiler_params=_params("arbitrary"),
        name="local_mixers",
    )(h, h, norm_g.reshape(1, A_WIDTH), norm_b.reshape(1, A_WIDTH), w_s, bias, h, h,
      jnp.broadcast_to(conv_w[:, None, :], (CONV_W, 8, B_WIDTH)), conv_b.reshape(1, B_WIDTH),
      ln_g.reshape(1, B_WIDTH), ln_b.reshape(1, B_WIDTH))


def _conv_sample_body(z_ref, st_ref, cw_ref, cb_ref, lg_ref, lb_ref, o_ref, buf_ref, pad_ref, *, bsz):
    hg = _glu(z_ref[...])
    pad_ref[:, 0:CONV_W - 1, :] = st_ref[...]
    pad_ref[:, CONV_W - 1:CONV_W - 1 + DEC_SEQ, :] = hg.reshape(bsz, DEC_SEQ, B_WIDTH)
    acc = jnp.zeros((bsz, DEC_SEQ, B_WIDTH), F32)
    for k in range(CONV_W):
        acc = acc + cw_ref[k:k + 1, :] * pad_ref[:, k:k + DEC_SEQ, :]
    y = _layer_norm(acc + cb_ref[...], lg_ref[...], lb_ref[...])
    o_ref[...] = (y * jax.nn.sigmoid(y)).reshape(bsz * DEC_SEQ, B_WIDTH)
    buf_ref[...] = pad_ref[:, DEC_SEQ:DEC_SEQ + CONV_W - 1, :]


def _conv_sample(h, state, conv_w, conv_b, ln_g, ln_b):
    m = h.shape[0]
    bsz = state.shape[0]
    body = functools.partial(_conv_sample_body, bsz=bsz)
    return pl.pallas_call(
        body,
        grid=(1,),
        in_specs=[pl.BlockSpec((m, 2 * B_WIDTH), lambda i: (0, OFF_B // (2 * B_WIDTH))),
                  pl.BlockSpec((bsz, CONV_W - 1, B_WIDTH), lambda i: (0, 0, 0)),
                  pl.BlockSpec((CONV_W, B_WIDTH), lambda i: (0, 0)),
                  pl.BlockSpec((1, B_WIDTH), lambda i: (0, 0)),
                  pl.BlockSpec((1, B_WIDTH), lambda i: (0, 0)),
                  pl.BlockSpec((1, B_WIDTH), lambda i: (0, 0))],
        out_specs=[pl.BlockSpec((m, B_WIDTH), lambda i: (0, 0)),
                   pl.BlockSpec((bsz, CONV_W - 1, B_WIDTH), lambda i: (0, 0, 0))],
        out_shape=[jax.ShapeDtypeStruct((m, B_WIDTH), F32),
                   jax.ShapeDtypeStruct((bsz, CONV_W - 1, B_WIDTH), F32)],
        scratch_shapes=[pltpu.VMEM((bsz, CONV_W - 1 + DEC_SEQ, B_WIDTH), F32)],
        compiler_params=_params("arbitrary"),
        name="conv_sample",
    )(h, state, conv_w, conv_b.reshape(1, B_WIDTH), ln_g.reshape(1, B_WIDTH), ln_b.reshape(1, B_WIDTH))


def _dot_nt(a, b):
    return lax.dot_general(a, b, (((1,), (1,)), ((), ())), preferred_element_type=F32)


ATTN_TILE = 2048
ATTN_ITEMS = 8


def _attn_prompt_body(slope_ref, *refs):
    group_refs = [refs[5 * g:5 * g + 5] for g in range(C_GROUPS)]
    o_ref, og_ref, lg_ref, sm_ref, sx_ref, pp_ref, pc_ref, mb_ref = refs[5 * C_GROUPS:]
    tq = C_WINDOW
    first_tile = pl.program_id(1) == 0
    hp = pl.program_id(2)
    row = lax.broadcasted_iota(jnp.int32, (tq, tq), 0)
    col = lax.broadcasted_iota(jnp.int32, (tq, tq), 1)
    upper = col > row
    diag = col == row
    diag0 = col == row + jnp.where(first_tile, tq, 0)
    valid0 = col <= row + jnp.where(first_tile, 0, tq)
    dist = jnp.where(upper, row - col + tq, row - col).astype(F32)
    low = lax.broadcasted_iota(jnp.int32, (tq, 128), 1) < C_HEAD_DIM
    sels = (low, jnp.logical_not(low))

    for g, (_, d) in enumerate(C_CONFIGS):
        q_ref, kp_ref, kc_ref, vp_ref, vc_ref = group_refs[g]
        slopes = [slope_ref[g * C_HEADS + 2 * hp + hh] * float(d) for hh in range(2)]
        bias = [-(s * dist) for s in slopes]
        bias_x = [-(s * float(tq)) for s in slopes]
        blocks = [(u, r) for u in range(ATTN_TILE // (tq * d)) for r in range(d)]

        def rows(u, r):
            return pl.ds(u * tq * d + r * tq, tq)

        def out_rows(u, r):
            return pl.ds(u * tq * d + r, tq, stride=d) if d > 1 else pl.ds(u * tq, tq)

        for first in range(0, len(blocks), ATTN_ITEMS):
            items = blocks[first:first + ATTN_ITEMS]
            for n, (u, r) in enumerate(items):
                q2 = q_ref[rows(u, r), :] * ATTN_SCALE
                kc2 = kc_ref[rows(u, r), :]
                kp2 = kp_ref[rows(0, r), :] if u == 0 else kc_ref[rows(u - 1, r), :]
                for hh in range(2):
                    qm = jnp.where(sels[hh], q2, 0.0)
                    sp = _dot_nt(qm, kp2)
                    sm = jnp.where(upper, sp, _dot_nt(qm, kc2)) + bias[hh]
                    if u == 0:
                        sm = jnp.where(valid0, sm, NEG_INF)
                    sm_ref[2 * n + hh] = sm
                    sx_ref[2 * n + hh] = jnp.where(diag0 if u == 0 else diag, sp + bias_x[hh], NEG_INF)
            for n in range(len(items)):
                mx = []
                for hh in range(2):
                    sm = sm_ref[2 * n + hh]
                    sx = sx_ref[2 * n + hh]
                    m = jnp.max(jnp.maximum(sm, sx), axis=-1, keepdims=True)
                    p = jnp.exp(sm - m)
                    pp_ref[2 * n + hh] = jnp.where(upper, p, jnp.exp(sx - m)).astype(BF16)
                    pc_ref[2 * n + hh] = jnp.where(upper, 0.0, p).astype(BF16)
                    mx.append(m)
                mb_ref[n] = jnp.where(low, mx[0], mx[1])
            for n, (u, r) in enumerate(items):
                vc2 = vc_ref[rows(u, r), :]
                vp2 = vp_ref[rows(0, r), :] if u == 0 else vc_ref[rows(u - 1, r), :]
                acc = []
                for hh in range(2):
                    vpx = jnp.where(sels[hh], vp2, 1.0)
                    vcx = jnp.where(sels[hh], vc2, 1.0)
                    acc.append(jnp.dot(pp_ref[2 * n + hh], vpx, preferred_element_type=F32)
                               + jnp.dot(pc_ref[2 * n + hh], vcx, preferred_element_type=F32))
                den = pltpu.roll(jnp.where(low, acc[1], acc[0]), C_HEAD_DIM, axis=1)
                og_ref[g, out_rows(u, r), :] = jnp.where(low, acc[0], acc[1]) / den
                lg_ref[g, out_rows(u, r), :] = mb_ref[n] + jnp.log(den)

    chunk = 256
    for c in range(ATTN_TILE // chunk):
        rs = slice(c * chunk, (c + 1) * chunk)
        l0, l1, l2 = lg_ref[0, rs, :], lg_ref[1, rs, :], lg_ref[2, rs, :]
        mx = jnp.maximum(jnp.maximum(l0, l1), l2)
        e0, e1, e2 = jnp.exp(l0 - mx), jnp.exp(l1 - mx), jnp.exp(l2 - mx)
        tot = e0 + e1 + e2
        o_ref[rs, :] = (e0 / tot) * og_ref[0, rs, :] + (e1 / tot) * og_ref[1, rs, :] + (e2 / tot) * og_ref[2, rs, :]


def _attn_prompt(h, bsz):
    m = h.shape[0]
    t = m // bsz
    assert t % ATTN_TILE == 0
    tiles = t // ATTN_TILE
    in_specs = [pl.BlockSpec(memory_space=pltpu.SMEM)]
    for g, (_, d) in enumerate(C_CONFIGS):
        unit_rows = C_WINDOW * d
        units = ATTN_TILE // unit_rows
        cq, ck, cv = ((off + g * C_GROUP_WIDTH) // 128 for off in (OFF_CQ, OFF_CK, OFF_CV))

        def cur(c):
            return pl.BlockSpec((ATTN_TILE, 128), lambda b, i, hp, c=c: (b * tiles + i, c + hp))

        def prev(c, units=units, unit_rows=unit_rows):
            return pl.BlockSpec((unit_rows, 128),
                                lambda b, i, hp, c=c: (jnp.maximum((b * tiles + i) * units - 1, 0), c + hp))

        in_specs += [cur(cq), prev(ck), cur(ck), prev(cv), cur(cv)]
    slopes = jnp.array([_slope(gg, hh) for gg in range(C_GROUPS) for hh in range(C_HEADS)], F32)
    n_items = 2 * ATTN_ITEMS
    return pl.pallas_call(
        _attn_prompt_body,
        grid=(bsz, tiles, C_HEADS // 2),
        in_specs=in_specs,
        out_specs=pl.BlockSpec((ATTN_TILE, 128), lambda b, i, hp: (b * tiles + i, hp)),
        out_shape=jax.ShapeDtypeStruct((m, C_GROUP_WIDTH), F32),
        scratch_shapes=[pltpu.VMEM((C_GROUPS, ATTN_TILE, 128), F32)] * 2
        + [pltpu.VMEM((n_items, C_WINDOW, C_WINDOW), F32)] * 2
        + [pltpu.VMEM((n_items, C_WINDOW, C_WINDOW), BF16)] * 2
        + [pltpu.VMEM((n_items // 2, C_WINDOW, 128), F32)],
        compiler_params=_params("arbitrary", "arbitrary", "arbitrary"),
        name="attn_prompt",
    )(slopes, *([h] * (5 * C_GROUPS)))


def _attn_sample_body(qkv_ref, slope_ref, k0_ref, v0_ref, k1_ref, v1_ref, k2_ref, v2_ref, o_ref):
    caches = ((k0_ref, v0_ref), (k1_ref, v1_ref), (k2_ref, v2_ref))
    rows = C_HEADS * DEC_SEQ
    rid = lax.broadcasted_iota(jnp.int32, (rows, 1), 0)
    i_row = rid & (DEC_SEQ - 1)
    own_lanes = (lax.broadcasted_iota(jnp.int32, (rows, C_GROUP_WIDTH), 1) >> 6) == (
        lax.broadcasted_iota(jnp.int32, (rows, C_GROUP_WIDTH), 0) >> 3)
    outs, lses = [], []
    for g, (w, d) in enumerate(C_CONFIGS):
        kt_ref, vt_ref = caches[g]
        q = qkv_ref[:, g * 512:(g + 1) * 512] * ATTN_SCALE
        kn = qkv_ref[:, (3 + g) * 512:(4 + g) * 512]
        vn = qkv_ref[:, (6 + g) * 512:(7 + g) * 512]
        qbd = jnp.where(own_lanes, jnp.concatenate([q] * C_HEADS, axis=0), 0.0)
        slope = slope_ref[g, :, 0:1]
        kt = kt_ref[...].reshape(C_GROUP_WIDTH, w).astype(BF16)
        vt = vt_ref[...].reshape(C_GROUP_WIDTH, w).astype(BF16)
        pos = lax.broadcasted_iota(jnp.int32, (rows, w), 1)
        back = (w + i_row) - pos
        ok = (pos >= i_row) if d == 1 else ((back & (d - 1)) == 0) & (pos >= i_row)
        s_c = jnp.dot(qbd.astype(BF16), kt, preferred_element_type=F32)
        s_c = jnp.where(ok, s_c - slope * back.astype(F32), NEG_INF)
        s_n = []
        for i2 in range(DEC_SEQ):
            s = jnp.sum(qbd * kn[i2:i2 + 1, :], axis=-1, keepdims=True)
            back_n = i_row - i2
            ok_n = (back_n >= 0) if d == 1 else ((back_n & (d - 1)) == 0) & (back_n >= 0)
            s_n.append(jnp.where(ok_n, s - slope * back_n.astype(F32), NEG_INF))
        mx = jnp.max(s_c, axis=-1, keepdims=True)
        for s in s_n:
            mx = jnp.maximum(mx, s)
        p_c = jnp.exp(s_c - mx)
        den = jnp.sum(p_c, axis=-1, keepdims=True)
        acc = _dot_nt(p_c.astype(BF16), vt)
        for i2, s in enumerate(s_n):
            p = jnp.exp(s - mx)
            den = den + p
            acc = acc + p * vn[i2:i2 + 1, :]
        outs.append(acc / den)
        lses.append(mx + jnp.log(den))
    mx = jnp.maximum(jnp.maximum(lses[0], lses[1]), lses[2])
    e = [jnp.exp(l - mx) for l in lses]
    tot = e[0] + e[1] + e[2]
    merged = (e[0] / tot) * outs[0] + (e[1] / tot) * outs[1] + (e[2] / tot) * outs[2]
    merged = jnp.where(own_lanes, merged, 0.0).reshape(C_HEADS, DEC_SEQ, C_GROUP_WIDTH)
    o_ref[...] = jnp.sum(merged, axis=0)


def _attn_sample(h, caches, layer):
    m = h.shape[0]
    bsz = m // DEC_SEQ
    qkv = h[:, OFF_CQ:D_IN]
    slopes = jnp.array([[_slope(g, hh) for hh in range(C_HEADS)] for g in range(C_GROUPS)], F32)
    slopes = jnp.broadcast_to(jnp.repeat(slopes, DEC_SEQ, axis=1)[:, :, None], (C_GROUPS, C_HEADS * DEC_SEQ, 128))
    in_specs = [pl.BlockSpec((DEC_SEQ, D_IN - OFF_CQ), lambda b: (b, 0)),
                pl.BlockSpec((C_GROUPS, C_HEADS * DEC_SEQ, 128), lambda b: (0, 0, 0))]
    views = []
    for g, (w, _) in enumerate(C_CONFIGS):
        for j in range(2):
            c = caches[2 * g + j]
            assert c.shape[2] == w, "the cache must hold the last W_g positions"
            views.append(jnp.transpose(c, (0, 1, 3, 4, 2)))
            in_specs.append(pl.BlockSpec((None, None, C_HEADS, C_HEAD_DIM, w), lambda b: (layer, b, 0, 0, 0)))
    return pl.pallas_call(
        _attn_sample_body,
        grid=(bsz,),
        in_specs=in_specs,
        out_specs=pl.BlockSpec((DEC_SEQ, C_GROUP_WIDTH), lambda b: (b, 0)),
        out_shape=jax.ShapeDtypeStruct((m, C_GROUP_WIDTH), F32),
        compiler_params=_params("arbitrary"),
        name="attn_sample",
    )(qkv, slopes, *views)


def _merge_out_body(x_ref, oa_ref, ob_ref, oc_ref, gate_ref, wb_ref, wo_ref, g_ref, out_ref):
    merged = None
    for n, o_ref in enumerate((oa_ref, ob_ref, oc_ref)):
        proj = jnp.dot(o_ref[...].astype(BF16), wb_ref[n * BRANCH_WIDTH:(n + 1) * BRANCH_WIDTH, :],
                       preferred_element_type=F32)
        term = jax.nn.sigmoid(gate_ref[:, n * D_MODEL:(n + 1) * D_MODEL].astype(F32)) * proj
        merged = term if merged is None else merged + term
    y = jnp.dot(merged.astype(BF16), wo_ref[...], preferred_element_type=F32)
    out_ref[...] = x_ref[...] + _rms(y, g_ref[...])


def _merge_out(x, h, o_a, o_b, o_c, w_branch_bf16, w_out_bf16, gain):
    m = x.shape[0]
    tm = min(m, 512)
    row512 = pl.BlockSpec((tm, BRANCH_WIDTH), lambda i: (i, 0))
    return pl.pallas_call(
        _merge_out_body,
        grid=(m // tm,),
        in_specs=[pl.BlockSpec((tm, D_MODEL), lambda i: (i, 0)), row512, row512, row512,
                  pl.BlockSpec((tm, N_BRANCH * D_MODEL), lambda i: (i, OFF_G // (N_BRANCH * D_MODEL))),
                  pl.BlockSpec((N_BRANCH * BRANCH_WIDTH, D_MODEL), lambda i: (0, 0)),
                  pl.BlockSpec((D_MODEL, D_MODEL), lambda i: (0, 0)),
                  pl.BlockSpec((1, D_MODEL), lambda i: (0, 0))],
        out_specs=pl.BlockSpec((tm, D_MODEL), lambda i: (i, 0)),
        out_shape=jax.ShapeDtypeStruct((m, D_MODEL), F32),
        compiler_params=_params("arbitrary"),
        name="merge_out",
    )(x, o_a, o_b, o_c, h, w_branch_bf16, w_out_bf16, gain.reshape(1, D_MODEL))


FF_CHUNK = D_FF // 2


def _ffn_body(x_ref, g1_ref, w1_ref, w2_ref, g2_ref, out_ref):
    x = x_ref[...]
    xn = _rms(x, g1_ref[...]).astype(BF16)
    y = None
    for c in range(D_FF // FF_CHUNK):
        gate = jnp.dot(xn, w1_ref[:, c * FF_CHUNK:(c + 1) * FF_CHUNK], preferred_element_type=F32)
        up = jnp.dot(xn, w1_ref[:, D_FF + c * FF_CHUNK:D_FF + (c + 1) * FF_CHUNK], preferred_element_type=F32)
        act = (gate * jax.nn.sigmoid(gate) * up).astype(BF16)
        part = jnp.dot(act, w2_ref[c * FF_CHUNK:(c + 1) * FF_CHUNK, :], preferred_element_type=F32)
        y = part if y is None else y + part
    out_ref[...] = x + _rms(y, g2_ref[...])


def _ffn(x, gain_pre, w1_bf16, w2_bf16, gain_post):
    m = x.shape[0]
    tm = min(m, 512)
    resident = dict(pipeline_mode=pl.Buffered(1))
    return pl.pallas_call(
        _ffn_body,
        grid=(m // tm,),
        in_specs=[pl.BlockSpec((tm, D_MODEL), lambda i: (i, 0)),
                  pl.BlockSpec((1, D_MODEL), lambda i: (0, 0)),
                  pl.BlockSpec((D_MODEL, 2 * D_FF), lambda i: (0, 0), **resident),
                  pl.BlockSpec((D_FF, D_MODEL), lambda i: (0, 0), **resident),
                  pl.BlockSpec((1, D_MODEL), lambda i: (0, 0))],
        out_specs=pl.BlockSpec((tm, D_MODEL), lambda i: (i, 0)),
        out_shape=jax.ShapeDtypeStruct((m, D_MODEL), F32),
        compiler_params=_params("arbitrary"),
        name="ffn",
    )(x, gain_pre.reshape(1, D_MODEL), w1_bf16, w2_bf16, gain_post.reshape(1, D_MODEL))


def _split_heads(cols):
    return cols.reshape(cols.shape[0], cols.shape[1], C_HEADS, C_HEAD_DIM)


def _prompt_trunk(x3, p):
    bsz, t, _ = x3.shape
    x = x3.reshape(bsz * t, D_MODEL)
    depth = p["w_in"].shape[0]
    bufs, kvs = [], []
    for l in range(depth):
        h, kv_tail = _in_proj(x, p["norm_pre_mix"][l], p["w_in"], l, bsz)
        o_a, o_b, buf = _local_mixers(h, bsz, p["a_norm_g"][l], p["a_norm_b"][l], p["a_w_s"][l], p["a_b_s"][l],
                                      p["b_conv_w"][l], p["b_conv_b"][l], p["b_norm_g"][l], p["b_norm_b"][l])
        o_c = _attn_prompt(h, bsz)
        x = _merge_out(x, h, o_a, o_b, o_c, p["w_branch"][l], p["w_out"][l], p["norm_post_mix"][l])
        x = _ffn(x, p["norm_pre_ffn"][l], p["ffn_w_in"][l], p["ffn_w_out"][l], p["norm_post_ffn"][l])
        bufs.append(buf)
        kv3 = kv_tail.reshape(bsz, TAIL, KV_WIDTH)
        layer_kv = []
        for g, (w, _) in enumerate(C_CONFIGS):
            keep = min(w, t)
            for off in (g * C_GROUP_WIDTH, C_GROUPS * C_GROUP_WIDTH + g * C_GROUP_WIDTH):
                layer_kv.append(_split_heads(kv3[:, TAIL - keep:, off:off + C_GROUP_WIDTH]))
        kvs.append(layer_kv)
    new_kv = [jnp.stack([kvs[l][j] for l in range(depth)], axis=0) for j in range(2 * C_GROUPS)]
    return x.reshape(bsz, t, D_MODEL), jnp.stack(bufs, axis=0), new_kv


def _sample_trunk(x3, conv_state, caches, p):
    bsz, t, _ = x3.shape
    x = x3.reshape(bsz * t, D_MODEL)
    depth = p["w_in"].shape[0]
    bufs, vns, kvs = [], [], []
    for l in range(depth):
        h = _in_proj_sample(x, p["norm_pre_mix"][l], p["w_in"], l)
        o_a, vn = _token_mix(h, p["a_norm_g"][l], p["a_norm_b"][l], p["a_w_s"][l], p["a_b_s"][l], t, True)
        o_b, buf = _conv_sample(h, conv_state[l], p["b_conv_w"][l], p["b_conv_b"][l],
                                p["b_norm_g"][l], p["b_norm_b"][l])
        o_c = _attn_sample(h, caches, l)
        x = _merge_out(x, h, o_a, o_b, o_c, p["w_branch"][l], p["w_out"][l], p["norm_post_mix"][l])
        x = _ffn(x, p["norm_pre_ffn"][l], p["ffn_w_in"][l], p["ffn_w_out"][l], p["norm_post_ffn"][l])
        bufs.append(buf)
        vns.append(vn.reshape(bsz, t, A_WIDTH))
        h3 = h.reshape(bsz, t, D_IN)
        kvs.append([_split_heads(h3[:, :, off + g * C_GROUP_WIDTH:off + (g + 1) * C_GROUP_WIDTH])
                    for g in range(C_GROUPS) for off in (OFF_CK, OFF_CV)])
    new_kv = [jnp.stack([kvs[l][j] for l in range(depth)], axis=0) for j in range(2 * C_GROUPS)]
    return x.reshape(bsz, t, D_MODEL), jnp.stack(bufs, axis=0), jnp.stack(vns, axis=0), new_kv


def kernel(x_prompt, x_sample, state_b_conv, cache_c0_k, cache_c0_v, cache_c1_k, cache_c1_v, cache_c2_k,
           cache_c2_v, norm_pre_mix, norm_post_mix, norm_pre_ffn, norm_post_ffn, w_in, a_norm_g, a_norm_b,
           a_w_s, a_b_s, b_conv_w, b_conv_b, b_norm_g, b_norm_b, w_branch, w_out, ffn_w_in, ffn_w_out):
    assert x_sample.shape[1] == DEC_SEQ
    p = dict(norm_pre_mix=norm_pre_mix, norm_post_mix=norm_post_mix, norm_pre_ffn=norm_pre_ffn,
             norm_post_ffn=norm_post_ffn, w_in=w_in, a_norm_g=a_norm_g, a_norm_b=a_norm_b,
             a_w_s=a_w_s, a_b_s=a_b_s, b_conv_w=b_conv_w, b_conv_b=b_conv_b, b_norm_g=b_norm_g,
             b_norm_b=b_norm_b, w_branch=w_branch.astype(BF16), w_out=w_out.astype(BF16),
             ffn_w_in=ffn_w_in.astype(BF16), ffn_w_out=ffn_w_out.astype(BF16))
    caches = (cache_c0_k, cache_c0_v, cache_c1_k, cache_c1_v, cache_c2_k, cache_c2_v)
    y_prompt, buf_p, kv_p = _prompt_trunk(x_prompt, p)
    y_sample, buf_s, vn_s, kv_s = _sample_trunk(x_sample, state_b_conv, caches, p)
    return (y_prompt, y_sample, buf_p, buf_s, vn_s, *kv_p, *kv_s)
```

```python
import functools

import jax
import jax.numpy as jnp
from jax import lax
from jax.experimental import pallas as pl
from jax.experimental.pallas import tpu as pltpu

F32 = jnp.float32
BF16 = jnp.bfloat16

D_MODEL = 1024
DEC_SEQ = 8
A_WIDTH = 512
A_GROUPS = 4
A_GROUP_DIM = 128
CHUNK = 128
B_WIDTH = 512
CONV_W = 31
CONV_HALO = 32
C_HEADS = 8
C_HEAD_DIM = 64
C_GROUP_WIDTH = 512
C_CONFIGS = ((128, 1), (512, 4), (2048, 16))
C_GROUPS = 3
C_WINDOW = 128
N_BRANCH = 3
BRANCH_WIDTH = 512
D_FF = 2816
EPS = 1e-6
NEG_INF = -1e30
ATTN_SCALE = C_HEAD_DIM ** -0.5
REF_OFF_G = 6656
OFF_G = 0
OFF_AU = 3072
OFF_AV = 3584
OFF_B = 4096
OFF_CQ = 5120
OFF_CK = 6656
OFF_CV = 8192
D_IN = 9728
COL_BLOCKS = D_IN // 512

VMEM_LIMIT = 56 * 1024 * 1024


def _slope(g, h):
    n = C_GROUPS * C_HEADS
    return 2.0 ** (-8.0 * (h * C_GROUPS + g + 1.0) / n)


def _params(*sem):
    return pltpu.CompilerParams(dimension_semantics=sem, vmem_limit_bytes=VMEM_LIMIT)


def _rms(x, gain):
    return x * lax.rsqrt(jnp.mean(x * x, axis=-1, keepdims=True) + EPS) * gain


def _layer_norm(x, gain, bias):
    mu = jnp.mean(x, axis=-1, keepdims=True)
    xc = x - mu
    var = jnp.mean(xc * xc, axis=-1, keepdims=True)
    return xc * lax.rsqrt(var + EPS) * gain + bias


Q_TILE0 = OFF_CQ // 512
KV_TILE0 = OFF_CK // 512
KV_WIDTH = D_IN - OFF_CK
TAIL = 2048


def _in_proj_sample_body(x_ref, g_ref, w_ref, o_ref, xn_ref):
    @pl.when(pl.program_id(1) == 0)
    def _():
        xn_ref[...] = _rms(x_ref[...], g_ref[...]).astype(BF16)

    o_ref[...] = jnp.dot(xn_ref[...], w_ref[...].astype(BF16), preferred_element_type=F32)


def _w_in_tile(j):
    return lax.rem(j + REF_OFF_G // 512, COL_BLOCKS)


def _in_proj_sample(x, gain, w_in, layer):
    m = x.shape[0]
    tn = 512
    return pl.pallas_call(
        _in_proj_sample_body,
        grid=(1, D_IN // tn),
        in_specs=[pl.BlockSpec((m, D_MODEL), lambda i, j: (0, 0)),
                  pl.BlockSpec((1, D_MODEL), lambda i, j: (0, 0)),
                  pl.BlockSpec((None, D_MODEL, tn), lambda i, j: (layer, 0, _w_in_tile(j)))],
        out_specs=pl.BlockSpec((m, tn), lambda i, j: (0, j)),
        out_shape=jax.ShapeDtypeStruct((m, D_IN), F32),
        scratch_shapes=[pltpu.VMEM((m, D_MODEL), BF16)],
        compiler_params=_params("arbitrary", "arbitrary"),
        name="in_proj_sample",
    )(x, gain.reshape(1, D_MODEL), w_in)


def _in_proj_body(x_ref, g_ref, w_ref, o_ref, kv_ref, xn_ref, res_ref, tmp_ref, *, tm, tiles_per_seq):
    i = pl.program_id(0)
    j = pl.program_id(1)
    lanes = res_ref.shape[1]

    def matmul_tile(slot):
        res = jnp.dot(xn_ref[...], w_ref[...].astype(BF16), preferred_element_type=F32)
        for c in range(lanes):
            res_ref[slot, c] = res[:, c * 128:(c + 1) * 128]

    def emit(slot, d):
        if d == 1:
            for c in range(lanes):
                o_ref[:, c * 128:(c + 1) * 128] = res_ref[slot, c].astype(BF16)
            return
        if d == 16:
            quarter = tm // 4
            for b4 in range(4):
                for c in range(lanes):
                    tmp_ref[c, b4 * quarter:(b4 + 1) * quarter, :] = res_ref[slot, c, pl.ds(b4, quarter, stride=4), :]
            for r in range(d):
                for c in range(lanes):
                    o_ref[r * C_WINDOW:(r + 1) * C_WINDOW, c * 128:(c + 1) * 128] = (
                        tmp_ref[c, pl.ds((r % 4) * quarter + r // 4, C_WINDOW, stride=4), :].astype(BF16))
            return
        for u in range(tm // (C_WINDOW * d)):
            for r in range(d):
                dst = u * C_WINDOW * d + r * C_WINDOW
                for c in range(lanes):
                    o_ref[dst:dst + C_WINDOW, c * 128:(c + 1) * 128] = (
                        res_ref[slot, c, pl.ds(u * C_WINDOW * d + r, C_WINDOW, stride=d), :].astype(BF16))

    @pl.when(j == 0)
    def _():
        xn_ref[...] = _rms(x_ref[...], g_ref[...]).astype(BF16)
        matmul_tile(0)

    prev_group = lax.rem(jnp.maximum(j - 1 - Q_TILE0, 0), C_GROUPS)
    prev_dilation_group = jnp.where(j - 1 >= Q_TILE0, prev_group, 0)
    in_range = jnp.logical_and(j >= 1, j < COL_BLOCKS)
    for parity in range(2):
        for g, (_, d) in enumerate(C_CONFIGS):
            @pl.when(jnp.logical_and(jnp.logical_and(in_range, lax.rem(j, 2) == parity), prev_dilation_group == g))
            def _():
                emit(1 - parity, d)
                matmul_tile(parity)

    @pl.when(j == COL_BLOCKS)
    def _():
        emit((COL_BLOCKS - 1) % 2, C_CONFIGS[(COL_BLOCKS - 1 - Q_TILE0) % C_GROUPS][1])

    @pl.when(jnp.logical_and(i % tiles_per_seq == tiles_per_seq - 1,
                             jnp.logical_and(j >= KV_TILE0, j < COL_BLOCKS)))
    def _():
        for c in range(lanes):
            kv_ref[:, c * 128:(c + 1) * 128] = res_ref[lax.rem(j, 2), c]


def _in_proj(x, gain, w_in, layer, bsz):
    m = x.shape[0]
    t = m // bsz
    tm = TAIL
    tn = 512
    assert t % tm == 0
    tiles_per_seq = t // tm
    last = COL_BLOCKS - 1

    def kv_index(i, j):
        is_tail = i % tiles_per_seq == tiles_per_seq - 1
        return i // tiles_per_seq, jnp.where(is_tail, jnp.clip(j - KV_TILE0, 0, last - KV_TILE0), 0)

    return pl.pallas_call(
        functools.partial(_in_proj_body, tm=tm, tiles_per_seq=tiles_per_seq),
        grid=(m // tm, COL_BLOCKS + 1),
        in_specs=[pl.BlockSpec((tm, D_MODEL), lambda i, j: (i, 0)),
                  pl.BlockSpec((1, D_MODEL), lambda i, j: (0, 0)),
                  pl.BlockSpec((None, D_MODEL, tn), lambda i, j: (layer, 0, _w_in_tile(jnp.minimum(j, last))))],
        out_specs=[pl.BlockSpec((tm, tn), lambda i, j: (i, jnp.maximum(j - 1, 0))),
                   pl.BlockSpec((tm, tn), kv_index)],
        out_shape=[jax.ShapeDtypeStruct((m, D_IN), BF16),
                   jax.ShapeDtypeStruct((bsz * TAIL, KV_WIDTH), F32)],
        scratch_shapes=[pltpu.VMEM((tm, D_MODEL), BF16), pltpu.VMEM((2, tn // 128, tm, 128), F32),
                        pltpu.VMEM((tn // 128, tm, 128), F32)],
        compiler_params=_params("arbitrary", "arbitrary"),
        name="in_proj",
    )(x, gain.reshape(1, D_MODEL), w_in)


def _token_mix_body(u_ref, v_ref, ng_ref, nb_ref, w_ref, bias_ref, o_ref, *maybe_vn_ref, r, chunk_len, n_sub):
    row = lax.broadcasted_iota(jnp.int32, (r, r), 0)
    col = lax.broadcasted_iota(jnp.int32, (r, r), 1)
    shift = chunk_len.bit_length() - 1
    keep = ((row >> shift) == (col >> shift)) & (col <= row)
    wm = [jnp.where(keep, w_ref[g], 0.0).astype(BF16) for g in range(A_GROUPS)]
    for c in range(n_sub):
        rows = slice(c * r, (c + 1) * r)
        vn = _layer_norm(v_ref[rows, :].astype(F32), ng_ref[...], nb_ref[...])
        for vn_ref in maybe_vn_ref:
            vn_ref[rows, :] = vn
        vnb = vn.astype(BF16)
        for g in range(A_GROUPS):
            lanes = slice(g * A_GROUP_DIM, (g + 1) * A_GROUP_DIM)
            mix = jnp.dot(wm[g], vnb[:, lanes], preferred_element_type=F32) + bias_ref[:, lanes]
            o_ref[rows, lanes] = u_ref[rows, lanes].astype(F32) * mix


def _token_mix(h, norm_g, norm_b, w_s, b_s, chunk_len, emit_vn):
    m = h.shape[0]
    if chunk_len == CHUNK:
        r, n_sub = CHUNK, min(4, m // CHUNK)
        w = w_s
        bias_rows = b_s.T
    else:
        r, n_sub = m, 1
        reps = m // chunk_len
        w = jnp.tile(w_s[:, :chunk_len, :chunk_len], (1, reps, reps))
        bias_rows = jnp.tile(b_s[:, :chunk_len].T, (reps, 1))
    bias = jnp.repeat(bias_rows, A_GROUP_DIM, axis=1)
    tr = r * n_sub
    body = functools.partial(_token_mix_body, r=r, chunk_len=chunk_len, n_sub=n_sub)
    return pl.pallas_call(
        body,
        grid=(m // tr,),
        in_specs=[pl.BlockSpec((tr, A_WIDTH), lambda i: (i, OFF_AU // A_WIDTH)),
                  pl.BlockSpec((tr, A_WIDTH), lambda i: (i, OFF_AV // A_WIDTH)),
                  pl.BlockSpec((1, A_WIDTH), lambda i: (0, 0)),
                  pl.BlockSpec((1, A_WIDTH), lambda i: (0, 0)),
                  pl.BlockSpec((A_GROUPS, r, r), lambda i: (0, 0, 0)),
                  pl.BlockSpec((r, A_WIDTH), lambda i: (0, 0))],
        out_specs=[pl.BlockSpec((tr, A_WIDTH), lambda i: (i, 0))] * (2 if emit_vn else 1),
        out_shape=[jax.ShapeDtypeStruct((m, A_WIDTH), F32)] * (2 if emit_vn else 1),
        compiler_params=_params("arbitrary"),
        name="token_mix",
    )(h, h, norm_g.reshape(1, A_WIDTH), norm_b.reshape(1, A_WIDTH), w, bias)


def _glu(z):
    return z[:, :B_WIDTH] * jax.nn.sigmoid(z[:, B_WIDTH:])


def _local_mixers_body(u_ref, v_ref, ng_ref, nb_ref, w_ref, bias_ref, z_ref, zh_ref, cw_ref, cb_ref, lg_ref, lb_ref,
                       oa_ref, ob_ref, buf_ref, pad_ref, *, tr, tiles):
    row = lax.broadcasted_iota(jnp.int32, (CHUNK, CHUNK), 0)
    col = lax.broadcasted_iota(jnp.int32, (CHUNK, CHUNK), 1)
    wm = [jnp.where(col <= row, w_ref[g], 0.0).astype(BF16) for g in range(A_GROUPS)]
    for c in range(tr // CHUNK):
        rows = slice(c * CHUNK, (c + 1) * CHUNK)
        vnb = _layer_norm(v_ref[rows, :].astype(F32), ng_ref[...], nb_ref[...]).astype(BF16)
        for g in range(A_GROUPS):
            lanes = slice(g * A_GROUP_DIM, (g + 1) * A_GROUP_DIM)
            mix = jnp.dot(wm[g], vnb[:, lanes], preferred_element_type=F32) + bias_ref[:, lanes]
            oa_ref[rows, lanes] = (u_ref[rows, lanes].astype(F32) * mix).astype(oa_ref.dtype)

    t = pl.program_id(0) % tiles
    hg = _glu(z_ref[...].astype(F32))
    pad_ref[0:CONV_HALO, :] = jnp.where(t == 0, 0.0, _glu(zh_ref[...].astype(F32)))
    pad_ref[CONV_HALO:CONV_HALO + tr, :] = hg
    first = CONV_HALO - (CONV_W - 1)
    sub = 32
    taps = [[k for k in range(CONV_W) if (first + k) % 8 == s] for s in range(8)]
    for blk in range(tr // sub):
        acc = jnp.zeros((sub // 8, 8, B_WIDTH), F32)
        for s in range(8):
            q_max = max((first + k) // 8 for k in taps[s])
            win = pad_ref[blk * sub + s:blk * sub + s + sub + 8 * q_max, :]
            for k in taps[s]:
                q = (first + k) // 8
                acc = acc + win[8 * q:8 * q + sub, :].reshape(sub // 8, 8, B_WIDTH) * cw_ref[k]
        y = _layer_norm(acc.reshape(sub, B_WIDTH) + cb_ref[...], lg_ref[...], lb_ref[...])
        ob_ref[blk * sub:(blk + 1) * sub, :] = (y * jax.nn.sigmoid(y)).astype(ob_ref.dtype)

    @pl.when(t == tiles - 1)
    def _():
        buf_ref[...] = pad_ref[CONV_HALO + tr - (CONV_W - 1):CONV_HALO + tr, :]


def _local_mixers(h, bsz, norm_g, norm_b, w_s, b_s, conv_w, conv_b, ln_g, ln_b):
    m = h.shape[0]
    t = m // bsz
    tr = min(512, t)
    tiles = t // tr
    halo_per_tile = tr // CONV_HALO
    bias = jnp.repeat(b_s.T, A_GROUP_DIM, axis=1)
    vec = pl.BlockSpec((1, B_WIDTH), lambda i: (0, 0))
    body = functools.partial(_local_mixers_body, tr=tr, tiles=tiles)
    return pl.pallas_call(
        body,
        grid=(m // tr,),
        in_specs=[pl.BlockSpec((tr, A_WIDTH), lambda i: (i, OFF_AU // A_WIDTH)),
                  pl.BlockSpec((tr, A_WIDTH), lambda i: (i, OFF_AV // A_WIDTH)),
                  vec, vec,
                  pl.BlockSpec((A_GROUPS, CHUNK, CHUNK), lambda i: (0, 0, 0)),
                  pl.BlockSpec((CHUNK, A_WIDTH), lambda i: (0, 0)),
                  pl.BlockSpec((tr, 2 * B_WIDTH), lambda i: (i, OFF_B // (2 * B_WIDTH))),
                  pl.BlockSpec((CONV_HALO, 2 * B_WIDTH),
                               lambda i: (jnp.maximum(i * halo_per_tile - 1, 0), OFF_B // (2 * B_WIDTH))),
                  pl.BlockSpec((CONV_W, 8, B_WIDTH), lambda i: (0, 0, 0)),
                  vec, vec, vec],
        out_specs=[pl.BlockSpec((tr, A_WIDTH), lambda i: (i, 0)),
                   pl.BlockSpec((tr, B_WIDTH), lambda i: (i, 0)),
                   pl.BlockSpec((None, CONV_W - 1, B_WIDTH), lambda i: (i // tiles, 0, 0))],
        out_shape=[jax.ShapeDtypeStruct((m, A_WIDTH), BF16),
                   jax.ShapeDtypeStruct((m, B_WIDTH), BF16),
                   jax.ShapeDtypeStruct((bsz, CONV_W - 1, B_WIDTH), F32)],
        scratch_shapes=[pltpu.VMEM((CONV_HALO + tr, B_WIDTH), F32)],
        compiler_params=_params("arbitrary"),
        name="local_mixers",
    )(h, h, norm_g.reshape(1, A_WIDTH), norm_b.reshape(1, A_WIDTH), w_s, bias, h, h,
      jnp.broadcast_to(conv_w[:, None, :], (CONV_W, 8, B_WIDTH)), conv_b.reshape(1, B_WIDTH),
      ln_g.reshape(1, B_WIDTH), ln_b.reshape(1, B_WIDTH))


def _conv_sample_body(z_ref, st_ref, cw_ref, cb_ref, lg_ref, lb_ref, o_ref, buf_ref, pad_ref, *, bsz):
    hg = _glu(z_ref[...])
    pad_ref[:, 0:CONV_W - 1, :] = st_ref[...]
    pad_ref[:, CONV_W - 1:CONV_W - 1 + DEC_SEQ, :] = hg.reshape(bsz, DEC_SEQ, B_WIDTH)
    acc = jnp.zeros((bsz, DEC_SEQ, B_WIDTH), F32)
    for k in range(CONV_W):
        acc = acc + cw_ref[k:k + 1, :] * pad_ref[:, k:k + DEC_SEQ, :]
    y = _layer_norm(acc + cb_ref[...], lg_ref[...], lb_ref[...])
    o_ref[...] = (y * jax.nn.sigmoid(y)).reshape(bsz * DEC_SEQ, B_WIDTH)
    buf_ref[...] = pad_ref[:, DEC_SEQ:DEC_SEQ + CONV_W - 1, :]


def _conv_sample(h, state, conv_w, conv_b, ln_g, ln_b):
    m = h.shape[0]
    bsz = state.shape[0]
    body = functools.partial(_conv_sample_body, bsz=bsz)
    return pl.pallas_call(
        body,
        grid=(1,),
        in_specs=[pl.BlockSpec((m, 2 * B_WIDTH), lambda i: (0, OFF_B // (2 * B_WIDTH))),
                  pl.BlockSpec((bsz, CONV_W - 1, B_WIDTH), lambda i: (0, 0, 0)),
                  pl.BlockSpec((CONV_W, B_WIDTH), lambda i: (0, 0)),
                  pl.BlockSpec((1, B_WIDTH), lambda i: (0, 0)),
                  pl.BlockSpec((1, B_WIDTH), lambda i: (0, 0)),
                  pl.BlockSpec((1, B_WIDTH), lambda i: (0, 0))],
        out_specs=[pl.BlockSpec((m, B_WIDTH), lambda i: (0, 0)),
                   pl.BlockSpec((bsz, CONV_W - 1, B_WIDTH), lambda i: (0, 0, 0))],
        out_shape=[jax.ShapeDtypeStruct((m, B_WIDTH), F32),
                   jax.ShapeDtypeStruct((bsz, CONV_W - 1, B_WIDTH), F32)],
        scratch_shapes=[pltpu.VMEM((bsz, CONV_W - 1 + DEC_SEQ, B_WIDTH), F32)],
        compiler_params=_params("arbitrary"),
        name="conv_sample",
    )(h, state, conv_w, conv_b.reshape(1, B_WIDTH), ln_g.reshape(1, B_WIDTH), ln_b.reshape(1, B_WIDTH))


def _dot_nt(a, b):
    return lax.dot_general(a, b, (((1,), (1,)), ((), ())), preferred_element_type=F32)


ATTN_TILE = 2048
ATTN_ITEMS = 8


def _attn_prompt_body(slope_ref, *refs):
    group_refs = [refs[5 * g:5 * g + 5] for g in range(C_GROUPS)]
    o_ref, og_ref, lg_ref, sm_ref, sx_ref, pp_ref, pc_ref, mb_ref = refs[5 * C_GROUPS:]
    tq = C_WINDOW
    first_tile = pl.program_id(1) == 0
    hp = pl.program_id(2)
    row = lax.broadcasted_iota(jnp.int32, (tq, tq), 0)
    col = lax.broadcasted_iota(jnp.int32, (tq, tq), 1)
    upper = col > row
    diag = col == row
    diag0 = col == row + jnp.where(first_tile, tq, 0)
    valid0 = col <= row + jnp.where(first_tile, 0, tq)
    dist = jnp.where(upper, row - col + tq, row - col).astype(F32)
    low = lax.broadcasted_iota(jnp.int32, (tq, 128), 1) < C_HEAD_DIM
    sels = (low, jnp.logical_not(low))

    for g, (_, d) in enumerate(C_CONFIGS):
        q_ref, kp_ref, kc_ref, vp_ref, vc_ref = group_refs[g]
        slopes = [slope_ref[g * C_HEADS + 2 * hp + hh] * float(d) for hh in range(2)]
        bias = [-(s * dist) for s in slopes]
        bias_x = [-(s * float(tq)) for s in slopes]
        blocks = [(u, r) for u in range(ATTN_TILE // (tq * d)) for r in range(d)]

        def rows(u, r):
            return pl.ds(u * tq * d + r * tq, tq)

        def out_rows(u, r):
            return pl.ds(u * tq * d + r, tq, stride=d) if d > 1 else pl.ds(u * tq, tq)

        for first in range(0, len(blocks), ATTN_ITEMS):
            items = blocks[first:first + ATTN_ITEMS]
            for n, (u, r) in enumerate(items):
                q2 = q_ref[rows(u, r), :] * ATTN_SCALE
                kc2 = kc_ref[rows(u, r), :]
                kp2 = kp_ref[rows(0, r), :] if u == 0 else kc_ref[rows(u - 1, r), :]
                for hh in range(2):
                    qm = jnp.where(sels[hh], q2, 0.0)
                    sp = _dot_nt(qm, kp2)
                    sm = jnp.where(upper, sp, _dot_nt(qm, kc2)) + bias[hh]
                    if u == 0:
                        sm = jnp.where(valid0, sm, NEG_INF)
                    sm_ref[2 * n + hh] = sm
                    sx_ref[2 * n + hh] = jnp.where(diag0 if u == 0 else diag, sp + bias_x[hh], NEG_INF)
            for n in range(len(items)):
                mx = []
                for hh in range(2):
                    sm = sm_ref[2 * n + hh]
                    sx = sx_ref[2 * n + hh]
                    m = jnp.max(jnp.maximum(sm, sx), axis=-1, keepdims=True)
                    p = jnp.exp(sm - m)
                    pp_ref[2 * n + hh] = jnp.where(upper, p, jnp.exp(sx - m)).astype(BF16)
                    pc_ref[2 * n + hh] = jnp.where(upper, 0.0, p).astype(BF16)
                    mx.append(m)
                mb_ref[n] = jnp.where(low, mx[0], mx[1])
            for n, (u, r) in enumerate(items):
                vc2 = vc_ref[rows(u, r), :]
                vp2 = vp_ref[rows(0, r), :] if u == 0 else vc_ref[rows(u - 1, r), :]
                acc = []
                for hh in range(2):
                    vpx = jnp.where(sels[hh], vp2, 1.0)
                    vcx = jnp.where(sels[hh], vc2, 1.0)
                    acc.append(jnp.dot(pp_ref[2 * n + hh], vpx, preferred_element_type=F32)
                               + jnp.dot(pc_ref[2 * n + hh], vcx, preferred_element_type=F32))
                den = pltpu.roll(jnp.where(low, acc[1], acc[0]), C_HEAD_DIM, axis=1)
                og_ref[g, out_rows(u, r), :] = jnp.where(low, acc[0], acc[1]) / den
                lg_ref[g, out_rows(u, r), :] = mb_ref[n] + jnp.log(den)

    chunk = 256
    for c in range(ATTN_TILE // chunk):
        rs = slice(c * chunk, (c + 1) * chunk)
        l0, l1, l2 = lg_ref[0, rs, :], lg_ref[1, rs, :], lg_ref[2, rs, :]
        mx = jnp.maximum(jnp.maximum(l0, l1), l2)
        e0, e1, e2 = jnp.exp(l0 - mx), jnp.exp(l1 - mx), jnp.exp(l2 - mx)
        tot = e0 + e1 + e2
        merged = (e0 / tot) * og_ref[0, rs, :] + (e1 / tot) * og_ref[1, rs, :] + (e2 / tot) * og_ref[2, rs, :]
        o_ref[rs, :] = merged.astype(o_ref.dtype)


def _attn_prompt(h, bsz):
    m = h.shape[0]
    t = m // bsz
    assert t % ATTN_TILE == 0
    tiles = t // ATTN_TILE
    in_specs = [pl.BlockSpec(memory_space=pltpu.SMEM)]
    for g, (_, d) in enumerate(C_CONFIGS):
        unit_rows = C_WINDOW * d
        units = ATTN_TILE // unit_rows
        cq, ck, cv = ((off + g * C_GROUP_WIDTH) // 128 for off in (OFF_CQ, OFF_CK, OFF_CV))

        def cur(c):
            return pl.BlockSpec((ATTN_TILE, 128), lambda b, i, hp, c=c: (b * tiles + i, c + hp))

        def prev(c, units=units, unit_rows=unit_rows):
            return pl.BlockSpec((unit_rows, 128),
                                lambda b, i, hp, c=c: (jnp.maximum((b * tiles + i) * units - 1, 0), c + hp))

        in_specs += [cur(cq), prev(ck), cur(ck), prev(cv), cur(cv)]
    slopes = jnp.array([_slope(gg, hh) for gg in range(C_GROUPS) for hh in range(C_HEADS)], F32)
    n_items = 2 * ATTN_ITEMS
    return pl.pallas_call(
        _attn_prompt_body,
        grid=(bsz, tiles, C_HEADS // 2),
        in_specs=in_specs,
        out_specs=pl.BlockSpec((ATTN_TILE, 128), lambda b, i, hp: (b * tiles + i, hp)),
        out_shape=jax.ShapeDtypeStruct((m, C_GROUP_WIDTH), BF16),
        scratch_shapes=[pltpu.VMEM((C_GROUPS, ATTN_TILE, 128), F32)] * 2
        + [pltpu.VMEM((n_items, C_WINDOW, C_WINDOW), F32)] * 2
        + [pltpu.VMEM((n_items, C_WINDOW, C_WINDOW), BF16)] * 2
        + [pltpu.VMEM((n_items // 2, C_WINDOW, 128), F32)],
        compiler_params=_params("arbitrary", "arbitrary", "arbitrary"),
        name="attn_prompt",
    )(slopes, *([h] * (5 * C_GROUPS)))


def _attn_sample_body(qkv_ref, slope_ref, k0_ref, v0_ref, k1_ref, v1_ref, k2_ref, v2_ref, o_ref):
    caches = ((k0_ref, v0_ref), (k1_ref, v1_ref), (k2_ref, v2_ref))
    rows = C_HEADS * DEC_SEQ
    rid = lax.broadcasted_iota(jnp.int32, (rows, 1), 0)
    i_row = rid & (DEC_SEQ - 1)
    own_lanes = (lax.broadcasted_iota(jnp.int32, (rows, C_GROUP_WIDTH), 1) >> 6) == (
        lax.broadcasted_iota(jnp.int32, (rows, C_GROUP_WIDTH), 0) >> 3)
    outs, lses = [], []
    for g, (w, d) in enumerate(C_CONFIGS):
        kt_ref, vt_ref = caches[g]
        q = qkv_ref[:, g * 512:(g + 1) * 512] * ATTN_SCALE
        kn = qkv_ref[:, (3 + g) * 512:(4 + g) * 512]
        vn = qkv_ref[:, (6 + g) * 512:(7 + g) * 512]
        qbd = jnp.where(own_lanes, jnp.concatenate([q] * C_HEADS, axis=0), 0.0)
        slope = slope_ref[g, :, 0:1]
        kt = kt_ref[...].reshape(C_GROUP_WIDTH, w).astype(BF16)
        vt = vt_ref[...].reshape(C_GROUP_WIDTH, w).astype(BF16)
        pos = lax.broadcasted_iota(jnp.int32, (rows, w), 1)
        back = (w + i_row) - pos
        ok = (pos >= i_row) if d == 1 else ((back & (d - 1)) == 0) & (pos >= i_row)
        s_c = jnp.dot(qbd.astype(BF16), kt, preferred_element_type=F32)
        s_c = jnp.where(ok, s_c - slope * back.astype(F32), NEG_INF)
        s_n = []
        for i2 in range(DEC_SEQ):
            s = jnp.sum(qbd * kn[i2:i2 + 1, :], axis=-1, keepdims=True)
            back_n = i_row - i2
            ok_n = (back_n >= 0) if d == 1 else ((back_n & (d - 1)) == 0) & (back_n >= 0)
            s_n.append(jnp.where(ok_n, s - slope * back_n.astype(F32), NEG_INF))
        mx = jnp.max(s_c, axis=-1, keepdims=True)
        for s in s_n:
            mx = jnp.maximum(mx, s)
        p_c = jnp.exp(s_c - mx)
        den = jnp.sum(p_c, axis=-1, keepdims=True)
        acc = _dot_nt(p_c.astype(BF16), vt)
        for i2, s in enumerate(s_n):
            p = jnp.exp(s - mx)
            den = den + p
            acc = acc + p * vn[i2:i2 + 1, :]
        outs.append(acc / den)
        lses.append(mx + jnp.log(den))
    mx = jnp.maximum(jnp.maximum(lses[0], lses[1]), lses[2])
    e = [jnp.exp(l - mx) for l in lses]
    tot = e[0] + e[1] + e[2]
    merged = (e[0] / tot) * outs[0] + (e[1] / tot) * outs[1] + (e[2] / tot) * outs[2]
    merged = jnp.where(own_lanes, merged, 0.0).reshape(C_HEADS, DEC_SEQ, C_GROUP_WIDTH)
    o_ref[...] = jnp.sum(merged, axis=0)


def _attn_sample(h, caches, layer):
    m = h.shape[0]
    bsz = m // DEC_SEQ
    qkv = h[:, OFF_CQ:D_IN]
    slopes = jnp.array([[_slope(g, hh) for hh in range(C_HEADS)] for g in range(C_GROUPS)], F32)
    slopes = jnp.broadcast_to(jnp.repeat(slopes, DEC_SEQ, axis=1)[:, :, None], (C_GROUPS, C_HEADS * DEC_SEQ, 128))
    in_specs = [pl.BlockSpec((DEC_SEQ, D_IN - OFF_CQ), lambda b: (b, 0)),
                pl.BlockSpec((C_GROUPS, C_HEADS * DEC_SEQ, 128), lambda b: (0, 0, 0))]
    views = []
    for g, (w, _) in enumerate(C_CONFIGS):
        for j in range(2):
            c = caches[2 * g + j]
            assert c.shape[2] == w, "the cache must hold the last W_g positions"
            views.append(jnp.transpose(c, (0, 1, 3, 4, 2)))
            in_specs.append(pl.BlockSpec((None, None, C_HEADS, C_HEAD_DIM, w), lambda b: (layer, b, 0, 0, 0)))
    return pl.pallas_call(
        _attn_sample_body,
        grid=(bsz,),
        in_specs=in_specs,
        out_specs=pl.BlockSpec((DEC_SEQ, C_GROUP_WIDTH), lambda b: (b, 0)),
        out_shape=jax.ShapeDtypeStruct((m, C_GROUP_WIDTH), F32),
        compiler_params=_params("arbitrary"),
        name="attn_sample",
    )(qkv, slopes, *views)


def _merge_out_body(x_ref, oa_ref, ob_ref, oc_ref, gate_ref, wb_ref, wo_ref, g_ref, out_ref):
    merged = None
    for n, o_ref in enumerate((oa_ref, ob_ref, oc_ref)):
        proj = jnp.dot(o_ref[...].astype(BF16), wb_ref[n * BRANCH_WIDTH:(n + 1) * BRANCH_WIDTH, :],
                       preferred_element_type=F32)
        term = jax.nn.sigmoid(gate_ref[:, n * D_MODEL:(n + 1) * D_MODEL].astype(F32)) * proj
        merged = term if merged is None else merged + term
    y = jnp.dot(merged.astype(BF16), wo_ref[...], preferred_element_type=F32)
    out_ref[...] = x_ref[...] + _rms(y, g_ref[...])


def _merge_out(x, h, o_a, o_b, o_c, w_branch_bf16, w_out_bf16, gain):
    m = x.shape[0]
    tm = min(m, 512)
    row512 = pl.BlockSpec((tm, BRANCH_WIDTH), lambda i: (i, 0))
    return pl.pallas_call(
        _merge_out_body,
        grid=(m // tm,),
        in_specs=[pl.BlockSpec((tm, D_MODEL), lambda i: (i, 0)), row512, row512, row512,
                  pl.BlockSpec((tm, N_BRANCH * D_MODEL), lambda i: (i, OFF_G // (N_BRANCH * D_MODEL))),
                  pl.BlockSpec((N_BRANCH * BRANCH_WIDTH, D_MODEL), lambda i: (0, 0)),
                  pl.BlockSpec((D_MODEL, D_MODEL), lambda i: (0, 0)),
                  pl.BlockSpec((1, D_MODEL), lambda i: (0, 0))],
        out_specs=pl.BlockSpec((tm, D_MODEL), lambda i: (i, 0)),
        out_shape=jax.ShapeDtypeStruct((m, D_MODEL), F32),
        compiler_params=_params("arbitrary"),
        name="merge_out",
    )(x, o_a, o_b, o_c, h, w_branch_bf16, w_out_bf16, gain.reshape(1, D_MODEL))


FF_CHUNK = D_FF // 2


def _ffn_body(x_ref, g1_ref, w1_ref, w2_ref, g2_ref, out_ref):
    x = x_ref[...]
    xn = _rms(x, g1_ref[...]).astype(BF16)
    y = None
    for c in range(D_FF // FF_CHUNK):
        gate = jnp.dot(xn, w1_ref[:, c * FF_CHUNK:(c + 1) * FF_CHUNK], preferred_element_type=F32)
        up = jnp.dot(xn, w1_ref[:, D_FF + c * FF_CHUNK:D_FF + (c + 1) * FF_CHUNK], preferred_element_type=F32)
        act = (gate * jax.nn.sigmoid(gate) * up).astype(BF16)
        part = jnp.dot(act, w2_ref[c * FF_CHUNK:(c + 1) * FF_CHUNK, :], preferred_element_type=F32)
        y = part if y is None else y + part
    out_ref[...] = x + _rms(y, g2_ref[...])


def _ffn(x, gain_pre, w1_bf16, w2_bf16, gain_post):
    m = x.shape[0]
    tm = min(m, 512)
    resident = dict(pipeline_mode=pl.Buffered(1))
    return pl.pallas_call(
        _ffn_body,
        grid=(m // tm,),
        in_specs=[pl.BlockSpec((tm, D_MODEL), lambda i: (i, 0)),
                  pl.BlockSpec((1, D_MODEL), lambda i: (0, 0)),
                  pl.BlockSpec((D_MODEL, 2 * D_FF), lambda i: (0, 0), **resident),
                  pl.BlockSpec((D_FF, D_MODEL), lambda i: (0, 0), **resident),
                  pl.BlockSpec((1, D_MODEL), lambda i: (0, 0))],
        out_specs=pl.BlockSpec((tm, D_MODEL), lambda i: (i, 0)),
        out_shape=jax.ShapeDtypeStruct((m, D_MODEL), F32),
        compiler_params=_params("arbitrary"),
        name="ffn",
    )(x, gain_pre.reshape(1, D_MODEL), w1_bf16, w2_bf16, gain_post.reshape(1, D_MODEL))


def _split_heads(cols):
    return cols.reshape(cols.shape[0], cols.shape[1], C_HEADS, C_HEAD_DIM)


def _prompt_trunk(x3, p):
    bsz, t, _ = x3.shape
    x = x3.reshape(bsz * t, D_MODEL)
    depth = p["w_in"].shape[0]
    bufs, kvs = [], []
    for l in range(depth):
        h, kv_tail = _in_proj(x, p["norm_pre_mix"][l], p["w_in"], l, bsz)
        o_a, o_b, buf = _local_mixers(h, bsz, p["a_norm_g"][l], p["a_norm_b"][l], p["a_w_s"][l], p["a_b_s"][l],
                                      p["b_conv_w"][l], p["b_conv_b"][l], p["b_norm_g"][l], p["b_norm_b"][l])
        o_c = _attn_prompt(h, bsz)
        x = _merge_out(x, h, o_a, o_b, o_c, p["w_branch"][l], p["w_out"][l], p["norm_post_mix"][l])
        x = _ffn(x, p["norm_pre_ffn"][l], p["ffn_w_in"][l], p["ffn_w_out"][l], p["norm_post_ffn"][l])
        bufs.append(buf)
        kv3 = kv_tail.reshape(bsz, TAIL, KV_WIDTH)
        layer_kv = []
        for g, (w, _) in enumerate(C_CONFIGS):
            keep = min(w, t)
            for off in (g * C_GROUP_WIDTH, C_GROUPS * C_GROUP_WIDTH + g * C_GROUP_WIDTH):
                layer_kv.append(_split_heads(kv3[:, TAIL - keep:, off:off + C_GROUP_WIDTH]))
        kvs.append(layer_kv)
    new_kv = [jnp.stack([kvs[l][j] for l in range(depth)], axis=0) for j in range(2 * C_GROUPS)]
    return x.reshape(bsz, t, D_MODEL), jnp.stack(bufs, axis=0), new_kv


def _sample_trunk(x3, conv_state, caches, p):
    bsz, t, _ = x3.shape
    x = x3.reshape(bsz * t, D_MODEL)
    depth = p["w_in"].shape[0]
    bufs, vns, kvs = [], [], []
    for l in range(depth):
        h = _in_proj_sample(x, p["norm_pre_mix"][l], p["w_in"], l)
        o_a, vn = _token_mix(h, p["a_norm_g"][l], p["a_norm_b"][l], p["a_w_s"][l], p["a_b_s"][l], t, True)
        o_b, buf = _conv_sample(h, conv_state[l], p["b_conv_w"][l], p["b_conv_b"][l],
                                p["b_norm_g"][l], p["b_norm_b"][l])
        o_c = _attn_sample(h, caches, l)
        x = _merge_out(x, h, o_a, o_b, o_c, p["w_branch"][l], p["w_out"][l], p["norm_post_mix"][l])
        x = _ffn(x, p["norm_pre_ffn"][l], p["ffn_w_in"][l], p["ffn_w_out"][l], p["norm_post_ffn"][l])
        bufs.append(buf)
        vns.append(vn.reshape(bsz, t, A_WIDTH))
        h3 = h.reshape(bsz, t, D_IN)
        kvs.append([_split_heads(h3[:, :, off + g * C_GROUP_WIDTH:off + (g + 1) * C_GROUP_WIDTH])
                    for g in range(C_GROUPS) for off in (OFF_CK, OFF_CV)])
    new_kv = [jnp.stack([kvs[l][j] for l in range(depth)], axis=0) for j in range(2 * C_GROUPS)]
    return x.reshape(bsz, t, D_MODEL), jnp.stack(bufs, axis=0), jnp.stack(vns, axis=0), new_kv


def kernel(x_prompt, x_sample, state_b_conv, cache_c0_k, cache_c0_v, cache_c1_k, cache_c1_v, cache_c2_k,
           cache_c2_v, norm_pre_mix, norm_post_mix, norm_pre_ffn, norm_post_ffn, w_in, a_norm_g, a_norm_b,
           a_w_s, a_b_s, b_conv_w, b_conv_b, b_norm_g, b_norm_b, w_branch, w_out, ffn_w_in, ffn_w_out):
    assert x_sample.shape[1] == DEC_SEQ
    p = dict(norm_pre_mix=norm_pre_mix, norm_post_mix=norm_post_mix, norm_pre_ffn=norm_pre_ffn,
             norm_post_ffn=norm_post_ffn, w_in=w_in, a_norm_g=a_norm_g, a_norm_b=a_norm_b,
             a_w_s=a_w_s, a_b_s=a_b_s, b_conv_w=b_conv_w, b_conv_b=b_conv_b, b_norm_g=b_norm_g,
             b_norm_b=b_norm_b, w_branch=w_branch.astype(BF16), w_out=w_out.astype(BF16),
             ffn_w_in=ffn_w_in.astype(BF16), ffn_w_out=ffn_w_out.astype(BF16))
    caches = (cache_c0_k, cache_c0_v, cache_c1_k, cache_c1_v, cache_c2_k, cache_c2_v)
    y_prompt, buf_p, kv_p = _prompt_trunk(x_prompt, p)
    y_sample, buf_s, vn_s, kv_s = _sample_trunk(x_sample, state_b_conv, caches, p)
    return (y_prompt, y_sample, buf_p, buf_s, vn_s, *kv_p, *kv_s)
```

```python
import functools

import jax
import jax.numpy as jnp
from jax import lax
from jax.experimental import pallas as pl
from jax.experimental.pallas import tpu as pltpu

F32 = jnp.float32
BF16 = jnp.bfloat16

D_MODEL = 1024
DEC_SEQ = 8
A_WIDTH = 512
A_GROUPS = 4
A_GROUP_DIM = 128
CHUNK = 128
B_WIDTH = 512
CONV_W = 31
CONV_HALO = 32
C_HEADS = 8
C_HEAD_DIM = 64
C_GROUP_WIDTH = 512
C_CONFIGS = ((128, 1), (512, 4), (2048, 16))
C_GROUPS = 3
C_WINDOW = 128
N_BRANCH = 3
BRANCH_WIDTH = 512
D_FF = 2816
EPS = 1e-6
NEG_INF = -1e30
ATTN_SCALE = C_HEAD_DIM ** -0.5
REF_OFF_G = 6656
OFF_G = 0
OFF_AU = 3072
OFF_AV = 3584
OFF_B = 4096
OFF_CQ = 5120
OFF_CK = 6656
OFF_CV = 8192
D_IN = 9728
COL_BLOCKS = D_IN // 512

VMEM_LIMIT = 56 * 1024 * 1024


def _slope(g, h):
    n = C_GROUPS * C_HEADS
    return 2.0 ** (-8.0 * (h * C_GROUPS + g + 1.0) / n)


def _params(*sem):
    return pltpu.CompilerParams(dimension_semantics=sem, vmem_limit_bytes=VMEM_LIMIT)


def _rms(x, gain):
    return x * lax.rsqrt(jnp.mean(x * x, axis=-1, keepdims=True) + EPS) * gain


def _layer_norm(x, gain, bias):
    mu = jnp.mean(x, axis=-1, keepdims=True)
    xc = x - mu
    var = jnp.mean(xc * xc, axis=-1, keepdims=True)
    return xc * lax.rsqrt(var + EPS) * gain + bias


Q_TILE0 = OFF_CQ // 512
KV_TILE0 = OFF_CK // 512
KV_WIDTH = D_IN - OFF_CK
TAIL = 2048


def _in_proj_sample_body(x_ref, g_ref, w_ref, o_ref, xn_ref):
    @pl.when(pl.program_id(1) == 0)
    def _():
        xn_ref[...] = _rms(x_ref[...], g_ref[...]).astype(BF16)

    o_ref[...] = jnp.dot(xn_ref[...], w_ref[...].astype(BF16), preferred_element_type=F32)


def _w_in_tile(j):
    return lax.rem(j + REF_OFF_G // 512, COL_BLOCKS)


def _in_proj_sample(x, gain, w_in, layer):
    m = x.shape[0]
    tn = 512
    return pl.pallas_call(
        _in_proj_sample_body,
        grid=(1, D_IN // tn),
        in_specs=[pl.BlockSpec((m, D_MODEL), lambda i, j: (0, 0)),
                  pl.BlockSpec((1, D_MODEL), lambda i, j: (0, 0)),
                  pl.BlockSpec((None, D_MODEL, tn), lambda i, j: (layer, 0, _w_in_tile(j)))],
        out_specs=pl.BlockSpec((m, tn), lambda i, j: (0, j)),
        out_shape=jax.ShapeDtypeStruct((m, D_IN), F32),
        scratch_shapes=[pltpu.VMEM((m, D_MODEL), BF16)],
        compiler_params=_params("arbitrary", "arbitrary"),
        name="in_proj_sample",
    )(x, gain.reshape(1, D_MODEL), w_in)


def _in_proj_body(x_ref, g_ref, w_ref, o_ref, kv_ref, xn_ref, res_ref, tmp_ref, *, tm, tiles_per_seq):
    i = pl.program_id(0)
    j = pl.program_id(1)
    lanes = res_ref.shape[1]

    def matmul_tile(slot):
        res = jnp.dot(xn_ref[...], w_ref[...].astype(BF16), preferred_element_type=F32)
        for c in range(lanes):
            res_ref[slot, c] = res[:, c * 128:(c + 1) * 128]

    def emit(slot, d):
        if d == 1:
            for c in range(lanes):
                o_ref[:, c * 128:(c + 1) * 128] = res_ref[slot, c].astype(BF16)
            return
        if d == 16:
            quarter = tm // 4
            for b4 in range(4):
                for c in range(lanes):
                    tmp_ref[c, b4 * quarter:(b4 + 1) * quarter, :] = res_ref[slot, c, pl.ds(b4, quarter, stride=4), :]
            for r in range(d):
                for c in range(lanes):
                    o_ref[r * C_WINDOW:(r + 1) * C_WINDOW, c * 128:(c + 1) * 128] = (
                        tmp_ref[c, pl.ds((r % 4) * quarter + r // 4, C_WINDOW, stride=4), :].astype(BF16))
            return
        for u in range(tm // (C_WINDOW * d)):
            for r in range(d):
                dst = u * C_WINDOW * d + r * C_WINDOW
                for c in range(lanes):
                    o_ref[dst:dst + C_WINDOW, c * 128:(c + 1) * 128] = (
                        res_ref[slot, c, pl.ds(u * C_WINDOW * d + r, C_WINDOW, stride=d), :].astype(BF16))

    @pl.when(j == 0)
    def _():
        xn_ref[...] = _rms(x_ref[...], g_ref[...]).astype(BF16)
        matmul_tile(0)

    prev_group = lax.rem(jnp.maximum(j - 1 - Q_TILE0, 0), C_GROUPS)
    prev_dilation_group = jnp.where(j - 1 >= Q_TILE0, prev_group, 0)
    in_range = jnp.logical_and(j >= 1, j < COL_BLOCKS)
    for parity in range(2):
        for g, (_, d) in enumerate(C_CONFIGS):
            @pl.when(jnp.logical_and(jnp.logical_and(in_range, lax.rem(j, 2) == parity), prev_dilation_group == g))
            def _():
                emit(1 - parity, d)
                matmul_tile(parity)

    @pl.when(j == COL_BLOCKS)
    def _():
        emit((COL_BLOCKS - 1) % 2, C_CONFIGS[(COL_BLOCKS - 1 - Q_TILE0) % C_GROUPS][1])

    @pl.when(jnp.logical_and(i % tiles_per_seq == tiles_per_seq - 1,
                             jnp.logical_and(j >= KV_TILE0, j < COL_BLOCKS)))
    def _():
        for c in range(lanes):
            kv_ref[:, c * 128:(c + 1) * 128] = res_ref[lax.rem(j, 2), c]


def _in_proj(x, gain, w_in, layer, bsz):
    m = x.shape[0]
    t = m // bsz
    tm = TAIL
    tn = 512
    assert t % tm == 0
    tiles_per_seq = t // tm
    last = COL_BLOCKS - 1

    def kv_index(i, j):
        is_tail = i % tiles_per_seq == tiles_per_seq - 1
        return i // tiles_per_seq, jnp.where(is_tail, jnp.clip(j - KV_TILE0, 0, last - KV_TILE0), 0)

    return pl.pallas_call(
        functools.partial(_in_proj_body, tm=tm, tiles_per_seq=tiles_per_seq),
        grid=(m // tm, COL_BLOCKS + 1),
        in_specs=[pl.BlockSpec((tm, D_MODEL), lambda i, j: (i, 0)),
                  pl.BlockSpec((1, D_MODEL), lambda i, j: (0, 0)),
                  pl.BlockSpec((None, D_MODEL, tn), lambda i, j: (layer, 0, _w_in_tile(jnp.minimum(j, last))))],
        out_specs=[pl.BlockSpec((tm, tn), lambda i, j: (i, jnp.maximum(j - 1, 0))),
                   pl.BlockSpec((tm, tn), kv_index)],
        out_shape=[jax.ShapeDtypeStruct((m, D_IN), BF16),
                   jax.ShapeDtypeStruct((bsz * TAIL, KV_WIDTH), F32)],
        scratch_shapes=[pltpu.VMEM((tm, D_MODEL), BF16), pltpu.VMEM((2, tn // 128, tm, 128), F32),
                        pltpu.VMEM((tn // 128, tm, 128), F32)],
        compiler_params=_params("arbitrary", "arbitrary"),
        name="in_proj",
    )(x, gain.reshape(1, D_MODEL), w_in)


def _token_mix_sample_body(u_ref, v_ref, ng_ref, nb_ref, w_ref, bias_ref, o_ref, vn_ref, *, bsz):
    vn = _layer_norm(v_ref[...], ng_ref[...], nb_ref[...])
    vn_ref[...] = vn
    vn3 = vn.reshape(bsz, DEC_SEQ, A_WIDTH)
    t_idx = lax.broadcasted_iota(jnp.int32, (DEC_SEQ, A_WIDTH), 0)
    mix = jnp.broadcast_to(bias_ref[...], (bsz, DEC_SEQ, A_WIDTH))
    for s in range(DEC_SEQ):
        w_s = jnp.where(t_idx >= s, w_ref[s], 0.0)
        mix = mix + jnp.broadcast_to(vn3[:, s:s + 1, :], (bsz, DEC_SEQ, A_WIDTH)) * w_s
    o_ref[...] = u_ref[...] * mix.reshape(bsz * DEC_SEQ, A_WIDTH)


def _token_mix_sample(h, norm_g, norm_b, w_s, b_s):
    m = h.shape[0]
    bsz = m // DEC_SEQ
    w = jnp.repeat(jnp.transpose(w_s[:, :DEC_SEQ, :DEC_SEQ], (2, 1, 0)), A_GROUP_DIM, axis=2)
    bias = jnp.repeat(b_s[:, :DEC_SEQ].T, A_GROUP_DIM, axis=1)
    vec = pl.BlockSpec((1, A_WIDTH), lambda i: (0, 0))
    return pl.pallas_call(
        functools.partial(_token_mix_sample_body, bsz=bsz),
        grid=(1,),
        in_specs=[pl.BlockSpec((m, A_WIDTH), lambda i: (0, OFF_AU // A_WIDTH)),
                  pl.BlockSpec((m, A_WIDTH), lambda i: (0, OFF_AV // A_WIDTH)),
                  vec, vec,
                  pl.BlockSpec((DEC_SEQ, DEC_SEQ, A_WIDTH), lambda i: (0, 0, 0)),
                  pl.BlockSpec((DEC_SEQ, A_WIDTH), lambda i: (0, 0))],
        out_specs=[pl.BlockSpec((m, A_WIDTH), lambda i: (0, 0))] * 2,
        out_shape=[jax.ShapeDtypeStruct((m, A_WIDTH), F32)] * 2,
        compiler_params=_params("arbitrary"),
        name="token_mix_sample",
    )(h, h, norm_g.reshape(1, A_WIDTH), norm_b.reshape(1, A_WIDTH), w, bias)


def _glu(z):
    return z[:, :B_WIDTH] * jax.nn.sigmoid(z[:, B_WIDTH:])


def _local_mixers_body(u_ref, v_ref, ng_ref, nb_ref, w_ref, bias_ref, z_ref, zh_ref, cw_ref, cb_ref, lg_ref, lb_ref,
                       oa_ref, ob_ref, buf_ref, pad_ref, cw8_ref, *, tr, tiles):
    row = lax.broadcasted_iota(jnp.int32, (CHUNK, CHUNK), 0)
    col = lax.broadcasted_iota(jnp.int32, (CHUNK, CHUNK), 1)
    wm = [jnp.where(col <= row, w_ref[g], 0.0).astype(BF16) for g in range(A_GROUPS)]
    for c in range(tr // CHUNK):
        rows = slice(c * CHUNK, (c + 1) * CHUNK)
        vnb = _layer_norm(v_ref[rows, :].astype(F32), ng_ref[...], nb_ref[...]).astype(BF16)
        for g in range(A_GROUPS):
            lanes = slice(g * A_GROUP_DIM, (g + 1) * A_GROUP_DIM)
            mix = jnp.dot(wm[g], vnb[:, lanes], preferred_element_type=F32) + bias_ref[:, lanes]
            oa_ref[rows, lanes] = (u_ref[rows, lanes].astype(F32) * mix).astype(oa_ref.dtype)

    @pl.when(pl.program_id(0) == 0)
    def _():
        for k in range(CONV_W):
            cw8_ref[k] = jnp.broadcast_to(cw_ref[k:k + 1, :], (8, B_WIDTH))

    t = pl.program_id(0) % tiles
    hg = _glu(z_ref[...].astype(F32))
    pad_ref[0:CONV_HALO, :] = jnp.where(t == 0, 0.0, _glu(zh_ref[...].astype(F32)))
    pad_ref[CONV_HALO:CONV_HALO + tr, :] = hg
    first = CONV_HALO - (CONV_W - 1)
    sub = 32
    taps = [[k for k in range(CONV_W) if (first + k) % 8 == s] for s in range(8)]
    for blk in range(tr // sub):
        acc = jnp.zeros((sub // 8, 8, B_WIDTH), F32)
        for s in range(8):
            q_max = max((first + k) // 8 for k in taps[s])
            win = pad_ref[blk * sub + s:blk * sub + s + sub + 8 * q_max, :]
            for k in taps[s]:
                q = (first + k) // 8
                acc = acc + win[8 * q:8 * q + sub, :].reshape(sub // 8, 8, B_WIDTH) * cw8_ref[k]
        y = _layer_norm(acc.reshape(sub, B_WIDTH) + cb_ref[...], lg_ref[...], lb_ref[...])
        ob_ref[blk * sub:(blk + 1) * sub, :] = (y * jax.nn.sigmoid(y)).astype(ob_ref.dtype)

    @pl.when(t == tiles - 1)
    def _():
        buf_ref[...] = pad_ref[CONV_HALO + tr - (CONV_W - 1):CONV_HALO + tr, :]


def _local_mixers(h, bsz, norm_g, norm_b, w_s, b_s, conv_w, conv_b, ln_g, ln_b):
    m = h.shape[0]
    t = m // bsz
    tr = min(512, t)
    tiles = t // tr
    halo_per_tile = tr // CONV_HALO
    bias = jnp.repeat(b_s.T, A_GROUP_DIM, axis=1)
    vec = pl.BlockSpec((1, B_WIDTH), lambda i: (0, 0))
    body = functools.partial(_local_mixers_body, tr=tr, tiles=tiles)
    return pl.pallas_call(
        body,
        grid=(m // tr,),
        in_specs=[pl.BlockSpec((tr, A_WIDTH), lambda i: (i, OFF_AU // A_WIDTH)),
                  pl.BlockSpec((tr, A_WIDTH), lambda i: (i, OFF_AV // A_WIDTH)),
                  vec, vec,
                  pl.BlockSpec((A_GROUPS, CHUNK, CHUNK), lambda i: (0, 0, 0)),
                  pl.BlockSpec((CHUNK, A_WIDTH), lambda i: (0, 0)),
                  pl.BlockSpec((tr, 2 * B_WIDTH), lambda i: (i, OFF_B // (2 * B_WIDTH))),
                  pl.BlockSpec((CONV_HALO, 2 * B_WIDTH),
                               lambda i: (jnp.maximum(i * halo_per_tile - 1, 0), OFF_B // (2 * B_WIDTH))),
                  pl.BlockSpec((CONV_W, B_WIDTH), lambda i: (0, 0)),
                  vec, vec, vec],
        out_specs=[pl.BlockSpec((tr, A_WIDTH), lambda i: (i, 0)),
                   pl.BlockSpec((tr, B_WIDTH), lambda i: (i, 0)),
                   pl.BlockSpec((None, CONV_W - 1, B_WIDTH), lambda i: (i // tiles, 0, 0))],
        out_shape=[jax.ShapeDtypeStruct((m, A_WIDTH), BF16),
                   jax.ShapeDtypeStruct((m, B_WIDTH), BF16),
                   jax.ShapeDtypeStruct((bsz, CONV_W - 1, B_WIDTH), F32)],
        scratch_shapes=[pltpu.VMEM((CONV_HALO + tr, B_WIDTH), F32), pltpu.VMEM((CONV_W, 8, B_WIDTH), F32)],
        compiler_params=_params("arbitrary"),
        name="local_mixers",
    )(h, h, norm_g.reshape(1, A_WIDTH), norm_b.reshape(1, A_WIDTH), w_s, bias, h, h, conv_w, conv_b.reshape(1, B_WIDTH),
      ln_g.reshape(1, B_WIDTH), ln_b.reshape(1, B_WIDTH))


def _conv_sample_body(z_ref, st_ref, cw_ref, cb_ref, lg_ref, lb_ref, o_ref, buf_ref, pad_ref, *, bsz):
    hg = _glu(z_ref[...])
    pad_ref[:, 0:CONV_W - 1, :] = st_ref[...]
    pad_ref[:, CONV_W - 1:CONV_W - 1 + DEC_SEQ, :] = hg.reshape(bsz, DEC_SEQ, B_WIDTH)
    acc = jnp.zeros((bsz, DEC_SEQ, B_WIDTH), F32)
    for k in range(CONV_W):
        acc = acc + cw_ref[k:k + 1, :] * pad_ref[:, k:k + DEC_SEQ, :]
    y = _layer_norm(acc + cb_ref[...], lg_ref[...], lb_ref[...])
    o_ref[...] = (y * jax.nn.sigmoid(y)).reshape(bsz * DEC_SEQ, B_WIDTH)
    buf_ref[...] = pad_ref[:, DEC_SEQ:DEC_SEQ + CONV_W - 1, :]


def _conv_sample(h, state, conv_w, conv_b, ln_g, ln_b):
    m = h.shape[0]
    bsz = state.shape[0]
    body = functools.partial(_conv_sample_body, bsz=bsz)
    return pl.pallas_call(
        body,
        grid=(1,),
        in_specs=[pl.BlockSpec((m, 2 * B_WIDTH), lambda i: (0, OFF_B // (2 * B_WIDTH))),
                  pl.BlockSpec((bsz, CONV_W - 1, B_WIDTH), lambda i: (0, 0, 0)),
                  pl.BlockSpec((CONV_W, B_WIDTH), lambda i: (0, 0)),
                  pl.BlockSpec((1, B_WIDTH), lambda i: (0, 0)),
                  pl.BlockSpec((1, B_WIDTH), lambda i: (0, 0)),
                  pl.BlockSpec((1, B_WIDTH), lambda i: (0, 0))],
        out_specs=[pl.BlockSpec((m, B_WIDTH), lambda i: (0, 0)),
                   pl.BlockSpec((bsz, CONV_W - 1, B_WIDTH), lambda i: (0, 0, 0))],
        out_shape=[jax.ShapeDtypeStruct((m, B_WIDTH), F32),
                   jax.ShapeDtypeStruct((bsz, CONV_W - 1, B_WIDTH), F32)],
        scratch_shapes=[pltpu.VMEM((bsz, CONV_W - 1 + DEC_SEQ, B_WIDTH), F32)],
        compiler_params=_params("arbitrary"),
        name="conv_sample",
    )(h, state, conv_w, conv_b.reshape(1, B_WIDTH), ln_g.reshape(1, B_WIDTH), ln_b.reshape(1, B_WIDTH))


def _dot_nt(a, b):
    return lax.dot_general(a, b, (((1,), (1,)), ((), ())), preferred_element_type=F32)


ATTN_TILE = 2048
ATTN_ITEMS = 8


def _attn_prompt_body(slope_ref, *refs):
    group_refs = [refs[5 * g:5 * g + 5] for g in range(C_GROUPS)]
    o_ref, og_ref, lg_ref, sm_ref, sx_ref, pp_ref, pc_ref, mb_ref = refs[5 * C_GROUPS:]
    tq = C_WINDOW
    first_tile = pl.program_id(1) == 0
    hp = pl.program_id(2)
    row = lax.broadcasted_iota(jnp.int32, (tq, tq), 0)
    col = lax.broadcasted_iota(jnp.int32, (tq, tq), 1)
    upper = col > row
    diag = col == row
    diag0 = col == row + jnp.where(first_tile, tq, 0)
    valid0 = col <= row + jnp.where(first_tile, 0, tq)
    dist = jnp.where(upper, row - col + tq, row - col).astype(F32)
    low = lax.broadcasted_iota(jnp.int32, (tq, 128), 1) < C_HEAD_DIM
    sels = (low, jnp.logical_not(low))

    for g, (_, d) in enumerate(C_CONFIGS):
        q_ref, kp_ref, kc_ref, vp_ref, vc_ref = group_refs[g]
        slopes = [slope_ref[g * C_HEADS + 2 * hp + hh] * float(d) for hh in range(2)]
        bias = [-(s * dist) for s in slopes]
        bias_x = [-(s * float(tq)) for s in slopes]
        blocks = [(u, r) for u in range(ATTN_TILE // (tq * d)) for r in range(d)]

        def rows(u, r):
            return pl.ds(u * tq * d + r * tq, tq)

        def out_rows(u, r):
            return pl.ds(u * tq * d + r, tq, stride=d) if d > 1 else pl.ds(u * tq, tq)

        for first in range(0, len(blocks), ATTN_ITEMS):
            items = blocks[first:first + ATTN_ITEMS]
            for n, (u, r) in enumerate(items):
                q2 = q_ref[rows(u, r), :] * ATTN_SCALE
                kc2 = kc_ref[rows(u, r), :]
                kp2 = kp_ref[rows(0, r), :] if u == 0 else kc_ref[rows(u - 1, r), :]
                for hh in range(2):
                    qm = jnp.where(sels[hh], q2, 0.0)
                    sp = _dot_nt(qm, kp2)
                    sm = jnp.where(upper, sp, _dot_nt(qm, kc2)) + bias[hh]
                    if u == 0:
                        sm = jnp.where(valid0, sm, NEG_INF)
                    sm_ref[2 * n + hh] = sm
                    sx_ref[2 * n + hh] = jnp.where(diag0 if u == 0 else diag, sp + bias_x[hh], NEG_INF)
            for n in range(len(items)):
                mx = []
                for hh in range(2):
                    sm = sm_ref[2 * n + hh]
                    sx = sx_ref[2 * n + hh]
                    m = jnp.max(jnp.maximum(sm, sx), axis=-1, keepdims=True)
                    p = jnp.exp(sm - m)
                    pp_ref[2 * n + hh] = jnp.where(upper, p, jnp.exp(sx - m)).astype(BF16)
                    pc_ref[2 * n + hh] = jnp.where(upper, 0.0, p).astype(BF16)
                    mx.append(m)
                mb_ref[n] = jnp.where(low, mx[0], mx[1])
            for n, (u, r) in enumerate(items):
                vc2 = vc_ref[rows(u, r), :]
                vp2 = vp_ref[rows(0, r), :] if u == 0 else vc_ref[rows(u - 1, r), :]
                acc = []
                for hh in range(2):
                    vpx = jnp.where(sels[hh], vp2, 1.0)
                    vcx = jnp.where(sels[hh], vc2, 1.0)
                    acc.append(jnp.dot(pp_ref[2 * n + hh], vpx, preferred_element_type=F32)
                               + jnp.dot(pc_ref[2 * n + hh], vcx, preferred_element_type=F32))
                den = pltpu.roll(jnp.where(low, acc[1], acc[0]), C_HEAD_DIM, axis=1)
                og_ref[g, out_rows(u, r), :] = jnp.where(low, acc[0], acc[1]) / den
                lg_ref[g, out_rows(u, r), :] = mb_ref[n] + jnp.log(den)

    chunk = 256
    for c in range(ATTN_TILE // chunk):
        rs = slice(c * chunk, (c + 1) * chunk)
        l0, l1, l2 = lg_ref[0, rs, :], lg_ref[1, rs, :], lg_ref[2, rs, :]
        mx = jnp.maximum(jnp.maximum(l0, l1), l2)
        e0, e1, e2 = jnp.exp(l0 - mx), jnp.exp(l1 - mx), jnp.exp(l2 - mx)
        tot = e0 + e1 + e2
        merged = (e0 / tot) * og_ref[0, rs, :] + (e1 / tot) * og_ref[1, rs, :] + (e2 / tot) * og_ref[2, rs, :]
        o_ref[rs, :] = merged.astype(o_ref.dtype)


def _attn_prompt(h, bsz):
    m = h.shape[0]
    t = m // bsz
    assert t % ATTN_TILE == 0
    tiles = t // ATTN_TILE
    in_specs = [pl.BlockSpec(memory_space=pltpu.SMEM)]
    for g, (_, d) in enumerate(C_CONFIGS):
        unit_rows = C_WINDOW * d
        units = ATTN_TILE // unit_rows
        cq, ck, cv = ((off + g * C_GROUP_WIDTH) // 128 for off in (OFF_CQ, OFF_CK, OFF_CV))

        def cur(c):
            return pl.BlockSpec((ATTN_TILE, 128), lambda b, i, hp, c=c: (b * tiles + i, c + hp))

        def prev(c, units=units, unit_rows=unit_rows):
            return pl.BlockSpec((unit_rows, 128),
                                lambda b, i, hp, c=c: (jnp.maximum((b * tiles + i) * units - 1, 0), c + hp))

        in_specs += [cur(cq), prev(ck), cur(ck), prev(cv), cur(cv)]
    slopes = jnp.array([_slope(gg, hh) for gg in range(C_GROUPS) for hh in range(C_HEADS)], F32)
    n_items = 2 * ATTN_ITEMS
    return pl.pallas_call(
        _attn_prompt_body,
        grid=(bsz, tiles, C_HEADS // 2),
        in_specs=in_specs,
        out_specs=pl.BlockSpec((ATTN_TILE, 128), lambda b, i, hp: (b * tiles + i, hp)),
        out_shape=jax.ShapeDtypeStruct((m, C_GROUP_WIDTH), BF16),
        scratch_shapes=[pltpu.VMEM((C_GROUPS, ATTN_TILE, 128), F32)] * 2
        + [pltpu.VMEM((n_items, C_WINDOW, C_WINDOW), F32)] * 2
        + [pltpu.VMEM((n_items, C_WINDOW, C_WINDOW), BF16)] * 2
        + [pltpu.VMEM((n_items // 2, C_WINDOW, 128), F32)],
        compiler_params=_params("arbitrary", "arbitrary", "arbitrary"),
        name="attn_prompt",
    )(slopes, *([h] * (5 * C_GROUPS)))


def _attn_sample_body(qkv_ref, slope_ref, k0_ref, v0_ref, k1_ref, v1_ref, k2_ref, v2_ref, o_ref):
    caches = ((k0_ref, v0_ref), (k1_ref, v1_ref), (k2_ref, v2_ref))
    rows = C_HEADS * DEC_SEQ
    rid = lax.broadcasted_iota(jnp.int32, (rows, 1), 0)
    i_row = rid & (DEC_SEQ - 1)
    own_lanes = (lax.broadcasted_iota(jnp.int32, (rows, C_GROUP_WIDTH), 1) >> 6) == (
        lax.broadcasted_iota(jnp.int32, (rows, C_GROUP_WIDTH), 0) >> 3)
    outs, lses = [], []
    for g, (w, d) in enumerate(C_CONFIGS):
        kt_ref, vt_ref = caches[g]
        q = qkv_ref[:, g * 512:(g + 1) * 512] * ATTN_SCALE
        kn = qkv_ref[:, (3 + g) * 512:(4 + g) * 512]
        vn = qkv_ref[:, (6 + g) * 512:(7 + g) * 512]
        qbd = jnp.where(own_lanes, jnp.concatenate([q] * C_HEADS, axis=0), 0.0)
        slope = slope_ref[g, :, 0:1]
        kt = kt_ref[...].reshape(C_GROUP_WIDTH, w).astype(BF16)
        vt = vt_ref[...].reshape(C_GROUP_WIDTH, w).astype(BF16)
        pos = lax.broadcasted_iota(jnp.int32, (rows, w), 1)
        back = (w + i_row) - pos
        ok = (pos >= i_row) if d == 1 else ((back & (d - 1)) == 0) & (pos >= i_row)
        s_c = jnp.dot(qbd.astype(BF16), kt, preferred_element_type=F32)
        s_c = jnp.where(ok, s_c - slope * back.astype(F32), NEG_INF)
        s_n = []
        for i2 in range(DEC_SEQ):
            s = jnp.sum(qbd * kn[i2:i2 + 1, :], axis=-1, keepdims=True)
            back_n = i_row - i2
            ok_n = (back_n >= 0) if d == 1 else ((back_n & (d - 1)) == 0) & (back_n >= 0)
            s_n.append(jnp.where(ok_n, s - slope * back_n.astype(F32), NEG_INF))
        mx = jnp.max(s_c, axis=-1, keepdims=True)
        for s in s_n:
            mx = jnp.maximum(mx, s)
        p_c = jnp.exp(s_c - mx)
        den = jnp.sum(p_c, axis=-1, keepdims=True)
        acc = _dot_nt(p_c.astype(BF16), vt)
        for i2, s in enumerate(s_n):
            p = jnp.exp(s - mx)
            den = den + p
            acc = acc + p * vn[i2:i2 + 1, :]
        outs.append(acc / den)
        lses.append(mx + jnp.log(den))
    mx = jnp.maximum(jnp.maximum(lses[0], lses[1]), lses[2])
    e = [jnp.exp(l - mx) for l in lses]
    tot = e[0] + e[1] + e[2]
    merged = (e[0] / tot) * outs[0] + (e[1] / tot) * outs[1] + (e[2] / tot) * outs[2]
    merged = jnp.where(own_lanes, merged, 0.0).reshape(C_HEADS, DEC_SEQ, C_GROUP_WIDTH)
    o_ref[...] = jnp.sum(merged, axis=0)


def _attn_sample(h, caches, layer):
    m = h.shape[0]
    bsz = m // DEC_SEQ
    qkv = h[:, OFF_CQ:D_IN]
    slopes = jnp.array([[_slope(g, hh) for hh in range(C_HEADS)] for g in range(C_GROUPS)], F32)
    slopes = jnp.broadcast_to(jnp.repeat(slopes, DEC_SEQ, axis=1)[:, :, None], (C_GROUPS, C_HEADS * DEC_SEQ, 128))
    in_specs = [pl.BlockSpec((DEC_SEQ, D_IN - OFF_CQ), lambda b: (b, 0)),
                pl.BlockSpec((C_GROUPS, C_HEADS * DEC_SEQ, 128), lambda b: (0, 0, 0))]
    views = []
    for g, (w, _) in enumerate(C_CONFIGS):
        for j in range(2):
            c = caches[2 * g + j]
            assert c.shape[2] == w, "the cache must hold the last W_g positions"
            views.append(jnp.transpose(c, (0, 1, 3, 4, 2)))
            in_specs.append(pl.BlockSpec((None, None, C_HEADS, C_HEAD_DIM, w), lambda b: (layer, b, 0, 0, 0)))
    return pl.pallas_call(
        _attn_sample_body,
        grid=(bsz,),
        in_specs=in_specs,
        out_specs=pl.BlockSpec((DEC_SEQ, C_GROUP_WIDTH), lambda b: (b, 0)),
        out_shape=jax.ShapeDtypeStruct((m, C_GROUP_WIDTH), F32),
        compiler_params=_params("arbitrary"),
        name="attn_sample",
    )(qkv, slopes, *views)


def _merge_out_body(x_ref, oa_ref, ob_ref, oc_ref, gate_ref, wb_ref, wo_ref, g_ref, out_ref):
    merged = None
    for n, o_ref in enumerate((oa_ref, ob_ref, oc_ref)):
        proj = jnp.dot(o_ref[...].astype(BF16), wb_ref[n * BRANCH_WIDTH:(n + 1) * BRANCH_WIDTH, :],
                       preferred_element_type=F32)
        term = jax.nn.sigmoid(gate_ref[:, n * D_MODEL:(n + 1) * D_MODEL].astype(F32)) * proj
        merged = term if merged is None else merged + term
    y = jnp.dot(merged.astype(BF16), wo_ref[...], preferred_element_type=F32)
    out_ref[...] = x_ref[...] + _rms(y, g_ref[...])


def _merge_out(x, h, o_a, o_b, o_c, w_branch_bf16, w_out_bf16, gain):
    m = x.shape[0]
    tm = min(m, 512)
    row512 = pl.BlockSpec((tm, BRANCH_WIDTH), lambda i: (i, 0))
    return pl.pallas_call(
        _merge_out_body,
        grid=(m // tm,),
        in_specs=[pl.BlockSpec((tm, D_MODEL), lambda i: (i, 0)), row512, row512, row512,
                  pl.BlockSpec((tm, N_BRANCH * D_MODEL), lambda i: (i, OFF_G // (N_BRANCH * D_MODEL))),
                  pl.BlockSpec((N_BRANCH * BRANCH_WIDTH, D_MODEL), lambda i: (0, 0)),
                  pl.BlockSpec((D_MODEL, D_MODEL), lambda i: (0, 0)),
                  pl.BlockSpec((1, D_MODEL), lambda i: (0, 0))],
        out_specs=pl.BlockSpec((tm, D_MODEL), lambda i: (i, 0)),
        out_shape=jax.ShapeDtypeStruct((m, D_MODEL), F32),
        compiler_params=_params("arbitrary"),
        name="merge_out",
    )(x, o_a, o_b, o_c, h, w_branch_bf16, w_out_bf16, gain.reshape(1, D_MODEL))


FF_CHUNK = D_FF // 2


def _ffn_body(x_ref, g1_ref, w1_ref, w2_ref, g2_ref, out_ref):
    x = x_ref[...]
    xn = _rms(x, g1_ref[...]).astype(BF16)
    y = None
    for c in range(D_FF // FF_CHUNK):
        gate = jnp.dot(xn, w1_ref[:, c * FF_CHUNK:(c + 1) * FF_CHUNK], preferred_element_type=F32)
        up = jnp.dot(xn, w1_ref[:, D_FF + c * FF_CHUNK:D_FF + (c + 1) * FF_CHUNK], preferred_element_type=F32)
        act = (gate * jax.nn.sigmoid(gate) * up).astype(BF16)
        part = jnp.dot(act, w2_ref[c * FF_CHUNK:(c + 1) * FF_CHUNK, :], preferred_element_type=F32)
        y = part if y is None else y + part
    out_ref[...] = x + _rms(y, g2_ref[...])


def _ffn(x, gain_pre, w1_bf16, w2_bf16, gain_post):
    m = x.shape[0]
    tm = min(m, 512)
    resident = dict(pipeline_mode=pl.Buffered(1))
    return pl.pallas_call(
        _ffn_body,
        grid=(m // tm,),
        in_specs=[pl.BlockSpec((tm, D_MODEL), lambda i: (i, 0)),
                  pl.BlockSpec((1, D_MODEL), lambda i: (0, 0)),
                  pl.BlockSpec((D_MODEL, 2 * D_FF), lambda i: (0, 0), **resident),
                  pl.BlockSpec((D_FF, D_MODEL), lambda i: (0, 0), **resident),
                  pl.BlockSpec((1, D_MODEL), lambda i: (0, 0))],
        out_specs=pl.BlockSpec((tm, D_MODEL), lambda i: (i, 0)),
        out_shape=jax.ShapeDtypeStruct((m, D_MODEL), F32),
        compiler_params=_params("arbitrary"),
        name="ffn",
    )(x, gain_pre.reshape(1, D_MODEL), w1_bf16, w2_bf16, gain_post.reshape(1, D_MODEL))


def _split_heads(cols):
    return cols.reshape(cols.shape[0], cols.shape[1], C_HEADS, C_HEAD_DIM)


def _prompt_trunk(x3, p):
    bsz, t, _ = x3.shape
    x = x3.reshape(bsz * t, D_MODEL)
    depth = p["w_in"].shape[0]
    bufs, kvs = [], []
    for l in range(depth):
        h, kv_tail = _in_proj(x, p["norm_pre_mix"][l], p["w_in"], l, bsz)
        o_a, o_b, buf = _local_mixers(h, bsz, p["a_norm_g"][l], p["a_norm_b"][l], p["a_w_s"][l], p["a_b_s"][l],
                                      p["b_conv_w"][l], p["b_conv_b"][l], p["b_norm_g"][l], p["b_norm_b"][l])
        o_c = _attn_prompt(h, bsz)
        x = _merge_out(x, h, o_a, o_b, o_c, p["w_branch"][l], p["w_out"][l], p["norm_post_mix"][l])
        x = _ffn(x, p["norm_pre_ffn"][l], p["ffn_w_in"][l], p["ffn_w_out"][l], p["norm_post_ffn"][l])
        bufs.append(buf)
        kv3 = kv_tail.reshape(bsz, TAIL, KV_WIDTH)
        layer_kv = []
        for g, (w, _) in enumerate(C_CONFIGS):
            keep = min(w, t)
            for off in (g * C_GROUP_WIDTH, C_GROUPS * C_GROUP_WIDTH + g * C_GROUP_WIDTH):
                layer_kv.append(_split_heads(kv3[:, TAIL - keep:, off:off + C_GROUP_WIDTH]))
        kvs.append(layer_kv)
    new_kv = [jnp.stack([kvs[l][j] for l in range(depth)], axis=0) for j in range(2 * C_GROUPS)]
    return x.reshape(bsz, t, D_MODEL), jnp.stack(bufs, axis=0), new_kv


def _sample_trunk(x3, conv_state, caches, p):
    bsz, t, _ = x3.shape
    x = x3.reshape(bsz * t, D_MODEL)
    depth = p["w_in"].shape[0]
    bufs, vns, kvs = [], [], []
    for l in range(depth):
        h = _in_proj_sample(x, p["norm_pre_mix"][l], p["w_in"], l)
        o_a, vn = _token_mix_sample(h, p["a_norm_g"][l], p["a_norm_b"][l], p["a_w_s"][l], p["a_b_s"][l])
        o_b, buf = _conv_sample(h, conv_state[l], p["b_conv_w"][l], p["b_conv_b"][l],
                                p["b_norm_g"][l], p["b_norm_b"][l])
        o_c = _attn_sample(h, caches, l)
        x = _merge_out(x, h, o_a, o_b, o_c, p["w_branch"][l], p["w_out"][l], p["norm_post_mix"][l])
        x = _ffn(x, p["norm_pre_ffn"][l], p["ffn_w_in"][l], p["ffn_w_out"][l], p["norm_post_ffn"][l])
        bufs.append(buf)
        vns.append(vn.reshape(bsz, t, A_WIDTH))
        h3 = h.reshape(bsz, t, D_IN)
        kvs.append([_split_heads(h3[:, :, off + g * C_GROUP_WIDTH:off + (g + 1) * C_GROUP_WIDTH])
                    for g in range(C_GROUPS) for off in (OFF_CK, OFF_CV)])
    new_kv = [jnp.stack([kvs[l][j] for l in range(depth)], axis=0) for j in range(2 * C_GROUPS)]
    return x.reshape(bsz, t, D_MODEL), jnp.stack(bufs, axis=0), jnp.stack(vns, axis=0), new_kv


def kernel(x_prompt, x_sample, state_b_conv, cache_c0_k, cache_c0_v, cache_c1_k, cache_c1_v, cache_c2_k,
           cache_c2_v, norm_pre_mix, norm_post_mix, norm_pre_ffn, norm_post_ffn, w_in, a_norm_g, a_norm_b,
           a_w_s, a_b_s, b_conv_w, b_conv_b, b_norm_g, b_norm_b, w_branch, w_out, ffn_w_in, ffn_w_out):
    assert x_sample.shape[1] == DEC_SEQ
    p = dict(norm_pre_mix=norm_pre_mix, norm_post_mix=norm_post_mix, norm_pre_ffn=norm_pre_ffn,
             norm_post_ffn=norm_post_ffn, w_in=w_in, a_norm_g=a_norm_g, a_norm_b=a_norm_b,
             a_w_s=a_w_s, a_b_s=a_b_s, b_conv_w=b_conv_w, b_conv_b=b_conv_b, b_norm_g=b_norm_g,
             b_norm_b=b_norm_b, w_branch=w_branch.astype(BF16), w_out=w_out.astype(BF16),
             ffn_w_in=ffn_w_in.astype(BF16), ffn_w_out=ffn_w_out.astype(BF16))
    caches = (cache_c0_k, cache_c0_v, cache_c1_k, cache_c1_v, cache_c2_k, cache_c2_v)
    y_prompt, buf_p, kv_p = _prompt_trunk(x_prompt, p)
    y_sample, buf_s, vn_s, kv_s = _sample_trunk(x_sample, state_b_conv, caches, p)
    return (y_prompt, y_sample, buf_p, buf_s, vn_s, *kv_p, *kv_s)
```

```python
import functools

import jax
import jax.numpy as jnp
from jax import lax
from jax.experimental import pallas as pl
from jax.experimental.pallas import tpu as pltpu

F32 = jnp.float32
BF16 = jnp.bfloat16

D_MODEL = 1024
DEC_SEQ = 8
A_WIDTH = 512
A_GROUPS = 4
A_GROUP_DIM = 128
CHUNK = 128
B_WIDTH = 512
CONV_W = 31
CONV_HALO = 32
C_HEADS = 8
C_HEAD_DIM = 64
C_GROUP_WIDTH = 512
C_CONFIGS = ((128, 1), (512, 4), (2048, 16))
C_GROUPS = 3
C_WINDOW = 128
N_BRANCH = 3
BRANCH_WIDTH = 512
D_FF = 2816
EPS = 1e-6
NEG_INF = -1e30
ATTN_SCALE = C_HEAD_DIM ** -0.5
REF_OFF_G = 6656
OFF_G = 0
OFF_AU = 3072
OFF_AV = 3584
OFF_B = 4096
OFF_CQ = 5120
OFF_CK = 6656
OFF_CV = 8192
D_IN = 9728
COL_BLOCKS = D_IN // 512

VMEM_LIMIT = 56 * 1024 * 1024


def _slope(g, h):
    n = C_GROUPS * C_HEADS
    return 2.0 ** (-8.0 * (h * C_GROUPS + g + 1.0) / n)


def _params(*sem):
    return pltpu.CompilerParams(dimension_semantics=sem, vmem_limit_bytes=VMEM_LIMIT)


def _rms(x, gain):
    return x * lax.rsqrt(jnp.mean(x * x, axis=-1, keepdims=True) + EPS) * gain


def _layer_norm(x, gain, bias):
    mu = jnp.mean(x, axis=-1, keepdims=True)
    xc = x - mu
    var = jnp.mean(xc * xc, axis=-1, keepdims=True)
    return xc * lax.rsqrt(var + EPS) * gain + bias


Q_TILE0 = OFF_CQ // 512
KV_TILE0 = OFF_CK // 512
KV_WIDTH = D_IN - OFF_CK
TAIL = 2048


def _in_proj_sample_body(x_ref, g_ref, w_ref, o_ref, xn_ref):
    @pl.when(pl.program_id(1) == 0)
    def _():
        xn_ref[...] = _rms(x_ref[...], g_ref[...]).astype(BF16)

    o_ref[...] = jnp.dot(xn_ref[...], w_ref[...].astype(BF16), preferred_element_type=F32)


def _w_in_tile(j):
    return lax.rem(j + REF_OFF_G // 512, COL_BLOCKS)


def _in_proj_sample(x, gain, w_in, layer):
    m = x.shape[0]
    tn = 512
    return pl.pallas_call(
        _in_proj_sample_body,
        grid=(1, D_IN // tn),
        in_specs=[pl.BlockSpec((m, D_MODEL), lambda i, j: (0, 0)),
                  pl.BlockSpec((1, D_MODEL), lambda i, j: (0, 0)),
                  pl.BlockSpec((None, D_MODEL, tn), lambda i, j: (layer, 0, _w_in_tile(j)))],
        out_specs=pl.BlockSpec((m, tn), lambda i, j: (0, j)),
        out_shape=jax.ShapeDtypeStruct((m, D_IN), F32),
        scratch_shapes=[pltpu.VMEM((m, D_MODEL), BF16)],
        compiler_params=_params("arbitrary", "arbitrary"),
        name="in_proj_sample",
    )(x, gain.reshape(1, D_MODEL), w_in)


def _in_proj_body(x_ref, g_ref, w_ref, o_ref, kv_ref, xn_ref, res_ref, tmp_ref, *, tm, tiles_per_seq):
    i = pl.program_id(0)
    j = pl.program_id(1)
    lanes = res_ref.shape[1]

    def matmul_tile(slot):
        res = jnp.dot(xn_ref[...], w_ref[...].astype(BF16), preferred_element_type=F32)
        for c in range(lanes):
            res_ref[slot, c] = res[:, c * 128:(c + 1) * 128]

    def emit(slot, d):
        if d == 1:
            for c in range(lanes):
                o_ref[:, c * 128:(c + 1) * 128] = res_ref[slot, c].astype(BF16)
            return
        if d == 16:
            quarter = tm // 4
            for b4 in range(4):
                for c in range(lanes):
                    tmp_ref[c, b4 * quarter:(b4 + 1) * quarter, :] = res_ref[slot, c, pl.ds(b4, quarter, stride=4), :]
            for r in range(d):
                for c in range(lanes):
                    o_ref[r * C_WINDOW:(r + 1) * C_WINDOW, c * 128:(c + 1) * 128] = (
                        tmp_ref[c, pl.ds((r % 4) * quarter + r // 4, C_WINDOW, stride=4), :].astype(BF16))
            return
        for u in range(tm // (C_WINDOW * d)):
            for r in range(d):
                dst = u * C_WINDOW * d + r * C_WINDOW
                for c in range(lanes):
                    o_ref[dst:dst + C_WINDOW, c * 128:(c + 1) * 128] = (
                        res_ref[slot, c, pl.ds(u * C_WINDOW * d + r, C_WINDOW, stride=d), :].astype(BF16))

    @pl.when(j == 0)
    def _():
        xn_ref[...] = _rms(x_ref[...], g_ref[...]).astype(BF16)
        matmul_tile(0)

    prev_group = lax.rem(jnp.maximum(j - 1 - Q_TILE0, 0), C_GROUPS)
    prev_dilation_group = jnp.where(j - 1 >= Q_TILE0, prev_group, 0)
    in_range = jnp.logical_and(j >= 1, j < COL_BLOCKS)
    for parity in range(2):
        for g, (_, d) in enumerate(C_CONFIGS):
            @pl.when(jnp.logical_and(jnp.logical_and(in_range, lax.rem(j, 2) == parity), prev_dilation_group == g))
            def _():
                emit(1 - parity, d)
                matmul_tile(parity)

    @pl.when(j == COL_BLOCKS)
    def _():
        emit((COL_BLOCKS - 1) % 2, C_CONFIGS[(COL_BLOCKS - 1 - Q_TILE0) % C_GROUPS][1])

    @pl.when(jnp.logical_and(i % tiles_per_seq == tiles_per_seq - 1,
                             jnp.logical_and(j >= KV_TILE0, j < COL_BLOCKS)))
    def _():
        for c in range(lanes):
            kv_ref[:, c * 128:(c + 1) * 128] = res_ref[lax.rem(j, 2), c]


def _in_proj(x, gain, w_in, layer, bsz):
    m = x.shape[0]
    t = m // bsz
    tm = TAIL
    tn = 512
    assert t % tm == 0
    tiles_per_seq = t // tm
    last = COL_BLOCKS - 1

    def kv_index(i, j):
        is_tail = i % tiles_per_seq == tiles_per_seq - 1
        return i // tiles_per_seq, jnp.where(is_tail, jnp.clip(j - KV_TILE0, 0, last - KV_TILE0), 0)

    return pl.pallas_call(
        functools.partial(_in_proj_body, tm=tm, tiles_per_seq=tiles_per_seq),
        grid=(m // tm, COL_BLOCKS + 1),
        in_specs=[pl.BlockSpec((tm, D_MODEL), lambda i, j: (i, 0)),
                  pl.BlockSpec((1, D_MODEL), lambda i, j: (0, 0)),
                  pl.BlockSpec((None, D_MODEL, tn), lambda i, j: (layer, 0, _w_in_tile(jnp.minimum(j, last))))],
        out_specs=[pl.BlockSpec((tm, tn), lambda i, j: (i, jnp.maximum(j - 1, 0))),
                   pl.BlockSpec((tm, tn), kv_index)],
        out_shape=[jax.ShapeDtypeStruct((m, D_IN), BF16),
                   jax.ShapeDtypeStruct((bsz * TAIL, KV_WIDTH), F32)],
        scratch_shapes=[pltpu.VMEM((tm, D_MODEL), BF16), pltpu.VMEM((2, tn // 128, tm, 128), F32),
                        pltpu.VMEM((tn // 128, tm, 128), F32)],
        compiler_params=_params("arbitrary", "arbitrary"),
        name="in_proj",
    )(x, gain.reshape(1, D_MODEL), w_in)


def _token_mix_sample_body(u_ref, v_ref, ng_ref, nb_ref, w_ref, bias_ref, o_ref, vn_ref, *, bsz):
    vn = _layer_norm(v_ref[...], ng_ref[...], nb_ref[...])
    vn_ref[...] = vn
    vn3 = vn.reshape(bsz, DEC_SEQ, A_WIDTH)
    t_idx = lax.broadcasted_iota(jnp.int32, (DEC_SEQ, A_WIDTH), 0)
    mix = jnp.broadcast_to(bias_ref[...], (bsz, DEC_SEQ, A_WIDTH))
    for s in range(DEC_SEQ):
        w_s = jnp.where(t_idx >= s, w_ref[s], 0.0)
        mix = mix + jnp.broadcast_to(vn3[:, s:s + 1, :], (bsz, DEC_SEQ, A_WIDTH)) * w_s
    o_ref[...] = u_ref[...] * mix.reshape(bsz * DEC_SEQ, A_WIDTH)


def _token_mix_sample(h, norm_g, norm_b, w_s, b_s):
    m = h.shape[0]
    bsz = m // DEC_SEQ
    w = jnp.repeat(jnp.transpose(w_s[:, :DEC_SEQ, :DEC_SEQ], (2, 1, 0)), A_GROUP_DIM, axis=2)
    bias = jnp.repeat(b_s[:, :DEC_SEQ].T, A_GROUP_DIM, axis=1)
    vec = pl.BlockSpec((1, A_WIDTH), lambda i: (0, 0))
    return pl.pallas_call(
        functools.partial(_token_mix_sample_body, bsz=bsz),
        grid=(1,),
        in_specs=[pl.BlockSpec((m, A_WIDTH), lambda i: (0, OFF_AU // A_WIDTH)),
                  pl.BlockSpec((m, A_WIDTH), lambda i: (0, OFF_AV // A_WIDTH)),
                  vec, vec,
                  pl.BlockSpec((DEC_SEQ, DEC_SEQ, A_WIDTH), lambda i: (0, 0, 0)),
                  pl.BlockSpec((DEC_SEQ, A_WIDTH), lambda i: (0, 0))],
        out_specs=[pl.BlockSpec((m, A_WIDTH), lambda i: (0, 0))] * 2,
        out_shape=[jax.ShapeDtypeStruct((m, A_WIDTH), F32)] * 2,
        compiler_params=_params("arbitrary"),
        name="token_mix_sample",
    )(h, h, norm_g.reshape(1, A_WIDTH), norm_b.reshape(1, A_WIDTH), w, bias)


def _glu(z):
    return z[:, :B_WIDTH] * jax.nn.sigmoid(z[:, B_WIDTH:])


def _local_mixers_body(u_ref, v_ref, ng_ref, nb_ref, w_ref, bias_ref, z_ref, zh_ref, cw_ref, cb_ref, lg_ref, lb_ref,
                       oa_ref, ob_ref, buf_ref, pad_ref, cw8_ref, *, tr, tiles):
    row = lax.broadcasted_iota(jnp.int32, (CHUNK, CHUNK), 0)
    col = lax.broadcasted_iota(jnp.int32, (CHUNK, CHUNK), 1)
    wm = [jnp.where(col <= row, w_ref[g], 0.0).astype(BF16) for g in range(A_GROUPS)]
    for c in range(tr // CHUNK):
        rows = slice(c * CHUNK, (c + 1) * CHUNK)
        vnb = _layer_norm(v_ref[rows, :].astype(F32), ng_ref[...], nb_ref[...]).astype(BF16)
        for g in range(A_GROUPS):
            lanes = slice(g * A_GROUP_DIM, (g + 1) * A_GROUP_DIM)
            mix = jnp.dot(wm[g], vnb[:, lanes], preferred_element_type=F32) + bias_ref[:, lanes]
            oa_ref[rows, lanes] = (u_ref[rows, lanes].astype(F32) * mix).astype(oa_ref.dtype)

    @pl.when(pl.program_id(0) == 0)
    def _():
        for k in range(CONV_W):
            cw8_ref[k] = jnp.broadcast_to(cw_ref[k:k + 1, :], (8, B_WIDTH))

    t = pl.program_id(0) % tiles
    hg = _glu(z_ref[...].astype(F32))
    pad_ref[0:CONV_HALO, :] = jnp.where(t == 0, 0.0, _glu(zh_ref[...].astype(F32)))
    pad_ref[CONV_HALO:CONV_HALO + tr, :] = hg
    first = CONV_HALO - (CONV_W - 1)
    sub = 32
    taps = [[k for k in range(CONV_W) if (first + k) % 8 == s] for s in range(8)]
    for blk in range(tr // sub):
        acc = jnp.zeros((sub // 8, 8, B_WIDTH), F32)
        for s in range(8):
            q_max = max((first + k) // 8 for k in taps[s])
            win = pad_ref[blk * sub + s:blk * sub + s + sub + 8 * q_max, :]
            for k in taps[s]:
                q = (first + k) // 8
                acc = acc + win[8 * q:8 * q + sub, :].reshape(sub // 8, 8, B_WIDTH) * cw8_ref[k]
        y = _layer_norm(acc.reshape(sub, B_WIDTH) + cb_ref[...], lg_ref[...], lb_ref[...])
        ob_ref[blk * sub:(blk + 1) * sub, :] = (y * jax.nn.sigmoid(y)).astype(ob_ref.dtype)

    @pl.when(t == tiles - 1)
    def _():
        buf_ref[...] = pad_ref[CONV_HALO + tr - (CONV_W - 1):CONV_HALO + tr, :]


def _local_mixers(h, bsz, norm_g, norm_b, w_s, b_s, conv_w, conv_b, ln_g, ln_b):
    m = h.shape[0]
    t = m // bsz
    tr = min(512, t)
    tiles = t // tr
    halo_per_tile = tr // CONV_HALO
    bias = jnp.repeat(b_s.T, A_GROUP_DIM, axis=1)
    vec = pl.BlockSpec((1, B_WIDTH), lambda i: (0, 0))
    body = functools.partial(_local_mixers_body, tr=tr, tiles=tiles)
    return pl.pallas_call(
        body,
        grid=(m // tr,),
        in_specs=[pl.BlockSpec((tr, A_WIDTH), lambda i: (i, OFF_AU // A_WIDTH)),
                  pl.BlockSpec((tr, A_WIDTH), lambda i: (i, OFF_AV // A_WIDTH)),
                  vec, vec,
                  pl.BlockSpec((A_GROUPS, CHUNK, CHUNK), lambda i: (0, 0, 0)),
                  pl.BlockSpec((CHUNK, A_WIDTH), lambda i: (0, 0)),
                  pl.BlockSpec((tr, 2 * B_WIDTH), lambda i: (i, OFF_B // (2 * B_WIDTH))),
                  pl.BlockSpec((CONV_HALO, 2 * B_WIDTH),
                               lambda i: (jnp.maximum(i * halo_per_tile - 1, 0), OFF_B // (2 * B_WIDTH))),
                  pl.BlockSpec((CONV_W, B_WIDTH), lambda i: (0, 0)),
                  vec, vec, vec],
        out_specs=[pl.BlockSpec((tr, A_WIDTH), lambda i: (i, 0)),
                   pl.BlockSpec((tr, B_WIDTH), lambda i: (i, 0)),
                   pl.BlockSpec((None, CONV_W - 1, B_WIDTH), lambda i: (i // tiles, 0, 0))],
        out_shape=[jax.ShapeDtypeStruct((m, A_WIDTH), BF16),
                   jax.ShapeDtypeStruct((m, B_WIDTH), BF16),
                   jax.ShapeDtypeStruct((bsz, CONV_W - 1, B_WIDTH), F32)],
        scratch_shapes=[pltpu.VMEM((CONV_HALO + tr, B_WIDTH), F32), pltpu.VMEM((CONV_W, 8, B_WIDTH), F32)],
        compiler_params=_params("arbitrary"),
        name="local_mixers",
    )(h, h, norm_g.reshape(1, A_WIDTH), norm_b.reshape(1, A_WIDTH), w_s, bias, h, h, conv_w, conv_b.reshape(1, B_WIDTH),
      ln_g.reshape(1, B_WIDTH), ln_b.reshape(1, B_WIDTH))


def _conv_sample_body(z_ref, st_ref, cw_ref, cb_ref, lg_ref, lb_ref, o_ref, buf_ref, pad_ref, *, bsz):
    hg = _glu(z_ref[...])
    pad_ref[:, 0:CONV_W - 1, :] = st_ref[...]
    pad_ref[:, CONV_W - 1:CONV_W - 1 + DEC_SEQ, :] = hg.reshape(bsz, DEC_SEQ, B_WIDTH)
    acc = jnp.zeros((bsz, DEC_SEQ, B_WIDTH), F32)
    for k in range(CONV_W):
        acc = acc + cw_ref[k:k + 1, :] * pad_ref[:, k:k + DEC_SEQ, :]
    y = _layer_norm(acc + cb_ref[...], lg_ref[...], lb_ref[...])
    o_ref[...] = (y * jax.nn.sigmoid(y)).reshape(bsz * DEC_SEQ, B_WIDTH)
    buf_ref[...] = pad_ref[:, DEC_SEQ:DEC_SEQ + CONV_W - 1, :]


def _conv_sample(h, state, conv_w, conv_b, ln_g, ln_b):
    m = h.shape[0]
    bsz = state.shape[0]
    body = functools.partial(_conv_sample_body, bsz=bsz)
    return pl.pallas_call(
        body,
        grid=(1,),
        in_specs=[pl.BlockSpec((m, 2 * B_WIDTH), lambda i: (0, OFF_B // (2 * B_WIDTH))),
                  pl.BlockSpec((bsz, CONV_W - 1, B_WIDTH), lambda i: (0, 0, 0)),
                  pl.BlockSpec((CONV_W, B_WIDTH), lambda i: (0, 0)),
                  pl.BlockSpec((1, B_WIDTH), lambda i: (0, 0)),
                  pl.BlockSpec((1, B_WIDTH), lambda i: (0, 0)),
                  pl.BlockSpec((1, B_WIDTH), lambda i: (0, 0))],
        out_specs=[pl.BlockSpec((m, B_WIDTH), lambda i: (0, 0)),
                   pl.BlockSpec((bsz, CONV_W - 1, B_WIDTH), lambda i: (0, 0, 0))],
        out_shape=[jax.ShapeDtypeStruct((m, B_WIDTH), F32),
                   jax.ShapeDtypeStruct((bsz, CONV_W - 1, B_WIDTH), F32)],
        scratch_shapes=[pltpu.VMEM((bsz, CONV_W - 1 + DEC_SEQ, B_WIDTH), F32)],
        compiler_params=_params("arbitrary"),
        name="conv_sample",
    )(h, state, conv_w, conv_b.reshape(1, B_WIDTH), ln_g.reshape(1, B_WIDTH), ln_b.reshape(1, B_WIDTH))


def _dot_nt(a, b):
    return lax.dot_general(a, b, (((1,), (1,)), ((), ())), preferred_element_type=F32)


ATTN_TILE = 2048
ATTN_ITEMS = 8


def _attn_prompt_body(slope_ref, *refs):
    group_refs = [refs[5 * g:5 * g + 5] for g in range(C_GROUPS)]
    o_ref, og_ref, lg_ref, sm_ref, sx_ref, pp_ref, pc_ref, mb_ref = refs[5 * C_GROUPS:]
    tq = C_WINDOW
    first_tile = pl.program_id(1) == 0
    hp = pl.program_id(2)
    row = lax.broadcasted_iota(jnp.int32, (tq, tq), 0)
    col = lax.broadcasted_iota(jnp.int32, (tq, tq), 1)
    upper = col > row
    diag = col == row
    diag0 = col == row + jnp.where(first_tile, tq, 0)
    valid0 = col <= row + jnp.where(first_tile, 0, tq)
    dist = jnp.where(upper, row - col + tq, row - col).astype(F32)
    low = lax.broadcasted_iota(jnp.int32, (tq, 128), 1) < C_HEAD_DIM
    sels = (low, jnp.logical_not(low))

    for g, (_, d) in enumerate(C_CONFIGS):
        q_ref, kp_ref, kc_ref, vp_ref, vc_ref = group_refs[g]
        slopes = [slope_ref[g * C_HEADS + 2 * hp + hh] * float(d) for hh in range(2)]
        bias = [-(s * dist) for s in slopes]
        bias_x = [-(s * float(tq)) for s in slopes]
        blocks = [(u, r) for u in range(ATTN_TILE // (tq * d)) for r in range(d)]

        def rows(u, r):
            return pl.ds(u * tq * d + r * tq, tq)

        def out_rows(u, r):
            return pl.ds(u * tq * d + r, tq, stride=d) if d > 1 else pl.ds(u * tq, tq)

        for first in range(0, len(blocks), ATTN_ITEMS):
            items = blocks[first:first + ATTN_ITEMS]
            for n, (u, r) in enumerate(items):
                q2 = q_ref[rows(u, r), :] * ATTN_SCALE
                kc2 = kc_ref[rows(u, r), :]
                kp2 = kp_ref[rows(0, r), :] if u == 0 else kc_ref[rows(u - 1, r), :]
                for hh in range(2):
                    qm = jnp.where(sels[hh], q2, 0.0)
                    sp = _dot_nt(qm, kp2)
                    sm = jnp.where(upper, sp, _dot_nt(qm, kc2)) + bias[hh]
                    if u == 0:
                        sm = jnp.where(valid0, sm, NEG_INF)
                    sm_ref[2 * n + hh] = sm
                    sx_ref[2 * n + hh] = jnp.where(diag0 if u == 0 else diag, sp + bias_x[hh], NEG_INF)
            for n in range(len(items)):
                mx = []
                for hh in range(2):
                    sm = sm_ref[2 * n + hh]
                    sx = sx_ref[2 * n + hh]
                    m = jnp.max(jnp.maximum(sm, sx), axis=-1, keepdims=True)
                    p = jnp.exp(sm - m)
                    pp_ref[2 * n + hh] = jnp.where(upper, p, jnp.exp(sx - m)).astype(BF16)
                    pc_ref[2 * n + hh] = jnp.where(upper, 0.0, p).astype(BF16)
                    mx.append(m)
                mb_ref[n] = jnp.where(low, mx[0], mx[1])
            for n, (u, r) in enumerate(items):
                vc2 = vc_ref[rows(u, r), :]
                vp2 = vp_ref[rows(0, r), :] if u == 0 else vc_ref[rows(u - 1, r), :]
                acc = []
                for hh in range(2):
                    vpx = jnp.where(sels[hh], vp2, 1.0)
                    vcx = jnp.where(sels[hh], vc2, 1.0)
                    acc.append(jnp.dot(pp_ref[2 * n + hh], vpx, preferred_element_type=F32)
                               + jnp.dot(pc_ref[2 * n + hh], vcx, preferred_element_type=F32))
                den = pltpu.roll(jnp.where(low, acc[1], acc[0]), C_HEAD_DIM, axis=1)
                og_ref[g, out_rows(u, r), :] = jnp.where(low, acc[0], acc[1]) / den
                lg_ref[g, out_rows(u, r), :] = mb_ref[n] + jnp.log(den)

    chunk = 256
    for c in range(ATTN_TILE // chunk):
        rs = slice(c * chunk, (c + 1) * chunk)
        l0, l1, l2 = lg_ref[0, rs, :], lg_ref[1, rs, :], lg_ref[2, rs, :]
        mx = jnp.maximum(jnp.maximum(l0, l1), l2)
        e0, e1, e2 = jnp.exp(l0 - mx), jnp.exp(l1 - mx), jnp.exp(l2 - mx)
        merged = e0 * og_ref[0, rs, :] + e1 * og_ref[1, rs, :] + e2 * og_ref[2, rs, :]
        o_ref[rs, :] = (merged / (e0 + e1 + e2)).astype(o_ref.dtype)


def _attn_prompt(h, bsz):
    m = h.shape[0]
    t = m // bsz
    assert t % ATTN_TILE == 0
    tiles = t // ATTN_TILE
    in_specs = [pl.BlockSpec(memory_space=pltpu.SMEM)]
    for g, (_, d) in enumerate(C_CONFIGS):
        unit_rows = C_WINDOW * d
        units = ATTN_TILE // unit_rows
        cq, ck, cv = ((off + g * C_GROUP_WIDTH) // 128 for off in (OFF_CQ, OFF_CK, OFF_CV))

        def cur(c):
            return pl.BlockSpec((ATTN_TILE, 128), lambda b, i, hp, c=c: (b * tiles + i, c + hp))

        def prev(c, units=units, unit_rows=unit_rows):
            return pl.BlockSpec((unit_rows, 128),
                                lambda b, i, hp, c=c: (jnp.maximum((b * tiles + i) * units - 1, 0), c + hp))

        in_specs += [cur(cq), prev(ck), cur(ck), prev(cv), cur(cv)]
    slopes = jnp.array([_slope(gg, hh) for gg in range(C_GROUPS) for hh in range(C_HEADS)], F32)
    n_items = 2 * ATTN_ITEMS
    return pl.pallas_call(
        _attn_prompt_body,
        grid=(bsz, tiles, C_HEADS // 2),
        in_specs=in_specs,
        out_specs=pl.BlockSpec((ATTN_TILE, 128), lambda b, i, hp: (b * tiles + i, hp)),
        out_shape=jax.ShapeDtypeStruct((m, C_GROUP_WIDTH), BF16),
        scratch_shapes=[pltpu.VMEM((C_GROUPS, ATTN_TILE, 128), F32)] * 2
        + [pltpu.VMEM((n_items, C_WINDOW, C_WINDOW), F32)] * 2
        + [pltpu.VMEM((n_items, C_WINDOW, C_WINDOW), BF16)] * 2
        + [pltpu.VMEM((n_items // 2, C_WINDOW, 128), F32)],
        compiler_params=_params("arbitrary", "arbitrary", "arbitrary"),
        name="attn_prompt",
    )(slopes, *([h] * (5 * C_GROUPS)))


def _attn_sample_body(qkv_ref, slope_ref, k0_ref, v0_ref, k1_ref, v1_ref, k2_ref, v2_ref, o_ref):
    caches = ((k0_ref, v0_ref), (k1_ref, v1_ref), (k2_ref, v2_ref))
    rows = C_HEADS * DEC_SEQ
    rid = lax.broadcasted_iota(jnp.int32, (rows, 1), 0)
    i_row = rid & (DEC_SEQ - 1)
    own_lanes = (lax.broadcasted_iota(jnp.int32, (rows, C_GROUP_WIDTH), 1) >> 6) == (
        lax.broadcasted_iota(jnp.int32, (rows, C_GROUP_WIDTH), 0) >> 3)
    outs, lses = [], []
    for g, (w, d) in enumerate(C_CONFIGS):
        kt_ref, vt_ref = caches[g]
        q = qkv_ref[:, g * 512:(g + 1) * 512] * ATTN_SCALE
        kn = qkv_ref[:, (3 + g) * 512:(4 + g) * 512]
        vn = qkv_ref[:, (6 + g) * 512:(7 + g) * 512]
        qbd = jnp.where(own_lanes, jnp.concatenate([q] * C_HEADS, axis=0), 0.0)
        slope = slope_ref[g, :, 0:1]
        kt = kt_ref[...].reshape(C_GROUP_WIDTH, w).astype(BF16)
        vt = vt_ref[...].reshape(C_GROUP_WIDTH, w).astype(BF16)
        pos = lax.broadcasted_iota(jnp.int32, (rows, w), 1)
        back = (w + i_row) - pos
        ok = (pos >= i_row) if d == 1 else ((back & (d - 1)) == 0) & (pos >= i_row)
        s_c = jnp.dot(qbd.astype(BF16), kt, preferred_element_type=F32)
        s_c = jnp.where(ok, s_c - slope * back.astype(F32), NEG_INF)
        s_n = []
        for i2 in range(DEC_SEQ):
            s = jnp.sum(qbd * kn[i2:i2 + 1, :], axis=-1, keepdims=True)
            back_n = i_row - i2
            ok_n = (back_n >= 0) if d == 1 else ((back_n & (d - 1)) == 0) & (back_n >= 0)
            s_n.append(jnp.where(ok_n, s - slope * back_n.astype(F32), NEG_INF))
        mx = jnp.max(s_c, axis=-1, keepdims=True)
        for s in s_n:
            mx = jnp.maximum(mx, s)
        p_c = jnp.exp(s_c - mx)
        den = jnp.sum(p_c, axis=-1, keepdims=True)
        acc = _dot_nt(p_c.astype(BF16), vt)
        for i2, s in enumerate(s_n):
            p = jnp.exp(s - mx)
            den = den + p
            acc = acc + p * vn[i2:i2 + 1, :]
        outs.append(acc / den)
        lses.append(mx + jnp.log(den))
    mx = jnp.maximum(jnp.maximum(lses[0], lses[1]), lses[2])
    e = [jnp.exp(l - mx) for l in lses]
    tot = e[0] + e[1] + e[2]
    merged = (e[0] / tot) * outs[0] + (e[1] / tot) * outs[1] + (e[2] / tot) * outs[2]
    merged = jnp.where(own_lanes, merged, 0.0).reshape(C_HEADS, DEC_SEQ, C_GROUP_WIDTH)
    o_ref[...] = jnp.sum(merged, axis=0)


def _attn_sample(h, caches, layer):
    m = h.shape[0]
    bsz = m // DEC_SEQ
    qkv = h[:, OFF_CQ:D_IN]
    slopes = jnp.array([[_slope(g, hh) for hh in range(C_HEADS)] for g in range(C_GROUPS)], F32)
    slopes = jnp.broadcast_to(jnp.repeat(slopes, DEC_SEQ, axis=1)[:, :, None], (C_GROUPS, C_HEADS * DEC_SEQ, 128))
    in_specs = [pl.BlockSpec((DEC_SEQ, D_IN - OFF_CQ), lambda b: (b, 0)),
                pl.BlockSpec((C_GROUPS, C_HEADS * DEC_SEQ, 128), lambda b: (0, 0, 0))]
    views = []
    for g, (w, _) in enumerate(C_CONFIGS):
        for j in range(2):
            c = caches[2 * g + j]
            assert c.shape[2] == w, "the cache must hold the last W_g positions"
            views.append(jnp.transpose(c, (0, 1, 3, 4, 2)))
            in_specs.append(pl.BlockSpec((None, None, C_HEADS, C_HEAD_DIM, w), lambda b: (layer, b, 0, 0, 0)))
    return pl.pallas_call(
        _attn_sample_body,
        grid=(bsz,),
        in_specs=in_specs,
        out_specs=pl.BlockSpec((DEC_SEQ, C_GROUP_WIDTH), lambda b: (b, 0)),
        out_shape=jax.ShapeDtypeStruct((m, C_GROUP_WIDTH), F32),
        compiler_params=_params("arbitrary"),
        name="attn_sample",
    )(qkv, slopes, *views)


def _merge_out_body(x_ref, oa_ref, ob_ref, oc_ref, gate_ref, wb_ref, wo_ref, g_ref, out_ref):
    merged = None
    for n, o_ref in enumerate((oa_ref, ob_ref, oc_ref)):
        proj = jnp.dot(o_ref[...].astype(BF16), wb_ref[n * BRANCH_WIDTH:(n + 1) * BRANCH_WIDTH, :],
                       preferred_element_type=F32)
        term = jax.nn.sigmoid(gate_ref[:, n * D_MODEL:(n + 1) * D_MODEL].astype(F32)) * proj
        merged = term if merged is None else merged + term
    y = jnp.dot(merged.astype(BF16), wo_ref[...], preferred_element_type=F32)
    out_ref[...] = x_ref[...] + _rms(y, g_ref[...])


def _merge_out(x, h, o_a, o_b, o_c, w_branch_bf16, w_out_bf16, gain):
    m = x.shape[0]
    tm = min(m, 512)
    row512 = pl.BlockSpec((tm, BRANCH_WIDTH), lambda i: (i, 0))
    return pl.pallas_call(
        _merge_out_body,
        grid=(m // tm,),
        in_specs=[pl.BlockSpec((tm, D_MODEL), lambda i: (i, 0)), row512, row512, row512,
                  pl.BlockSpec((tm, N_BRANCH * D_MODEL), lambda i: (i, OFF_G // (N_BRANCH * D_MODEL))),
                  pl.BlockSpec((N_BRANCH * BRANCH_WIDTH, D_MODEL), lambda i: (0, 0)),
                  pl.BlockSpec((D_MODEL, D_MODEL), lambda i: (0, 0)),
                  pl.BlockSpec((1, D_MODEL), lambda i: (0, 0))],
        out_specs=pl.BlockSpec((tm, D_MODEL), lambda i: (i, 0)),
        out_shape=jax.ShapeDtypeStruct((m, D_MODEL), F32),
        compiler_params=_params("arbitrary"),
        name="merge_out",
    )(x, o_a, o_b, o_c, h, w_branch_bf16, w_out_bf16, gain.reshape(1, D_MODEL))


FF_CHUNK = D_FF // 2


def _ffn_body(x_ref, g1_ref, w1_ref, w2_ref, g2_ref, out_ref):
    x = x_ref[...]
    xn = _rms(x, g1_ref[...]).astype(BF16)
    y = None
    for c in range(D_FF // FF_CHUNK):
        gate = jnp.dot(xn, w1_ref[:, c * FF_CHUNK:(c + 1) * FF_CHUNK], preferred_element_type=F32)
        up = jnp.dot(xn, w1_ref[:, D_FF + c * FF_CHUNK:D_FF + (c + 1) * FF_CHUNK], preferred_element_type=F32)
        act = (gate * jax.nn.sigmoid(gate) * up).astype(BF16)
        part = jnp.dot(act, w2_ref[c * FF_CHUNK:(c + 1) * FF_CHUNK, :], preferred_element_type=F32)
        y = part if y is None else y + part
    out_ref[...] = x + _rms(y, g2_ref[...])


def _ffn(x, gain_pre, w1_bf16, w2_bf16, gain_post):
    m = x.shape[0]
    tm = min(m, 512)
    resident = dict(pipeline_mode=pl.Buffered(1))
    return pl.pallas_call(
        _ffn_body,
        grid=(m // tm,),
        in_specs=[pl.BlockSpec((tm, D_MODEL), lambda i: (i, 0)),
                  pl.BlockSpec((1, D_MODEL), lambda i: (0, 0)),
                  pl.BlockSpec((D_MODEL, 2 * D_FF), lambda i: (0, 0), **resident),
                  pl.BlockSpec((D_FF, D_MODEL), lambda i: (0, 0), **resident),
                  pl.BlockSpec((1, D_MODEL), lambda i: (0, 0))],
        out_specs=pl.BlockSpec((tm, D_MODEL), lambda i: (i, 0)),
        out_shape=jax.ShapeDtypeStruct((m, D_MODEL), F32),
        compiler_params=_params("arbitrary"),
        name="ffn",
    )(x, gain_pre.reshape(1, D_MODEL), w1_bf16, w2_bf16, gain_post.reshape(1, D_MODEL))


def _split_heads(cols):
    return cols.reshape(cols.shape[0], cols.shape[1], C_HEADS, C_HEAD_DIM)


def _prompt_trunk(x3, p):
    bsz, t, _ = x3.shape
    x = x3.reshape(bsz * t, D_MODEL)
    depth = p["w_in"].shape[0]
    bufs, kvs = [], []
    for l in range(depth):
        h, kv_tail = _in_proj(x, p["norm_pre_mix"][l], p["w_in"], l, bsz)
        o_a, o_b, buf = _local_mixers(h, bsz, p["a_norm_g"][l], p["a_norm_b"][l], p["a_w_s"][l], p["a_b_s"][l],
                                      p["b_conv_w"][l], p["b_conv_b"][l], p["b_norm_g"][l], p["b_norm_b"][l])
        o_c = _attn_prompt(h, bsz)
        x = _merge_out(x, h, o_a, o_b, o_c, p["w_branch"][l], p["w_out"][l], p["norm_post_mix"][l])
        x = _ffn(x, p["norm_pre_ffn"][l], p["ffn_w_in"][l], p["ffn_w_out"][l], p["norm_post_ffn"][l])
        bufs.append(buf)
        kv3 = kv_tail.reshape(bsz, TAIL, KV_WIDTH)
        layer_kv = []
        for g, (w, _) in enumerate(C_CONFIGS):
            keep = min(w, t)
            for off in (g * C_GROUP_WIDTH, C_GROUPS * C_GROUP_WIDTH + g * C_GROUP_WIDTH):
                layer_kv.append(_split_heads(kv3[:, TAIL - keep:, off:off + C_GROUP_WIDTH]))
        kvs.append(layer_kv)
    new_kv = [jnp.stack([kvs[l][j] for l in range(depth)], axis=0) for j in range(2 * C_GROUPS)]
    return x.reshape(bsz, t, D_MODEL), jnp.stack(bufs, axis=0), new_kv


def _sample_trunk(x3, conv_state, caches, p):
    bsz, t, _ = x3.shape
    x = x3.reshape(bsz * t, D_MODEL)
    depth = p["w_in"].shape[0]
    bufs, vns, kvs = [], [], []
    for l in range(depth):
        h = _in_proj_sample(x, p["norm_pre_mix"][l], p["w_in"], l)
        o_a, vn = _token_mix_sample(h, p["a_norm_g"][l], p["a_norm_b"][l], p["a_w_s"][l], p["a_b_s"][l])
        o_b, buf = _conv_sample(h, conv_state[l], p["b_conv_w"][l], p["b_conv_b"][l],
                                p["b_norm_g"][l], p["b_norm_b"][l])
        o_c = _attn_sample(h, caches, l)
        x = _merge_out(x, h, o_a, o_b, o_c, p["w_branch"][l], p["w_out"][l], p["norm_post_mix"][l])
        x = _ffn(x, p["norm_pre_ffn"][l], p["ffn_w_in"][l], p["ffn_w_out"][l], p["norm_post_ffn"][l])
        bufs.append(buf)
        vns.append(vn.reshape(bsz, t, A_WIDTH))
        h3 = h.reshape(bsz, t, D_IN)
        kvs.append([_split_heads(h3[:, :, off + g * C_GROUP_WIDTH:off + (g + 1) * C_GROUP_WIDTH])
                    for g in range(C_GROUPS) for off in (OFF_CK, OFF_CV)])
    new_kv = [jnp.stack([kvs[l][j] for l in range(depth)], axis=0) for j in range(2 * C_GROUPS)]
    return x.reshape(bsz, t, D_MODEL), jnp.stack(bufs, axis=0), jnp.stack(vns, axis=0), new_kv


def kernel(x_prompt, x_sample, state_b_conv, cache_c0_k, cache_c0_v, cache_c1_k, cache_c1_v, cache_c2_k,
           cache_c2_v, norm_pre_mix, norm_post_mix, norm_pre_ffn, norm_post_ffn, w_in, a_norm_g, a_norm_b,
           a_w_s, a_b_s, b_conv_w, b_conv_b, b_norm_g, b_norm_b, w_branch, w_out, ffn_w_in, ffn_w_out):
    assert x_sample.shape[1] == DEC_SEQ
    p = dict(norm_pre_mix=norm_pre_mix, norm_post_mix=norm_post_mix, norm_pre_ffn=norm_pre_ffn,
             norm_post_ffn=norm_post_ffn, w_in=w_in, a_norm_g=a_norm_g, a_norm_b=a_norm_b,
             a_w_s=a_w_s, a_b_s=a_b_s, b_conv_w=b_conv_w, b_conv_b=b_conv_b, b_norm_g=b_norm_g,
             b_norm_b=b_norm_b, w_branch=w_branch.astype(BF16), w_out=w_out.astype(BF16),
             ffn_w_in=ffn_w_in.astype(BF16), ffn_w_out=ffn_w_out.astype(BF16))
    caches = (cache_c0_k, cache_c0_v, cache_c1_k, cache_c1_v, cache_c2_k, cache_c2_v)
    y_prompt, buf_p, kv_p = _prompt_trunk(x_prompt, p)
    y_sample, buf_s, vn_s, kv_s = _sample_trunk(x_sample, state_b_conv, caches, p)
    return (y_prompt, y_sample, buf_p, buf_s, vn_s, *kv_p, *kv_s)
```

```python
import functools

import jax
import jax.numpy as jnp
from jax import lax
from jax.experimental import pallas as pl
from jax.experimental.pallas import tpu as pltpu

F32 = jnp.float32
BF16 = jnp.bfloat16

D_MODEL = 1024
DEC_SEQ = 8
A_WIDTH = 512
A_GROUPS = 4
A_GROUP_DIM = 128
CHUNK = 128
B_WIDTH = 512
CONV_W = 31
CONV_HALO = 32
C_HEADS = 8
C_HEAD_DIM = 64
C_GROUP_WIDTH = 512
C_CONFIGS = ((128, 1), (512, 4), (2048, 16))
C_GROUPS = 3
C_WINDOW = 128
N_BRANCH = 3
BRANCH_WIDTH = 512
D_FF = 2816
EPS = 1e-6
NEG_INF = -1e30
ATTN_SCALE = C_HEAD_DIM ** -0.5
REF_OFF_G = 6656
OFF_G = 0
OFF_AU = 3072
OFF_AV = 3584
OFF_B = 4096
OFF_CQ = 5120
OFF_CK = 6656
OFF_CV = 8192
D_IN = 9728
COL_BLOCKS = D_IN // 512

VMEM_LIMIT = 56 * 1024 * 1024


def _slope(g, h):
    n = C_GROUPS * C_HEADS
    return 2.0 ** (-8.0 * (h * C_GROUPS + g + 1.0) / n)


def _params(*sem):
    return pltpu.CompilerParams(dimension_semantics=sem, vmem_limit_bytes=VMEM_LIMIT)


def _rms(x, gain):
    return x * lax.rsqrt(jnp.mean(x * x, axis=-1, keepdims=True) + EPS) * gain


def _layer_norm(x, gain, bias):
    mu = jnp.mean(x, axis=-1, keepdims=True)
    xc = x - mu
    var = jnp.mean(xc * xc, axis=-1, keepdims=True)
    return xc * lax.rsqrt(var + EPS) * gain + bias


Q_TILE0 = OFF_CQ // 512
KV_TILE0 = OFF_CK // 512
KV_WIDTH = D_IN - OFF_CK
TAIL = 2048


def _in_proj_sample_body(x_ref, g_ref, w_ref, o_ref, xn_ref):
    @pl.when(pl.program_id(1) == 0)
    def _():
        xn_ref[...] = _rms(x_ref[...], g_ref[...]).astype(BF16)

    o_ref[...] = jnp.dot(xn_ref[...], w_ref[...].astype(BF16), preferred_element_type=F32)


def _w_in_tile(j):
    return lax.rem(j + REF_OFF_G // 512, COL_BLOCKS)


def _in_proj_sample(x, gain, w_in, layer):
    m = x.shape[0]
    tn = 512
    return pl.pallas_call(
        _in_proj_sample_body,
        grid=(1, D_IN // tn),
        in_specs=[pl.BlockSpec((m, D_MODEL), lambda i, j: (0, 0)),
                  pl.BlockSpec((1, D_MODEL), lambda i, j: (0, 0)),
                  pl.BlockSpec((None, D_MODEL, tn), lambda i, j: (layer, 0, _w_in_tile(j)))],
        out_specs=pl.BlockSpec((m, tn), lambda i, j: (0, j)),
        out_shape=jax.ShapeDtypeStruct((m, D_IN), F32),
        scratch_shapes=[pltpu.VMEM((m, D_MODEL), BF16)],
        compiler_params=_params("arbitrary", "arbitrary"),
        name="in_proj_sample",
    )(x, gain.reshape(1, D_MODEL), w_in)


def _in_proj_body(x_hbm, g_ref, w_ref, o_ref, kv_ref, xn_ref, res_ref, tmp_ref, xbuf_ref, xsem, *, tm, tiles_per_seq,
                  n_tiles):
    i = pl.program_id(0)
    j = pl.program_id(1)
    lanes = res_ref.shape[1]

    def matmul_tile(slot):
        res = jnp.dot(xn_ref[...], w_ref[...].astype(BF16), preferred_element_type=F32)
        for c in range(lanes):
            res_ref[slot, c] = res[:, c * 128:(c + 1) * 128]

    def emit(slot, d):
        if d == 1:
            for c in range(lanes):
                o_ref[:, c * 128:(c + 1) * 128] = res_ref[slot, c].astype(BF16)
            return
        if d == 16:
            quarter = tm // 4
            for b4 in range(4):
                for c in range(lanes):
                    tmp_ref[c, b4 * quarter:(b4 + 1) * quarter, :] = res_ref[slot, c, pl.ds(b4, quarter, stride=4), :]
            for r in range(d):
                for c in range(lanes):
                    o_ref[r * C_WINDOW:(r + 1) * C_WINDOW, c * 128:(c + 1) * 128] = (
                        tmp_ref[c, pl.ds((r % 4) * quarter + r // 4, C_WINDOW, stride=4), :].astype(BF16))
            return
        for u in range(tm // (C_WINDOW * d)):
            for r in range(d):
                dst = u * C_WINDOW * d + r * C_WINDOW
                for c in range(lanes):
                    o_ref[dst:dst + C_WINDOW, c * 128:(c + 1) * 128] = (
                        res_ref[slot, c, pl.ds(u * C_WINDOW * d + r, C_WINDOW, stride=d), :].astype(BF16))

    def x_copy(tile, slot):
        return pltpu.make_async_copy(x_hbm.at[pl.ds(tile * tm, tm), :], xbuf_ref.at[slot], xsem.at[slot])

    @pl.when(jnp.logical_and(i == 0, j == 0))
    def _():
        x_copy(0, 0).start()

    @pl.when(j == 0)
    def _():
        slot = lax.rem(i, 2)
        x_copy(i, slot).wait()
        xn_ref[...] = _rms(xbuf_ref[slot], g_ref[...]).astype(BF16)
        matmul_tile(0)

    @pl.when(jnp.logical_and(j == 1, i + 1 < n_tiles))
    def _():
        x_copy(i + 1, lax.rem(i + 1, 2)).start()

    prev_group = lax.rem(jnp.maximum(j - 1 - Q_TILE0, 0), C_GROUPS)
    prev_dilation_group = jnp.where(j - 1 >= Q_TILE0, prev_group, 0)
    in_range = jnp.logical_and(j >= 1, j < COL_BLOCKS)
    for parity in range(2):
        for g, (_, d) in enumerate(C_CONFIGS):
            @pl.when(jnp.logical_and(jnp.logical_and(in_range, lax.rem(j, 2) == parity), prev_dilation_group == g))
            def _():
                emit(1 - parity, d)
                matmul_tile(parity)

    @pl.when(j == COL_BLOCKS)
    def _():
        emit((COL_BLOCKS - 1) % 2, C_CONFIGS[(COL_BLOCKS - 1 - Q_TILE0) % C_GROUPS][1])

    @pl.when(jnp.logical_and(i % tiles_per_seq == tiles_per_seq - 1,
                             jnp.logical_and(j >= KV_TILE0, j < COL_BLOCKS)))
    def _():
        for c in range(lanes):
            kv_ref[:, c * 128:(c + 1) * 128] = res_ref[lax.rem(j, 2), c]


def _in_proj(x, gain, w_in, layer, bsz):
    m = x.shape[0]
    t = m // bsz
    tm = TAIL
    tn = 512
    assert t % tm == 0
    tiles_per_seq = t // tm
    last = COL_BLOCKS - 1

    def kv_index(i, j):
        is_tail = i % tiles_per_seq == tiles_per_seq - 1
        return i // tiles_per_seq, jnp.where(is_tail, jnp.clip(j - KV_TILE0, 0, last - KV_TILE0), 0)

    return pl.pallas_call(
        functools.partial(_in_proj_body, tm=tm, tiles_per_seq=tiles_per_seq, n_tiles=m // tm),
        grid=(m // tm, COL_BLOCKS + 1),
        in_specs=[pl.BlockSpec(memory_space=pl.ANY),
                  pl.BlockSpec((1, D_MODEL), lambda i, j: (0, 0)),
                  pl.BlockSpec((None, D_MODEL, tn), lambda i, j: (layer, 0, _w_in_tile(jnp.minimum(j, last))))],
        out_specs=[pl.BlockSpec((tm, tn), lambda i, j: (i, jnp.maximum(j - 1, 0))),
                   pl.BlockSpec((tm, tn), kv_index)],
        out_shape=[jax.ShapeDtypeStruct((m, D_IN), BF16),
                   jax.ShapeDtypeStruct((bsz * TAIL, KV_WIDTH), F32)],
        scratch_shapes=[pltpu.VMEM((tm, D_MODEL), BF16), pltpu.VMEM((2, tn // 128, tm, 128), F32),
                        pltpu.VMEM((tn // 128, tm, 128), F32),
                        pltpu.VMEM((2, tm, D_MODEL), F32), pltpu.SemaphoreType.DMA((2,))],
        compiler_params=_params("arbitrary", "arbitrary"),
        name="in_proj",
    )(x, gain.reshape(1, D_MODEL), w_in)


def _token_mix_sample_body(u_ref, v_ref, ng_ref, nb_ref, w_ref, bias_ref, o_ref, vn_ref, *, bsz):
    vn = _layer_norm(v_ref[...], ng_ref[...], nb_ref[...])
    vn_ref[...] = vn
    vn3 = vn.reshape(bsz, DEC_SEQ, A_WIDTH)
    t_idx = lax.broadcasted_iota(jnp.int32, (DEC_SEQ, A_WIDTH), 0)
    mix = jnp.broadcast_to(bias_ref[...], (bsz, DEC_SEQ, A_WIDTH))
    for s in range(DEC_SEQ):
        w_s = jnp.where(t_idx >= s, w_ref[s], 0.0)
        mix = mix + jnp.broadcast_to(vn3[:, s:s + 1, :], (bsz, DEC_SEQ, A_WIDTH)) * w_s
    o_ref[...] = u_ref[...] * mix.reshape(bsz * DEC_SEQ, A_WIDTH)


def _token_mix_sample(h, norm_g, norm_b, w_s, b_s):
    m = h.shape[0]
    bsz = m // DEC_SEQ
    w = jnp.repeat(jnp.transpose(w_s[:, :DEC_SEQ, :DEC_SEQ], (2, 1, 0)), A_GROUP_DIM, axis=2)
    bias = jnp.repeat(b_s[:, :DEC_SEQ].T, A_GROUP_DIM, axis=1)
    vec = pl.BlockSpec((1, A_WIDTH), lambda i: (0, 0))
    return pl.pallas_call(
        functools.partial(_token_mix_sample_body, bsz=bsz),
        grid=(1,),
        in_specs=[pl.BlockSpec((m, A_WIDTH), lambda i: (0, OFF_AU // A_WIDTH)),
                  pl.BlockSpec((m, A_WIDTH), lambda i: (0, OFF_AV // A_WIDTH)),
                  vec, vec,
                  pl.BlockSpec((DEC_SEQ, DEC_SEQ, A_WIDTH), lambda i: (0, 0, 0)),
                  pl.BlockSpec((DEC_SEQ, A_WIDTH), lambda i: (0, 0))],
        out_specs=[pl.BlockSpec((m, A_WIDTH), lambda i: (0, 0))] * 2,
        out_shape=[jax.ShapeDtypeStruct((m, A_WIDTH), F32)] * 2,
        compiler_params=_params("arbitrary"),
        name="token_mix_sample",
    )(h, h, norm_g.reshape(1, A_WIDTH), norm_b.reshape(1, A_WIDTH), w, bias)


def _glu(z):
    return z[:, :B_WIDTH] * jax.nn.sigmoid(z[:, B_WIDTH:])


def _local_mixers_body(u_ref, v_ref, ng_ref, nb_ref, w_ref, bias_ref, z_ref, zh_ref, cw_ref, cb_ref, lg_ref, lb_ref,
                       oa_ref, ob_ref, buf_ref, pad_ref, cw8_ref, *, tr, tiles):
    row = lax.broadcasted_iota(jnp.int32, (CHUNK, CHUNK), 0)
    col = lax.broadcasted_iota(jnp.int32, (CHUNK, CHUNK), 1)
    wm = [jnp.where(col <= row, w_ref[g], 0.0).astype(BF16) for g in range(A_GROUPS)]
    for c in range(tr // CHUNK):
        rows = slice(c * CHUNK, (c + 1) * CHUNK)
        vnb = _layer_norm(v_ref[rows, :].astype(F32), ng_ref[...], nb_ref[...]).astype(BF16)
        for g in range(A_GROUPS):
            lanes = slice(g * A_GROUP_DIM, (g + 1) * A_GROUP_DIM)
            mix = jnp.dot(wm[g], vnb[:, lanes], preferred_element_type=F32) + bias_ref[:, lanes]
            oa_ref[rows, lanes] = (u_ref[rows, lanes].astype(F32) * mix).astype(oa_ref.dtype)

    @pl.when(pl.program_id(0) == 0)
    def _():
        for k in range(CONV_W):
            cw8_ref[k] = jnp.broadcast_to(cw_ref[k:k + 1, :], (8, B_WIDTH))

    t = pl.program_id(0) % tiles
    hg = _glu(z_ref[...].astype(F32))
    pad_ref[0:CONV_HALO, :] = jnp.where(t == 0, 0.0, _glu(zh_ref[...].astype(F32)))
    pad_ref[CONV_HALO:CONV_HALO + tr, :] = hg
    first = CONV_HALO - (CONV_W - 1)
    sub = 32
    taps = [[k for k in range(CONV_W) if (first + k) % 8 == s] for s in range(8)]
    for blk in range(tr // sub):
        acc = jnp.zeros((sub // 8, 8, B_WIDTH), F32)
        for s in range(8):
            q_max = max((first + k) // 8 for k in taps[s])
            win = pad_ref[blk * sub + s:blk * sub + s + sub + 8 * q_max, :]
            for k in taps[s]:
                q = (first + k) // 8
                acc = acc + win[8 * q:8 * q + sub, :].reshape(sub // 8, 8, B_WIDTH) * cw8_ref[k]
        y = _layer_norm(acc.reshape(sub, B_WIDTH) + cb_ref[...], lg_ref[...], lb_ref[...])
        ob_ref[blk * sub:(blk + 1) * sub, :] = (y * jax.nn.sigmoid(y)).astype(ob_ref.dtype)

    @pl.when(t == tiles - 1)
    def _():
        buf_ref[...] = pad_ref[CONV_HALO + tr - (CONV_W - 1):CONV_HALO + tr, :]


def _local_mixers(h, bsz, norm_g, norm_b, w_s, b_s, conv_w, conv_b, ln_g, ln_b):
    m = h.shape[0]
    t = m // bsz
    tr = min(512, t)
    tiles = t // tr
    halo_per_tile = tr // CONV_HALO
    bias = jnp.repeat(b_s.T, A_GROUP_DIM, axis=1)
    vec = pl.BlockSpec((1, B_WIDTH), lambda i: (0, 0))
    body = functools.partial(_local_mixers_body, tr=tr, tiles=tiles)
    return pl.pallas_call(
        body,
        grid=(m // tr,),
        in_specs=[pl.BlockSpec((tr, A_WIDTH), lambda i: (i, OFF_AU // A_WIDTH)),
                  pl.BlockSpec((tr, A_WIDTH), lambda i: (i, OFF_AV // A_WIDTH)),
                  vec, vec,
                  pl.BlockSpec((A_GROUPS, CHUNK, CHUNK), lambda i: (0, 0, 0)),
                  pl.BlockSpec((CHUNK, A_WIDTH), lambda i: (0, 0)),
                  pl.BlockSpec((tr, 2 * B_WIDTH), lambda i: (i, OFF_B // (2 * B_WIDTH))),
                  pl.BlockSpec((CONV_HALO, 2 * B_WIDTH),
                               lambda i: (jnp.maximum(i * halo_per_tile - 1, 0), OFF_B // (2 * B_WIDTH))),
                  pl.BlockSpec((CONV_W, B_WIDTH), lambda i: (0, 0)),
                  vec, vec, vec],
        out_specs=[pl.BlockSpec((tr, A_WIDTH), lambda i: (i, 0)),
                   pl.BlockSpec((tr, B_WIDTH), lambda i: (i, 0)),
                   pl.BlockSpec((None, CONV_W - 1, B_WIDTH), lambda i: (i // tiles, 0, 0))],
        out_shape=[jax.ShapeDtypeStruct((m, A_WIDTH), BF16),
                   jax.ShapeDtypeStruct((m, B_WIDTH), BF16),
                   jax.ShapeDtypeStruct((bsz, CONV_W - 1, B_WIDTH), F32)],
        scratch_shapes=[pltpu.VMEM((CONV_HALO + tr, B_WIDTH), F32), pltpu.VMEM((CONV_W, 8, B_WIDTH), F32)],
        compiler_params=_params("arbitrary"),
        name="local_mixers",
    )(h, h, norm_g.reshape(1, A_WIDTH), norm_b.reshape(1, A_WIDTH), w_s, bias, h, h, conv_w, conv_b.reshape(1, B_WIDTH),
      ln_g.reshape(1, B_WIDTH), ln_b.reshape(1, B_WIDTH))


def _conv_sample_body(z_ref, st_ref, cw_ref, cb_ref, lg_ref, lb_ref, o_ref, buf_ref, pad_ref, *, bsz):
    hg = _glu(z_ref[...])
    pad_ref[:, 0:CONV_W - 1, :] = st_ref[...]
    pad_ref[:, CONV_W - 1:CONV_W - 1 + DEC_SEQ, :] = hg.reshape(bsz, DEC_SEQ, B_WIDTH)
    acc = jnp.zeros((bsz, DEC_SEQ, B_WIDTH), F32)
    for k in range(CONV_W):
        acc = acc + cw_ref[k:k + 1, :] * pad_ref[:, k:k + DEC_SEQ, :]
    y = _layer_norm(acc + cb_ref[...], lg_ref[...], lb_ref[...])
    o_ref[...] = (y * jax.nn.sigmoid(y)).reshape(bsz * DEC_SEQ, B_WIDTH)
    buf_ref[...] = pad_ref[:, DEC_SEQ:DEC_SEQ + CONV_W - 1, :]


def _conv_sample(h, state, conv_w, conv_b, ln_g, ln_b):
    m = h.shape[0]
    bsz = state.shape[0]
    body = functools.partial(_conv_sample_body, bsz=bsz)
    return pl.pallas_call(
        body,
        grid=(1,),
        in_specs=[pl.BlockSpec((m, 2 * B_WIDTH), lambda i: (0, OFF_B // (2 * B_WIDTH))),
                  pl.BlockSpec((bsz, CONV_W - 1, B_WIDTH), lambda i: (0, 0, 0)),
                  pl.BlockSpec((CONV_W, B_WIDTH), lambda i: (0, 0)),
                  pl.BlockSpec((1, B_WIDTH), lambda i: (0, 0)),
                  pl.BlockSpec((1, B_WIDTH), lambda i: (0, 0)),
                  pl.BlockSpec((1, B_WIDTH), lambda i: (0, 0))],
        out_specs=[pl.BlockSpec((m, B_WIDTH), lambda i: (0, 0)),
                   pl.BlockSpec((bsz, CONV_W - 1, B_WIDTH), lambda i: (0, 0, 0))],
        out_shape=[jax.ShapeDtypeStruct((m, B_WIDTH), F32),
                   jax.ShapeDtypeStruct((bsz, CONV_W - 1, B_WIDTH), F32)],
        scratch_shapes=[pltpu.VMEM((bsz, CONV_W - 1 + DEC_SEQ, B_WIDTH), F32)],
        compiler_params=_params("arbitrary"),
        name="conv_sample",
    )(h, state, conv_w, conv_b.reshape(1, B_WIDTH), ln_g.reshape(1, B_WIDTH), ln_b.reshape(1, B_WIDTH))


def _dot_nt(a, b):
    return lax.dot_general(a, b, (((1,), (1,)), ((), ())), preferred_element_type=F32)


ATTN_TILE = 2048
ATTN_ITEMS = 8


def _attn_prompt_body(slope_ref, *refs):
    group_refs = [refs[5 * g:5 * g + 5] for g in range(C_GROUPS)]
    o_ref, og_ref, lg_ref, sm_ref, sx_ref, pp_ref, pc_ref, mb_ref = refs[5 * C_GROUPS:]
    tq = C_WINDOW
    first_tile = pl.program_id(1) == 0
    hp = pl.program_id(2)
    row = lax.broadcasted_iota(jnp.int32, (tq, tq), 0)
    col = lax.broadcasted_iota(jnp.int32, (tq, tq), 1)
    upper = col > row
    diag = col == row
    diag0 = col == row + jnp.where(first_tile, tq, 0)
    valid0 = col <= row + jnp.where(first_tile, 0, tq)
    dist = jnp.where(upper, row - col + tq, row - col).astype(F32)
    low = lax.broadcasted_iota(jnp.int32, (tq, 128), 1) < C_HEAD_DIM
    sels = (low, jnp.logical_not(low))

    for g, (_, d) in enumerate(C_CONFIGS):
        q_ref, kp_ref, kc_ref, vp_ref, vc_ref = group_refs[g]
        slopes = [slope_ref[g * C_HEADS + 2 * hp + hh] * float(d) for hh in range(2)]
        bias = [-(s * dist) for s in slopes]
        bias_x = [-(s * float(tq)) for s in slopes]
        blocks = [(u, r) for u in range(ATTN_TILE // (tq * d)) for r in range(d)]

        def rows(u, r):
            return pl.ds(u * tq * d + r * tq, tq)

        def out_rows(u, r):
            return pl.ds(u * tq * d + r, tq, stride=d) if d > 1 else pl.ds(u * tq, tq)

        for first in range(0, len(blocks), ATTN_ITEMS):
            items = blocks[first:first + ATTN_ITEMS]
            for n, (u, r) in enumerate(items):
                q2 = q_ref[rows(u, r), :] * ATTN_SCALE
                kc2 = kc_ref[rows(u, r), :]
                kp2 = kp_ref[rows(0, r), :] if u == 0 else kc_ref[rows(u - 1, r), :]
                for hh in range(2):
                    qm = jnp.where(sels[hh], q2, 0.0)
                    sp = _dot_nt(qm, kp2)
                    sm = jnp.where(upper, sp, _dot_nt(qm, kc2)) + bias[hh]
                    if u == 0:
                        sm = jnp.where(valid0, sm, NEG_INF)
                    sm_ref[2 * n + hh] = sm
                    sx_ref[2 * n + hh] = jnp.where(diag0 if u == 0 else diag, sp + bias_x[hh], NEG_INF)
            for n in range(len(items)):
                mx = []
                for hh in range(2):
                    sm = sm_ref[2 * n + hh]
                    sx = sx_ref[2 * n + hh]
                    m = jnp.max(jnp.maximum(sm, sx), axis=-1, keepdims=True)
                    p = jnp.exp(sm - m)
                    pp_ref[2 * n + hh] = jnp.where(upper, p, jnp.exp(sx - m)).astype(BF16)
                    pc_ref[2 * n + hh] = jnp.where(upper, 0.0, p).astype(BF16)
                    mx.append(m)
                mb_ref[n] = jnp.where(low, mx[0], mx[1])
            for n, (u, r) in enumerate(items):
                vc2 = vc_ref[rows(u, r), :]
                vp2 = vp_ref[rows(0, r), :] if u == 0 else vc_ref[rows(u - 1, r), :]
                acc = []
                for hh in range(2):
                    vpx = jnp.where(sels[hh], vp2, 1.0)
                    vcx = jnp.where(sels[hh], vc2, 1.0)
                    acc.append(jnp.dot(pp_ref[2 * n + hh], vpx, preferred_element_type=F32)
                               + jnp.dot(pc_ref[2 * n + hh], vcx, preferred_element_type=F32))
                den = pltpu.roll(jnp.where(low, acc[1], acc[0]), C_HEAD_DIM, axis=1)
                og_ref[g, out_rows(u, r), :] = jnp.where(low, acc[0], acc[1]) / den
                lg_ref[g, out_rows(u, r), :] = mb_ref[n] + jnp.log(den)

    chunk = 256
    for c in range(ATTN_TILE // chunk):
        rs = slice(c * chunk, (c + 1) * chunk)
        l0, l1, l2 = lg_ref[0, rs, :], lg_ref[1, rs, :], lg_ref[2, rs, :]
        mx = jnp.maximum(jnp.maximum(l0, l1), l2)
        e0, e1, e2 = jnp.exp(l0 - mx), jnp.exp(l1 - mx), jnp.exp(l2 - mx)
        merged = e0 * og_ref[0, rs, :] + e1 * og_ref[1, rs, :] + e2 * og_ref[2, rs, :]
        o_ref[rs, :] = (merged / (e0 + e1 + e2)).astype(o_ref.dtype)


def _attn_prompt(h, bsz):
    m = h.shape[0]
    t = m // bsz
    assert t % ATTN_TILE == 0
    tiles = t // ATTN_TILE
    in_specs = [pl.BlockSpec(memory_space=pltpu.SMEM)]
    for g, (_, d) in enumerate(C_CONFIGS):
        unit_rows = C_WINDOW * d
        units = ATTN_TILE // unit_rows
        cq, ck, cv = ((off + g * C_GROUP_WIDTH) // 128 for off in (OFF_CQ, OFF_CK, OFF_CV))

        def cur(c):
            return pl.BlockSpec((ATTN_TILE, 128), lambda b, i, hp, c=c: (b * tiles + i, c + hp))

        def prev(c, units=units, unit_rows=unit_rows):
            return pl.BlockSpec((unit_rows, 128),
                                lambda b, i, hp, c=c: (jnp.maximum((b * tiles + i) * units - 1, 0), c + hp))

        in_specs += [cur(cq), prev(ck), cur(ck), prev(cv), cur(cv)]
    slopes = jnp.array([_slope(gg, hh) for gg in range(C_GROUPS) for hh in range(C_HEADS)], F32)
    n_items = 2 * ATTN_ITEMS
    return pl.pallas_call(
        _attn_prompt_body,
        grid=(bsz, tiles, C_HEADS // 2),
        in_specs=in_specs,
        out_specs=pl.BlockSpec((ATTN_TILE, 128), lambda b, i, hp: (b * tiles + i, hp)),
        out_shape=jax.ShapeDtypeStruct((m, C_GROUP_WIDTH), BF16),
        scratch_shapes=[pltpu.VMEM((C_GROUPS, ATTN_TILE, 128), F32)] * 2
        + [pltpu.VMEM((n_items, C_WINDOW, C_WINDOW), F32)] * 2
        + [pltpu.VMEM((n_items, C_WINDOW, C_WINDOW), BF16)] * 2
        + [pltpu.VMEM((n_items // 2, C_WINDOW, 128), F32)],
        compiler_params=_params("arbitrary", "arbitrary", "arbitrary"),
        name="attn_prompt",
    )(slopes, *([h] * (5 * C_GROUPS)))


def _attn_sample_body(qkv_ref, slope_ref, k0_ref, v0_ref, k1_ref, v1_ref, k2_ref, v2_ref, o_ref):
    caches = ((k0_ref, v0_ref), (k1_ref, v1_ref), (k2_ref, v2_ref))
    rows = C_HEADS * DEC_SEQ
    rid = lax.broadcasted_iota(jnp.int32, (rows, 1), 0)
    i_row = rid & (DEC_SEQ - 1)
    own_lanes = (lax.broadcasted_iota(jnp.int32, (rows, C_GROUP_WIDTH), 1) >> 6) == (
        lax.broadcasted_iota(jnp.int32, (rows, C_GROUP_WIDTH), 0) >> 3)
    outs, lses = [], []
    for g, (w, d) in enumerate(C_CONFIGS):
        kt_ref, vt_ref = caches[g]
        q = qkv_ref[:, g * 512:(g + 1) * 512] * ATTN_SCALE
        kn = qkv_ref[:, (3 + g) * 512:(4 + g) * 512]
        vn = qkv_ref[:, (6 + g) * 512:(7 + g) * 512]
        qbd = jnp.where(own_lanes, jnp.concatenate([q] * C_HEADS, axis=0), 0.0)
        slope = slope_ref[g, :, 0:1]
        kt = kt_ref[...].reshape(C_GROUP_WIDTH, w).astype(BF16)
        vt = vt_ref[...].reshape(C_GROUP_WIDTH, w).astype(BF16)
        pos = lax.broadcasted_iota(jnp.int32, (rows, w), 1)
        back = (w + i_row) - pos
        ok = (pos >= i_row) if d == 1 else ((back & (d - 1)) == 0) & (pos >= i_row)
        s_c = jnp.dot(qbd.astype(BF16), kt, preferred_element_type=F32)
        s_c = jnp.where(ok, s_c - slope * back.astype(F32), NEG_INF)
        s_n = []
        for i2 in range(DEC_SEQ):
            s = jnp.sum(qbd * kn[i2:i2 + 1, :], axis=-1, keepdims=True)
            back_n = i_row - i2
            ok_n = (back_n >= 0) if d == 1 else ((back_n & (d - 1)) == 0) & (back_n >= 0)
            s_n.append(jnp.where(ok_n, s - slope * back_n.astype(F32), NEG_INF))
        mx = jnp.max(s_c, axis=-1, keepdims=True)
        for s in s_n:
            mx = jnp.maximum(mx, s)
        p_c = jnp.exp(s_c - mx)
        den = jnp.sum(p_c, axis=-1, keepdims=True)
        acc = _dot_nt(p_c.astype(BF16), vt)
        for i2, s in enumerate(s_n):
            p = jnp.exp(s - mx)
            den = den + p
            acc = acc + p * vn[i2:i2 + 1, :]
        outs.append(acc / den)
        lses.append(mx + jnp.log(den))
    mx = jnp.maximum(jnp.maximum(lses[0], lses[1]), lses[2])
    e = [jnp.exp(l - mx) for l in lses]
    tot = e[0] + e[1] + e[2]
    merged = (e[0] / tot) * outs[0] + (e[1] / tot) * outs[1] + (e[2] / tot) * outs[2]
    merged = jnp.where(own_lanes, merged, 0.0).reshape(C_HEADS, DEC_SEQ, C_GROUP_WIDTH)
    o_ref[...] = jnp.sum(merged, axis=0)


def _attn_sample(h, caches, layer):
    m = h.shape[0]
    bsz = m // DEC_SEQ
    qkv = h[:, OFF_CQ:D_IN]
    slopes = jnp.array([[_slope(g, hh) for hh in range(C_HEADS)] for g in range(C_GROUPS)], F32)
    slopes = jnp.broadcast_to(jnp.repeat(slopes, DEC_SEQ, axis=1)[:, :, None], (C_GROUPS, C_HEADS * DEC_SEQ, 128))
    in_specs = [pl.BlockSpec((DEC_SEQ, D_IN - OFF_CQ), lambda b: (b, 0)),
                pl.BlockSpec((C_GROUPS, C_HEADS * DEC_SEQ, 128), lambda b: (0, 0, 0))]
    views = []
    for g, (w, _) in enumerate(C_CONFIGS):
        for j in range(2):
            c = caches[2 * g + j]
            assert c.shape[2] == w, "the cache must hold the last W_g positions"
            views.append(jnp.transpose(c, (0, 1, 3, 4, 2)))
            in_specs.append(pl.BlockSpec((None, None, C_HEADS, C_HEAD_DIM, w), lambda b: (layer, b, 0, 0, 0)))
    return pl.pallas_call(
        _attn_sample_body,
        grid=(bsz,),
        in_specs=in_specs,
        out_specs=pl.BlockSpec((DEC_SEQ, C_GROUP_WIDTH), lambda b: (b, 0)),
        out_shape=jax.ShapeDtypeStruct((m, C_GROUP_WIDTH), F32),
        compiler_params=_params("arbitrary"),
        name="attn_sample",
    )(qkv, slopes, *views)


def _merge_out_body(x_ref, oa_ref, ob_ref, oc_ref, gate_ref, wb_ref, wo_ref, g_ref, out_ref):
    merged = None
    for n, o_ref in enumerate((oa_ref, ob_ref, oc_ref)):
        proj = jnp.dot(o_ref[...].astype(BF16), wb_ref[n * BRANCH_WIDTH:(n + 1) * BRANCH_WIDTH, :],
                       preferred_element_type=F32)
        term = jax.nn.sigmoid(gate_ref[:, n * D_MODEL:(n + 1) * D_MODEL].astype(F32)) * proj
        merged = term if merged is None else merged + term
    y = jnp.dot(merged.astype(BF16), wo_ref[...], preferred_element_type=F32)
    out_ref[...] = x_ref[...] + _rms(y, g_ref[...])


def _merge_out(x, h, o_a, o_b, o_c, w_branch_bf16, w_out_bf16, gain):
    m = x.shape[0]
    tm = min(m, 512)
    row512 = pl.BlockSpec((tm, BRANCH_WIDTH), lambda i: (i, 0))
    return pl.pallas_call(
        _merge_out_body,
        grid=(m // tm,),
        in_specs=[pl.BlockSpec((tm, D_MODEL), lambda i: (i, 0)), row512, row512, row512,
                  pl.BlockSpec((tm, N_BRANCH * D_MODEL), lambda i: (i, OFF_G // (N_BRANCH * D_MODEL))),
                  pl.BlockSpec((N_BRANCH * BRANCH_WIDTH, D_MODEL), lambda i: (0, 0)),
                  pl.BlockSpec((D_MODEL, D_MODEL), lambda i: (0, 0)),
                  pl.BlockSpec((1, D_MODEL), lambda i: (0, 0))],
        out_specs=pl.BlockSpec((tm, D_MODEL), lambda i: (i, 0)),
        out_shape=jax.ShapeDtypeStruct((m, D_MODEL), F32),
        compiler_params=_params("arbitrary"),
        name="merge_out",
    )(x, o_a, o_b, o_c, h, w_branch_bf16, w_out_bf16, gain.reshape(1, D_MODEL))


FF_CHUNK = D_FF // 2


def _ffn_body(x_ref, g1_ref, w1_ref, w2_ref, g2_ref, out_ref):
    x = x_ref[...]
    xn = _rms(x, g1_ref[...]).astype(BF16)
    y = None
    for c in range(D_FF // FF_CHUNK):
        gate = jnp.dot(xn, w1_ref[:, c * FF_CHUNK:(c + 1) * FF_CHUNK], preferred_element_type=F32)
        up = jnp.dot(xn, w1_ref[:, D_FF + c * FF_CHUNK:D_FF + (c + 1) * FF_CHUNK], preferred_element_type=F32)
        act = (gate * jax.nn.sigmoid(gate) * up).astype(BF16)
        part = jnp.dot(act, w2_ref[c * FF_CHUNK:(c + 1) * FF_CHUNK, :], preferred_element_type=F32)
        y = part if y is None else y + part
    out_ref[...] = x + _rms(y, g2_ref[...])


def _ffn(x, gain_pre, w1_bf16, w2_bf16, gain_post):
    m = x.shape[0]
    tm = min(m, 512)
    resident = dict(pipeline_mode=pl.Buffered(1))
    return pl.pallas_call(
        _ffn_body,
        grid=(m // tm,),
        in_specs=[pl.BlockSpec((tm, D_MODEL), lambda i: (i, 0)),
                  pl.BlockSpec((1, D_MODEL), lambda i: (0, 0)),
                  pl.BlockSpec((D_MODEL, 2 * D_FF), lambda i: (0, 0), **resident),
                  pl.BlockSpec((D_FF, D_MODEL), lambda i: (0, 0), **resident),
                  pl.BlockSpec((1, D_MODEL), lambda i: (0, 0))],
        out_specs=pl.BlockSpec((tm, D_MODEL), lambda i: (i, 0)),
        out_shape=jax.ShapeDtypeStruct((m, D_MODEL), F32),
        compiler_params=_params("arbitrary"),
        name="ffn",
    )(x, gain_pre.reshape(1, D_MODEL), w1_bf16, w2_bf16, gain_post.reshape(1, D_MODEL))


def _split_heads(cols):
    return cols.reshape(cols.shape[0], cols.shape[1], C_HEADS, C_HEAD_DIM)


def _prompt_trunk(x3, p):
    bsz, t, _ = x3.shape
    x = x3.reshape(bsz * t, D_MODEL)
    depth = p["w_in"].shape[0]
    bufs, kvs = [], []
    for l in range(depth):
        h, kv_tail = _in_proj(x, p["norm_pre_mix"][l], p["w_in"], l, bsz)
        o_a, o_b, buf = _local_mixers(h, bsz, p["a_norm_g"][l], p["a_norm_b"][l], p["a_w_s"][l], p["a_b_s"][l],
                                      p["b_conv_w"][l], p["b_conv_b"][l], p["b_norm_g"][l], p["b_norm_b"][l])
        o_c = _attn_prompt(h, bsz)
        x = _merge_out(x, h, o_a, o_b, o_c, p["w_branch"][l], p["w_out"][l], p["norm_post_mix"][l])
        x = _ffn(x, p["norm_pre_ffn"][l], p["ffn_w_in"][l], p["ffn_w_out"][l], p["norm_post_ffn"][l])
        bufs.append(buf)
        kv3 = kv_tail.reshape(bsz, TAIL, KV_WIDTH)
        layer_kv = []
        for g, (w, _) in enumerate(C_CONFIGS):
            keep = min(w, t)
            for off in (g * C_GROUP_WIDTH, C_GROUPS * C_GROUP_WIDTH + g * C_GROUP_WIDTH):
                layer_kv.append(_split_heads(kv3[:, TAIL - keep:, off:off + C_GROUP_WIDTH]))
        kvs.append(layer_kv)
    new_kv = [jnp.stack([kvs[l][j] for l in range(depth)], axis=0) for j in range(2 * C_GROUPS)]
    return x.reshape(bsz, t, D_MODEL), jnp.stack(bufs, axis=0), new_kv


def _sample_trunk(x3, conv_state, caches, p):
    bsz, t, _ = x3.shape
    x = x3.reshape(bsz * t, D_MODEL)
    depth = p["w_in"].shape[0]
    bufs, vns, kvs = [], [], []
    for l in range(depth):
        h = _in_proj_sample(x, p["norm_pre_mix"][l], p["w_in"], l)
        o_a, vn = _token_mix_sample(h, p["a_norm_g"][l], p["a_norm_b"][l], p["a_w_s"][l], p["a_b_s"][l])
        o_b, buf = _conv_sample(h, conv_state[l], p["b_conv_w"][l], p["b_conv_b"][l],
                                p["b_norm_g"][l], p["b_norm_b"][l])
        o_c = _attn_sample(h, caches, l)
        x = _merge_out(x, h, o_a, o_b, o_c, p["w_branch"][l], p["w_out"][l], p["norm_post_mix"][l])
        x = _ffn(x, p["norm_pre_ffn"][l], p["ffn_w_in"][l], p["ffn_w_out"][l], p["norm_post_ffn"][l])
        bufs.append(buf)
        vns.append(vn.reshape(bsz, t, A_WIDTH))
        h3 = h.reshape(bsz, t, D_IN)
        kvs.append([_split_heads(h3[:, :, off + g * C_GROUP_WIDTH:off + (g + 1) * C_GROUP_WIDTH])
                    for g in range(C_GROUPS) for off in (OFF_CK, OFF_CV)])
    new_kv = [jnp.stack([kvs[l][j] for l in range(depth)], axis=0) for j in range(2 * C_GROUPS)]
    return x.reshape(bsz, t, D_MODEL), jnp.stack(bufs, axis=0), jnp.stack(vns, axis=0), new_kv


def kernel(x_prompt, x_sample, state_b_conv, cache_c0_k, cache_c0_v, cache_c1_k, cache_c1_v, cache_c2_k,
           cache_c2_v, norm_pre_mix, norm_post_mix, norm_pre_ffn, norm_post_ffn, w_in, a_norm_g, a_norm_b,
           a_w_s, a_b_s, b_conv_w, b_conv_b, b_norm_g, b_norm_b, w_branch, w_out, ffn_w_in, ffn_w_out):
    assert x_sample.shape[1] == DEC_SEQ
    p = dict(norm_pre_mix=norm_pre_mix, norm_post_mix=norm_post_mix, norm_pre_ffn=norm_pre_ffn,
             norm_post_ffn=norm_post_ffn, w_in=w_in, a_norm_g=a_norm_g, a_norm_b=a_norm_b,
             a_w_s=a_w_s, a_b_s=a_b_s, b_conv_w=b_conv_w, b_conv_b=b_conv_b, b_norm_g=b_norm_g,
             b_norm_b=b_norm_b, w_branch=w_branch.astype(BF16), w_out=w_out.astype(BF16),
             ffn_w_in=ffn_w_in.astype(BF16), ffn_w_out=ffn_w_out.astype(BF16))
    caches = (cache_c0_k, cache_c0_v, cache_c1_k, cache_c1_v, cache_c2_k, cache_c2_v)
    y_prompt, buf_p, kv_p = _prompt_trunk(x_prompt, p)
    y_sample, buf_s, vn_s, kv_s = _sample_trunk(x_sample, state_b_conv, caches, p)
    return (y_prompt, y_sample, buf_p, buf_s, vn_s, *kv_p, *kv_s)
```
